```python
import math
import jax, jax.numpy as jnp
from jax import lax
import numpy as np

D_MODEL = 1024
BATCH = 2
SEQ = 8192
DEPTH = 2
DEC_BATCH = 32
DEC_SEQ = 1
PAST_LEN = 8192
PAGE_SIZE = 128

D_RNN = D_MODEL
N_LRU_BLOCKS = 16
LRU_BLOCK = D_RNN // N_LRU_BLOCKS
CONV_WIDTH = 4
LRU_C = 8.0
HEAD_DIM = 64
HEADS_PER_GROUP = 8
DILATED_GROUPS = ((128, 1), (512, 4), (2048, 16))
N_GROUPS = len(DILATED_GROUPS)
N_ATT_HEADS = N_GROUPS * HEADS_PER_GROUP
ATT_WIDTH = N_ATT_HEADS * HEAD_DIM
ATT_OUT = HEADS_PER_GROUP * HEAD_DIM
BAND_BLOCK = 128
NUM_BUCKETS = 32
MAX_DISTANCE = 2048
D_FF = -(-8 * D_MODEL // (3 * 256)) * 256
EPS = 1e-6
D_IN = 2 * D_RNN + 3 * ATT_WIDTH
SPLITS = (D_RNN, 2 * D_RNN, 2 * D_RNN + ATT_WIDTH, 2 * D_RNN + 2 * ATT_WIDTH)

kernel_name = "hawk_dilated_hybrid_step"


def rmsnorm(x, g):
    xf = x.astype(jnp.float32)
    y = xf * lax.rsqrt(jnp.mean(xf * xf, axis=-1, keepdims=True) + EPS)
    return (y * g.astype(jnp.float32)).astype(x.dtype)


def t5_bucket(dist):
    max_exact = NUM_BUCKETS // 2
    d_f = jnp.maximum(dist, 1).astype(jnp.float32)
    large = max_exact + (jnp.log(d_f / max_exact) / math.log(MAX_DISTANCE / max_exact)
                         * (NUM_BUCKETS - max_exact)).astype(jnp.int32)
    large = jnp.minimum(large, NUM_BUCKETS - 1)
    return jnp.where(dist < max_exact, dist, large)


def causal_conv(x_ext, w, b):
    T = x_ext.shape[1] - (CONV_WIDTH - 1)
    out = b
    for k in range(CONV_WIDTH):
        out = out + w[k] * x_ext[:, k:k + T]
    return out


def rg_lru(x, pos, h0, w_a, b_a, w_x, b_x, lam):
    B, T, _ = x.shape
    xb = x.reshape(B, T, N_LRU_BLOCKS, LRU_BLOCK)
    gate_a = jnp.einsum('btni,nij->btnj', xb, w_a).reshape(B, T, D_RNN) + b_a
    gate_x = jnp.einsum('btni,nij->btnj', xb, w_x).reshape(B, T, D_RNN) + b_x
    r = jax.nn.sigmoid(gate_a.astype(jnp.float32))
    i = jax.nn.sigmoid(gate_x.astype(jnp.float32))
    log_a = -LRU_C * r * jax.nn.softplus(-lam.astype(jnp.float32))
    a = jnp.exp(log_a)
    mult = jnp.sqrt(-jnp.expm1(2.0 * log_a))
    mult = jnp.where((pos == 0)[None, :, None], 1.0, mult)
    u = mult * i * x.astype(jnp.float32)

    def step(h, au):
        a_t, u_t = au
        h = a_t * h + u_t
        return h, h

    h_T, hs = lax.scan(step, h0.astype(jnp.float32),
                       (jnp.swapaxes(a, 0, 1), jnp.swapaxes(u, 0, 1)))
    return jnp.swapaxes(hs, 0, 1).astype(x.dtype), h_T


def dilated_attn_prompt(q, k, v, tab, window, dil):
    B, S, H, E = q.shape
    BB = BAND_BLOCK
    band = window // dil
    n = S // dil
    nb = -(-n // BB)
    n_pad = nb * BB

    def to_res(t):
        t = jnp.swapaxes(t.reshape(B, n, dil, H, E), 1, 2)
        return jnp.pad(t, ((0, 0), (0, 0), (0, n_pad - n), (0, 0), (0, 0)))

    def key_blocks(t):
        tp = jnp.pad(t, ((0, 0), (0, 0), (BB, 0), (0, 0), (0, 0))).reshape(B, dil, nb + 1, BB, H, E)
        return jnp.concatenate([tp[:, :, :-1], tp[:, :, 1:]], axis=3)

    qb = to_res(q).reshape(B, dil, nb, BB, H, E)
    kb = key_blocks(to_res(k))
    vb = key_blocks(to_res(v))
    s = jnp.einsum('brcqhe,brckhe->brchqk', qb, kb,
                   preferred_element_type=jnp.float32) * (HEAD_DIM ** -0.5)
    qi = jnp.arange(BB)[:, None]
    ki = jnp.arange(2 * BB)[None, :]
    sub_dist = qi + BB - ki
    in_band = (sub_dist >= 0) & (sub_dist <= band)
    bias = tab[t5_bucket(jnp.clip(sub_dist, 0, band) * dil)].astype(jnp.float32)
    bias = jnp.transpose(bias, (2, 0, 1))
    key_sub = jnp.arange(nb)[:, None, None] * BB - BB + ki[None]
    valid = in_band[None] & (key_sub >= 0)
    s = jnp.where(valid[:, None], s + bias, -jnp.inf)
    m = jnp.max(s, axis=-1)
    p = jnp.exp(s - m[..., None])
    l = jnp.sum(p, axis=-1)
    o = jnp.einsum('brchqk,brckhe->brcqhe', p, vb.astype(jnp.float32))
    m = jnp.swapaxes(m, 3, 4)
    l = jnp.swapaxes(l, 3, 4)
    o = o / l[..., None]

    def from_res(t):
        t = t.reshape((B, dil, n_pad) + t.shape[4:])[:, :, :n]
        t = jnp.swapaxes(t, 1, 2)
        return t.reshape((B, S) + t.shape[3:])

    return from_res(o), from_res(m), from_res(l)


def dilated_attn_sample(q, k_all, v_all, tab, window, dil):
    B, T, H, E = q.shape
    L = k_all.shape[1] - T
    kk = jnp.arange(window // dil + 1)
    idx = L + jnp.arange(T)[:, None] - kk[None, :] * dil
    valid = idx >= 0
    idx_c = jnp.maximum(idx, 0)
    kg = k_all[:, idx_c]
    vg = v_all[:, idx_c]
    s = jnp.einsum('bthe,btkhe->bthk', q, kg,
                   preferred_element_type=jnp.float32) * (HEAD_DIM ** -0.5)
    bias = tab[t5_bucket(kk * dil)].astype(jnp.float32).T
    s = jnp.where(valid[:, None, :], s + bias, -jnp.inf)
    m = jnp.max(s, axis=-1)
    p = jnp.exp(s - m[..., None])
    l = jnp.sum(p, axis=-1)
    o = jnp.einsum('bthk,btkhe->bthe', p, vg.astype(jnp.float32)) / l[..., None]
    return o, m, l


def combine_groups(outs):
    m_max = jnp.max(jnp.stack([m for _, m, _ in outs]), axis=0)
    num = 0.0
    den = 0.0
    for o, m, l in outs:
        z = l * jnp.exp(m - m_max)
        num = num + z[..., None] * o
        den = den + z
    return num / den[..., None]


def trunk_layer(x, pos, conv_prefix, h0, kv_bufs, rel_bias, norm_mix, w_in, w_conv, b_conv,
                w_rg_a, b_rg_a, w_rg_x, b_rg_x, lru_lambda, q_gain, k_gain, w_merge, b_merge,
                w_branch_lru, w_branch_att, w_o, norm_ffn, w_ffn_in, w_ffn_out):
    B, T, _ = x.shape
    xn = rmsnorm(x, norm_mix)
    lru_x, lru_g, q, k, v = jnp.split(xn @ w_in, SPLITS, axis=-1)

    if conv_prefix is None:
        conv_prefix = jnp.zeros((B, CONV_WIDTH - 1, D_RNN), lru_x.dtype)
    x_ext = jnp.concatenate([conv_prefix.astype(lru_x.dtype), lru_x], axis=1)
    new_conv = x_ext[:, -(CONV_WIDTH - 1):]
    xc = causal_conv(x_ext, w_conv, b_conv)
    hs, h_T = rg_lru(xc, pos, h0, w_rg_a, b_rg_a, w_rg_x, b_rg_x, lru_lambda)
    y_lru = (hs * jax.nn.gelu(lru_g)) @ w_branch_lru

    q = rmsnorm(q.reshape(B, T, N_ATT_HEADS, HEAD_DIM), q_gain)
    k = rmsnorm(k.reshape(B, T, N_ATT_HEADS, HEAD_DIM), k_gain)
    v = v.reshape(B, T, N_ATT_HEADS, HEAD_DIM)
    outs, new_kv = [], []
    for g, (window, dil) in enumerate(DILATED_GROUPS):
        hsl = slice(g * HEADS_PER_GROUP, (g + 1) * HEADS_PER_GROUP)
        qg, kg, vg = q[:, :, hsl], k[:, :, hsl], v[:, :, hsl]
        tab = rel_bias[:, hsl]
        if kv_bufs is None:
            outs.append(dilated_attn_prompt(qg, kg, vg, tab, window, dil))
            keep = min(window, T)
            new_kv.append(jnp.stack([kg[:, T - keep:], vg[:, T - keep:]], axis=2))
        else:
            buf = kv_bufs[g].astype(kg.dtype)
            k_all = jnp.concatenate([buf[:, :, 0], kg], axis=1)
            v_all = jnp.concatenate([buf[:, :, 1], vg], axis=1)
            outs.append(dilated_attn_sample(qg, k_all, v_all, tab, window, dil))
            new_kv.append(jnp.stack([kg, vg], axis=2))
    o = combine_groups(outs).astype(x.dtype).reshape(B, T, ATT_OUT)
    y_att = o @ w_branch_att

    g_lru, g_att = jnp.split(jax.nn.sigmoid(xn @ w_merge + b_merge), 2, axis=-1)
    x = x + (g_lru * y_lru + g_att * y_att) @ w_o

    gate, up = jnp.split(rmsnorm(x, norm_ffn) @ w_ffn_in, 2, axis=-1)
    x = x + (jax.nn.silu(gate) * up) @ w_ffn_out
    return x, new_conv, h_T, new_kv


def setup_inputs(seed: int = 0) -> dict:
    key = jax.random.key(seed)
    ks = iter(jax.random.split(key, 40))
    nrm = lambda shape, scale: jax.random.normal(next(ks), shape, jnp.float32) * scale
    inp = {}
    inp["x_prompt"] = nrm((BATCH, SEQ, D_MODEL), 1.0)
    inp["x_sample"] = nrm((DEC_BATCH, DEC_SEQ, D_MODEL), 1.0)
    for g, (window, dil) in enumerate(DILATED_GROUPS):
        L = min(window, PAST_LEN)
        inp["cache_kv_g%d" % g] = nrm((DEPTH, DEC_BATCH, L, 2, HEADS_PER_GROUP, HEAD_DIM), 1.0)
    inp["state_conv"] = nrm((DEPTH, DEC_BATCH, CONV_WIDTH - 1, D_RNN), 1.0)
    inp["state_h"] = nrm((DEPTH, DEC_BATCH, D_RNN), 0.5)
    inp["rel_bias"] = nrm((NUM_BUCKETS, N_ATT_HEADS), 0.5)
    inp["norm_mix"] = 1.0 + nrm((DEPTH, D_MODEL), 0.01)
    inp["w_in"] = nrm((DEPTH, D_MODEL, D_IN), D_MODEL ** -0.5)
    inp["w_conv"] = nrm((DEPTH, CONV_WIDTH, D_RNN), CONV_WIDTH ** -0.5)
    inp["b_conv"] = nrm((DEPTH, D_RNN), 0.01)
    inp["w_rg_a"] = nrm((DEPTH, N_LRU_BLOCKS, LRU_BLOCK, LRU_BLOCK), LRU_BLOCK ** -0.5)
    inp["b_rg_a"] = nrm((DEPTH, D_RNN), 0.01)
    inp["w_rg_x"] = nrm((DEPTH, N_LRU_BLOCKS, LRU_BLOCK, LRU_BLOCK), LRU_BLOCK ** -0.5)
    inp["b_rg_x"] = nrm((DEPTH, D_RNN), 0.01)
    a_c = jax.random.uniform(next(ks), (DEPTH, D_RNN), jnp.float32, 0.9, 0.999)
    a = a_c ** (1.0 / LRU_C)
    inp["lru_lambda"] = jnp.log(a) - jnp.log1p(-a)
    inp["q_gain"] = 1.0 + nrm((DEPTH, HEAD_DIM), 0.01)
    inp["k_gain"] = 1.0 + nrm((DEPTH, HEAD_DIM), 0.01)
    inp["w_merge"] = nrm((DEPTH, D_MODEL, 2 * D_MODEL), D_MODEL ** -0.5)
    inp["b_merge"] = nrm((DEPTH, 2 * D_MODEL), 0.01)
    inp["w_branch_lru"] = nrm((DEPTH, D_RNN, D_MODEL), D_RNN ** -0.5)
    inp["w_branch_att"] = nrm((DEPTH, ATT_OUT, D_MODEL), ATT_OUT ** -0.5)
    inp["w_o"] = nrm((DEPTH, D_MODEL, D_MODEL), D_MODEL ** -0.5)
    inp["norm_ffn"] = 1.0 + nrm((DEPTH, D_MODEL), 0.01)
    inp["w_ffn_in"] = nrm((DEPTH, D_MODEL, 2 * D_FF), D_MODEL ** -0.5)
    inp["w_ffn_out"] = nrm((DEPTH, D_FF, D_MODEL), D_FF ** -0.5)
    return inp


def reference(x_prompt, x_sample, cache_kv_g0, cache_kv_g1, cache_kv_g2, state_conv, state_h,
              rel_bias, norm_mix, w_in, w_conv, b_conv, w_rg_a, b_rg_a, w_rg_x, b_rg_x, lru_lambda,
              q_gain, k_gain, w_merge, b_merge, w_branch_lru, w_branch_att, w_o, norm_ffn,
              w_ffn_in, w_ffn_out):
    pos_p = jnp.arange(x_prompt.shape[1])
    pos_s = PAST_LEN + jnp.arange(x_sample.shape[1])

    def layer_weights(l):
        return (norm_mix[l], w_in[l], w_conv[l], b_conv[l], w_rg_a[l], b_rg_a[l], w_rg_x[l],
                b_rg_x[l], lru_lambda[l], q_gain[l], k_gain[l], w_merge[l], b_merge[l],
                w_branch_lru[l], w_branch_att[l], w_o[l], norm_ffn[l], w_ffn_in[l], w_ffn_out[l])

    yp, ys = x_prompt, x_sample
    kvp = [[], [], []]
    kvs = [[], [], []]
    conv_p, h_p, conv_s, h_s = [], [], [], []
    for l in range(DEPTH):
        lw = layer_weights(l)
        h0 = jnp.zeros((x_prompt.shape[0], D_RNN), jnp.float32)
        yp, c_p, hp_l, kv_p = trunk_layer(yp, pos_p, None, h0, None, rel_bias, *lw)
        ys, c_s, hs_l, kv_s = trunk_layer(ys, pos_s, state_conv[l], state_h[l],
                                          (cache_kv_g0[l], cache_kv_g1[l], cache_kv_g2[l]),
                                          rel_bias, *lw)
        for g in range(N_GROUPS):
            kvp[g].append(kv_p[g])
            kvs[g].append(kv_s[g])
        conv_p.append(c_p)
        h_p.append(hp_l)
        conv_s.append(c_s)
        h_s.append(hs_l)
    kv_g0_p, kv_g1_p, kv_g2_p = (jnp.stack(kvp[0]), jnp.stack(kvp[1]), jnp.stack(kvp[2]))
    kv_g0_s, kv_g1_s, kv_g2_s = (jnp.stack(kvs[0]), jnp.stack(kvs[1]), jnp.stack(kvs[2]))
    conv_p_out = jnp.stack(conv_p)
    h_p_out = jnp.stack(h_p)
    conv_s_out = jnp.stack(conv_s)
    h_s_out = jnp.stack(h_s)
    return (yp, ys, kv_g0_p, kv_g1_p, kv_g2_p, conv_p_out, h_p_out,
            kv_g0_s, kv_g1_s, kv_g2_s, conv_s_out, h_s_out)
```

```python
import functools
import math

import numpy as np
import jax
import jax.numpy as jnp
from jax import lax
from jax.experimental import pallas as pl
from jax.experimental.pallas import tpu as pltpu

D_MODEL = 1024
D_RNN = D_MODEL
N_LRU_BLOCKS = 16
LRU_BLOCK = D_RNN // N_LRU_BLOCKS
CONV_WIDTH = 4
LRU_C = 8.0
HEAD_DIM = 64
HEADS_PER_GROUP = 8
DILATED_GROUPS = ((128, 1), (512, 4), (2048, 16))
N_GROUPS = len(DILATED_GROUPS)
N_ATT_HEADS = N_GROUPS * HEADS_PER_GROUP
ATT_WIDTH = N_ATT_HEADS * HEAD_DIM
GROUP_WIDTH = HEADS_PER_GROUP * HEAD_DIM
BAND_BLOCK = 128
NUM_BUCKETS = 32
MAX_DISTANCE = 2048
D_FF = 2816
EPS = 1e-6
PAST_LEN = 8192

F32 = jnp.float32
BF16 = jnp.bfloat16

MXU_DIM = 256
SUBLANES = 8
VMEM_LIMIT_BYTES = 56 * 1024 * 1024

TM_PROMPT = 256
RG_TILE = MXU_DIM
FFN_CHUNKS = ((0, 1536), (1536, 1280))


def _dot(a, b):
    return jnp.dot(a, b, preferred_element_type=F32)


def _dot_nt(a, b):
    return lax.dot_general(a, b, (((1,), (1,)), ((), ())), preferred_element_type=F32)


def _split_dot(x, m):
    hi = x.astype(BF16)
    lo = (x - hi.astype(F32)).astype(BF16)
    return _dot(hi, m) + _dot(lo, m)


def _sigmoid(x):
    return 0.5 * (jnp.tanh(0.5 * x) + 1.0)


def _gelu_tanh(x):
    c = math.sqrt(2.0 / math.pi)
    return 0.5 * x * (1.0 + jnp.tanh(c * (x + 0.044715 * (x * x * x))))


def _softplus(z):
    return jnp.maximum(z, 0.0) + jnp.log1p(jnp.exp(-jnp.abs(z)))


def _rms_rows(x, g):
    y = x * lax.rsqrt(jnp.mean(x * x, axis=-1, keepdims=True) + EPS)
    return y * g


def _head_rms(t, pn, gain):
    t2 = t * t
    halves = []
    for c in range(GROUP_WIDTH // MXU_DIM):
        halves.append(_split_dot(t2[:, c * MXU_DIM:(c + 1) * MXU_DIM], pn))
    ms = jnp.concatenate(halves, axis=1)
    return (t * lax.rsqrt(ms + EPS)) * gain


def _lru_gates(xc_t, g, b_a, b_x, sp):
    w = g.shape[1] // 2
    r = _sigmoid(g[:, :w] + b_a)
    ig = _sigmoid(g[:, w:] + b_x)
    log_a = (-LRU_C * r) * sp
    a = jnp.exp(log_a)
    mult = jnp.sqrt(-jnp.tanh(log_a) * (a * a + 1.0))
    return a, mult, ig


def _scan_rows(a, u, h_in):
    rows, c = a.shape
    groups = rows // SUBLANES
    a3 = a.reshape(groups, SUBLANES, c)
    u3 = u.reshape(groups, SUBLANES, c)
    row = lax.broadcasted_iota(jnp.int32, (groups, SUBLANES, c), 1)
    shift = 1
    while shift < SUBLANES:
        ok = row >= shift
        a_sh = jnp.where(ok, pltpu.roll(a3, shift, axis=1), 1.0)
        u_sh = jnp.where(ok, pltpu.roll(u3, shift, axis=1), 0.0)
        u3 = u3 + a3 * u_sh
        a3 = a3 * a_sh
        shift *= 2
    out = []
    h = h_in
    for g in range(groups):
        hg = a3[g] * h + u3[g]
        out.append(hg)
        h = hg[SUBLANES - 1:SUBLANES, :]
    return jnp.concatenate(out, axis=0), h


def _front_kernel(x_ref, nm_ref, win_ref, wmg_ref, bmg_ref, wconv_ref, bconv_ref, wrg_ref,
                  brga_ref, brgx_ref, lam_ref, qg_ref, kg_ref, pn_ref, wbl_ref,
                  q0_ref, q1_ref, q2_ref, k0_ref, k1_ref, k2_ref, v0_ref, v1_ref, v2_ref,
                  ylg_ref, gatt_ref, kvp0_ref, kvp1_ref, kvp2_ref, convp_ref, hp_ref,
                  ext_ref, h_ref, *, tm):
    i = pl.program_id(1)

    @pl.when(i == 0)
    def _():
        ext_ref[0:SUBLANES, :] = jnp.zeros((SUBLANES, D_RNN), F32)
        h_ref[...] = jnp.zeros((SUBLANES, D_RNN), F32)

    xb = _rms_rows(x_ref[...], nm_ref[...]).astype(BF16)

    lru_x = _dot(xb, win_ref[:, 0:D_RNN])
    ext_ref[SUBLANES:SUBLANES + tm, :] = lru_x
    xc = bconv_ref[...] + wconv_ref[0:1, :] * ext_ref[SUBLANES - 3:SUBLANES - 3 + tm, :]
    xc = xc + wconv_ref[1:2, :] * ext_ref[SUBLANES - 2:SUBLANES - 2 + tm, :]
    xc = xc + wconv_ref[2:3, :] * ext_ref[SUBLANES - 1:SUBLANES - 1 + tm, :]
    xc = xc + wconv_ref[3:4, :] * lru_x
    tail = ext_ref[tm:tm + SUBLANES, :]
    ext_ref[0:SUBLANES, :] = tail
    convp_ref[...] = tail

    lru_g = _dot(xb, win_ref[:, D_RNN:2 * D_RNN])
    xcb = xc.astype(BF16)
    sp = _softplus(-lam_ref[...])
    first_row = (lax.broadcasted_iota(jnp.int32, (tm, RG_TILE), 0) == 0) & (i == 0)
    ys = []
    for t in range(D_RNN // RG_TILE):
        sl = slice(t * RG_TILE, (t + 1) * RG_TILE)
        g = _dot(xcb[:, sl], wrg_ref[t])
        a, mult, ig = _lru_gates(xc[:, sl], g, brga_ref[:, sl], brgx_ref[:, sl], sp[:, sl])
        mult = jnp.where(first_row, 1.0, mult)
        u = (mult * ig) * xc[:, sl]
        hs, h_last = _scan_rows(a, u, h_ref[0:1, sl])
        h_ref[:, sl] = jnp.broadcast_to(h_last, (SUBLANES, RG_TILE))
        ys.append((hs * _gelu_tanh(lru_g[:, sl])).astype(BF16))
    hp_ref[...] = h_ref[...]
    y_lru = _dot(jnp.concatenate(ys, axis=1), wbl_ref[...])
    g_lru = _sigmoid(_dot(xb, wmg_ref[:, 0:D_MODEL]) + bmg_ref[:, 0:D_MODEL])
    ylg_ref[...] = g_lru * y_lru
    gatt_ref[...] = _sigmoid(_dot(xb, wmg_ref[:, D_MODEL:2 * D_MODEL]) + bmg_ref[:, D_MODEL:2 * D_MODEL])

    pn = pn_ref[...]
    q_refs = (q0_ref, q1_ref, q2_ref)
    k_refs = (k0_ref, k1_ref, k2_ref)
    v_refs = (v0_ref, v1_ref, v2_ref)
    kvp_refs = (kvp0_ref, kvp1_ref, kvp2_ref)
    for g, (window, _) in enumerate(DILATED_GROUPS):
        c = 2 * D_RNN + g * GROUP_WIDTH
        q = _dot(xb, win_ref[:, c:c + GROUP_WIDTH])
        k = _dot(xb, win_ref[:, c + ATT_WIDTH:c + ATT_WIDTH + GROUP_WIDTH])
        v = _dot(xb, win_ref[:, c + 2 * ATT_WIDTH:c + 2 * ATT_WIDTH + GROUP_WIDTH])
        qn = _head_rms(q, pn, qg_ref[...])
        kn = _head_rms(k, pn, kg_ref[...])
        q_refs[g][...] = (qn * (HEAD_DIM ** -0.5)).astype(BF16)
        k_refs[g][...] = kn.astype(BF16)
        v_refs[g][...] = v.astype(BF16)
        keep = min(window, tm)
        kvp_refs[g][:, 0:GROUP_WIDTH] = kn[tm - keep:, :]
        kvp_refs[g][:, GROUP_WIDTH:2 * GROUP_WIDTH] = v[tm - keep:, :]


def _const_spec(shape):
    zeros = (0,) * len(shape)
    return pl.BlockSpec(shape, lambda *_: zeros, pipeline_mode=pl.Buffered(1))


def _front_call(x, lw, tm):
    B, S, _ = x.shape
    nt = S // tm
    row_spec = lambda width: pl.BlockSpec((None, tm, width), lambda b, i: (b, i, 0))
    consts = (lw["norm_mix"], lw["w_in"], lw["w_merge"], lw["b_merge"], lw["w_conv"], lw["b_conv"],
              lw["w_rg"], lw["b_rg_a"], lw["b_rg_x"], lw["lam"], lw["q_gain"], lw["k_gain"],
              lw["pnorm"], lw["w_branch_lru"])
    in_specs = [row_spec(D_MODEL)] + [_const_spec(c.shape) for c in consts]

    out_shape, out_specs = [], []
    for _ in range(9):
        out_shape.append(jax.ShapeDtypeStruct((B, S, GROUP_WIDTH), BF16))
        out_specs.append(row_spec(GROUP_WIDTH))
    out_shape.append(jax.ShapeDtypeStruct((B, S, D_MODEL), F32))
    out_specs.append(row_spec(D_MODEL))
    out_shape.append(jax.ShapeDtypeStruct((B, S, D_MODEL), F32))
    out_specs.append(row_spec(D_MODEL))
    for window, _ in DILATED_GROUPS:
        keep = min(window, tm)
        first = nt - window // keep
        out_shape.append(jax.ShapeDtypeStruct((B, window, 2 * GROUP_WIDTH), F32))
        out_specs.append(pl.BlockSpec(
            (None, keep, 2 * GROUP_WIDTH),
            functools.partial(lambda b, i, first: (b, jnp.maximum(i - first, 0), 0), first=first)))
    for _ in range(2):
        out_shape.append(jax.ShapeDtypeStruct((B, SUBLANES, D_RNN), F32))
        out_specs.append(pl.BlockSpec((None, SUBLANES, D_RNN), lambda b, i: (b, 0, 0)))

    return pl.pallas_call(
        functools.partial(_front_kernel, tm=tm),
        grid=(B, nt),
        in_specs=in_specs,
        out_specs=out_specs,
        out_shape=out_shape,
        scratch_shapes=[pltpu.VMEM((tm + SUBLANES, D_RNN), F32), pltpu.VMEM((SUBLANES, D_RNN), F32)],
        compiler_params=pltpu.CompilerParams(
            dimension_semantics=("arbitrary", "arbitrary"), vmem_limit_bytes=VMEM_LIMIT_BYTES),
        name="prompt_front",
    )(x, *consts)


def _attn_kernel(tab_ref, bkt_ref, q_ref, kp_ref, kc_ref, vp_ref, vc_ref, o_ref, lse_ref, bias_ref):
    b, r, j = pl.program_id(0), pl.program_id(1), pl.program_id(2)
    BB = BAND_BLOCK

    @pl.when((b == 0) & (r == 0) & (j == 0))
    def _():
        bkt = bkt_ref[...]
        for h in range(HEADS_PER_GROUP):
            acc = jnp.full(bkt.shape, -jnp.inf, F32)
            for n in range(NUM_BUCKETS):
                acc = jnp.where(bkt == n, tab_ref[n, h], acc)
            bias_ref[h] = acc

    pen = jnp.where(j == 0, -jnp.inf, 0.0).astype(F32)
    low_half = lax.broadcasted_iota(jnp.int32, (BB, 2 * HEAD_DIM), 1) < HEAD_DIM
    for hp in range(HEADS_PER_GROUP // 2):
        sl = slice(hp * 2 * HEAD_DIM, (hp + 1) * 2 * HEAD_DIM)
        qp = q_ref[:, sl]
        kp, kc, vp, vc = kp_ref[:, sl], kc_ref[:, sl], vp_ref[:, sl], vc_ref[:, sl]
        o_pair = None
        lse_pair = None
        for e in range(2):
            h = 2 * hp + e
            mine = low_half if e == 0 else jnp.logical_not(low_half)
            qm = jnp.where(mine, qp, jnp.zeros_like(qp))
            s_p = _dot_nt(qm, kp) + bias_ref[h, :, 0:BB] + pen
            s_c = _dot_nt(qm, kc) + bias_ref[h, :, BB:2 * BB]
            m = jnp.maximum(jnp.max(s_p, axis=-1, keepdims=True), jnp.max(s_c, axis=-1, keepdims=True))
            p_p = jnp.exp(s_p - m)
            p_c = jnp.exp(s_c - m)
            l = jnp.sum(p_p, axis=-1, keepdims=True) + jnp.sum(p_c, axis=-1, keepdims=True)
            o = (_dot(p_p.astype(BF16), vp) + _dot(p_c.astype(BF16), vc)) / l
            lse = jnp.broadcast_to(m + jnp.log(l), o.shape)
            o_pair = o if e == 0 else jnp.where(low_half, o_pair, o)
            lse_pair = lse if e == 0 else jnp.where(low_half, lse_pair, lse)
        o_ref[:, sl] = o_pair
        lse_ref[:, sl] = lse_pair


def _band_buckets(dil):
    BB = BAND_BLOCK
    qi = jnp.arange(BB)[:, None]
    ki = jnp.arange(2 * BB)[None, :]
    sub = qi + BB - ki
    in_band = (sub >= 0) & (sub <= BB)
    return jnp.where(in_band, _t5_bucket(jnp.clip(sub, 0, BB) * dil), -1).astype(jnp.int32)


def _t5_bucket(dist):
    max_exact = NUM_BUCKETS // 2
    d_f = jnp.maximum(dist, 1).astype(F32)
    large = max_exact + (jnp.log(d_f / max_exact) / math.log(MAX_DISTANCE / max_exact)
                         * (NUM_BUCKETS - max_exact)).astype(jnp.int32)
    large = jnp.minimum(large, NUM_BUCKETS - 1)
    return jnp.where(dist < max_exact, dist, large)


def _attn_call(q, k, v, tab, dil):
    B, S, C = q.shape
    BB = BAND_BLOCK
    n = S // dil
    nb = n // BB
    view = lambda t: t.reshape(B, n, dil * C)
    cur = pl.BlockSpec((None, BB, C), lambda b, r, j: (b, j, r))
    prev = pl.BlockSpec((None, BB, C), lambda b, r, j: (b, jnp.maximum(j - 1, 0), r))
    o, lse = pl.pallas_call(
        _attn_kernel,
        grid=(B, dil, nb),
        in_specs=[pl.BlockSpec(memory_space=pltpu.SMEM),
                  pl.BlockSpec((BB, 2 * BB), lambda b, r, j: (0, 0)),
                  cur, prev, cur, prev, cur],
        out_specs=[cur, cur],
        out_shape=[jax.ShapeDtypeStruct((B, n, dil * C), F32)] * 2,
        scratch_shapes=[pltpu.VMEM((HEADS_PER_GROUP, BB, 2 * BB), F32)],
        compiler_params=pltpu.CompilerParams(
            dimension_semantics=("arbitrary", "arbitrary", "arbitrary")),
        name="prompt_attn_d%d" % dil,
    )(tab, _band_buckets(dil), view(q), view(k), view(k), view(v), view(v))
    return o.reshape(B, S, C), lse.reshape(B, S, C)


def _merge_groups(o_refs, lse_refs):
    lses = [r[...] for r in lse_refs]
    top = jnp.maximum(jnp.maximum(lses[0], lses[1]), lses[2])
    num = 0.0
    den = 0.0
    for o_ref, lse in zip(o_refs, lses):
        z = jnp.exp(lse - top)
        num = num + z * o_ref[...]
        den = den + z
    return num / den


def _back_kernel(x_ref, ylg_ref, gatt_ref, o0_ref, o1_ref, o2_ref, l0_ref, l1_ref, l2_ref,
                 wba_ref, wo_ref, nf_ref, wfi_ref, wfo_ref, y_ref):
    o = _merge_groups((o0_ref, o1_ref, o2_ref), (l0_ref, l1_ref, l2_ref))
    y_att = _dot(o.astype(BF16), wba_ref[...])
    mix = ylg_ref[...] + gatt_ref[...] * y_att
    x1 = x_ref[...] + _dot(mix.astype(BF16), wo_ref[...])
    xb = _rms_rows(x1, nf_ref[...]).astype(BF16)
    acc = x1
    for start, width in FFN_CHUNKS:
        gate = _dot(xb, wfi_ref[:, start:start + width])
        up = _dot(xb, wfi_ref[:, D_FF + start:D_FF + start + width])
        hid = (gate * _sigmoid(gate)) * up
        acc = acc + _dot(hid.astype(BF16), wfo_ref[start:start + width, :])
    y_ref[...] = acc


def _back_call(x, ylg, gatt, os_, lses, lw, tm):
    M = x.shape[0]
    row_spec = lambda width: pl.BlockSpec((tm, width), lambda i: (i, 0))
    consts = (lw["w_branch_att"], lw["w_o"], lw["norm_ffn"], lw["w_ffn_in"], lw["w_ffn_out"])
    in_specs = ([row_spec(D_MODEL)] * 3 + [row_spec(GROUP_WIDTH)] * 6
                + [_const_spec(c.shape) for c in consts])
    return pl.pallas_call(
        _back_kernel,
        grid=(M // tm,),
        in_specs=in_specs,
        out_specs=row_spec(D_MODEL),
        out_shape=jax.ShapeDtypeStruct((M, D_MODEL), F32),
        compiler_params=pltpu.CompilerParams(
            dimension_semantics=("arbitrary",), vmem_limit_bytes=VMEM_LIMIT_BYTES),
        name="back_m%d" % M,
    )(x, ylg, gatt, *os_, *lses, *consts)


def _dec_front_kernel(x_ref, c0_ref, c1_ref, c2_ref, h0_ref, nm_ref, win_ref, wmg_ref, bmg_ref,
                      wconv_ref, bconv_ref, wrg_ref, brga_ref, brgx_ref, lam_ref, qg_ref, kg_ref,
                      pn_ref, wbl_ref,
                      q_ref, k_ref, v_ref, ylg_ref, gatt_ref, lrux_ref, hs_ref):
    xb = _rms_rows(x_ref[...], nm_ref[...]).astype(BF16)
    lru_x = _dot(xb, win_ref[:, 0:D_RNN])
    lrux_ref[...] = lru_x
    xc = bconv_ref[...] + wconv_ref[0:1, :] * c0_ref[...]
    xc = xc + wconv_ref[1:2, :] * c1_ref[...]
    xc = xc + wconv_ref[2:3, :] * c2_ref[...]
    xc = xc + wconv_ref[3:4, :] * lru_x

    lru_g = _dot(xb, win_ref[:, D_RNN:2 * D_RNN])
    xcb = xc.astype(BF16)
    sp = _softplus(-lam_ref[...])
    ys = []
    for t in range(D_RNN // RG_TILE):
        sl = slice(t * RG_TILE, (t + 1) * RG_TILE)
        g = _dot(xcb[:, sl], wrg_ref[t])
        a, mult, ig = _lru_gates(xc[:, sl], g, brga_ref[:, sl], brgx_ref[:, sl], sp[:, sl])
        h = a * h0_ref[:, sl] + (mult * ig) * xc[:, sl]
        hs_ref[:, sl] = h
        ys.append((h * _gelu_tanh(lru_g[:, sl])).astype(BF16))
    y_lru = _dot(jnp.concatenate(ys, axis=1), wbl_ref[...])
    g_lru = _sigmoid(_dot(xb, wmg_ref[:, 0:D_MODEL]) + bmg_ref[:, 0:D_MODEL])
    ylg_ref[...] = g_lru * y_lru
    gatt_ref[...] = _sigmoid(_dot(xb, wmg_ref[:, D_MODEL:2 * D_MODEL]) + bmg_ref[:, D_MODEL:2 * D_MODEL])

    pn = pn_ref[...]
    for g in range(N_GROUPS):
        c = 2 * D_RNN + g * GROUP_WIDTH
        sl = slice(g * GROUP_WIDTH, (g + 1) * GROUP_WIDTH)
        q = _dot(xb, win_ref[:, c:c + GROUP_WIDTH])
        k = _dot(xb, win_ref[:, c + ATT_WIDTH:c + ATT_WIDTH + GROUP_WIDTH])
        v = _dot(xb, win_ref[:, c + 2 * ATT_WIDTH:c + 2 * ATT_WIDTH + GROUP_WIDTH])
        q_ref[:, sl] = _head_rms(q, pn, qg_ref[...]) * (HEAD_DIM ** -0.5)
        k_ref[:, sl] = _head_rms(k, pn, kg_ref[...])
        v_ref[:, sl] = v


def _dec_front_call(x, conv_rows, h0, lw):
    nb = x.shape[0]
    consts = (lw["norm_mix"], lw["w_in"], lw["w_merge"], lw["b_merge"], lw["w_conv"], lw["b_conv"],
              lw["w_rg"], lw["b_rg_a"], lw["b_rg_x"], lw["lam"], lw["q_gain"], lw["k_gain"],
              lw["pnorm"], lw["w_branch_lru"])
    acts = (x,) + tuple(conv_rows) + (h0,)
    widths = (ATT_WIDTH, ATT_WIDTH, ATT_WIDTH, D_MODEL, D_MODEL, D_RNN, D_RNN)
    return pl.pallas_call(
        _dec_front_kernel,
        grid=(1,),
        in_specs=[_const_spec(a.shape) for a in acts + consts],
        out_specs=[pl.BlockSpec((nb, w), lambda i: (0, 0)) for w in widths],
        out_shape=[jax.ShapeDtypeStruct((nb, w), F32) for w in widths],
        compiler_params=pltpu.CompilerParams(
            dimension_semantics=("arbitrary",), vmem_limit_bytes=VMEM_LIMIT_BYTES),
        name="decode_front",
    )(*acts, *consts)


def _dec_attn_kernel(q_ref, k_ref, v_ref, c0_ref, c1_ref, c2_ref, tab_ref, bkt_ref, ind_ref, indt_ref,
                     o_ref, *, bt):
    NK = BAND_BLOCK
    ind = ind_ref[...]
    indt = indt_ref[...]
    parts = []
    for g, c_ref in enumerate((c0_ref, c1_ref, c2_ref)):
        sl = slice(g * GROUP_WIDTH, (g + 1) * GROUP_WIDTH)
        q = q_ref[:, sl]
        k_new = k_ref[:, sl]
        v_new = v_ref[:, sl]
        kc = c_ref[:, :, 0:GROUP_WIDTH]
        vc = c_ref[:, :, GROUP_WIDTH:2 * GROUP_WIDTH]

        bkt = bkt_ref[g]
        bias = jnp.zeros((NK, 128), F32)
        for n in range(NUM_BUCKETS):
            bias = jnp.where(bkt == n, tab_ref[g, n:n + 1, :], bias)
        bias0 = tab_ref[g, 0:1, :]

        s = _split_dot((kc * q[:, None, :]).reshape(bt * NK, GROUP_WIDTH), ind)
        s = s.reshape(bt, NK, 128) + bias[None]
        s0 = _split_dot(k_new * q, ind) + bias0
        m = jnp.maximum(jnp.max(s, axis=1), s0)
        p = jnp.exp(s - m[:, None, :])
        p0 = jnp.exp(s0 - m)
        l = jnp.sum(p, axis=1) + p0
        pe = _split_dot(p.reshape(bt * NK, 128), indt).reshape(bt, NK, GROUP_WIDTH)
        acc = jnp.sum(pe * vc, axis=1) + _split_dot(p0, indt) * v_new
        parts.append((acc, m, l))

    m_max = jnp.maximum(jnp.maximum(parts[0][1], parts[1][1]), parts[2][1])
    num = 0.0
    den = 0.0
    for acc, m, l in parts:
        w = jnp.exp(m - m_max)
        num = num + _split_dot(w, indt) * acc
        den = den + _split_dot(l * w, indt)
    o_ref[...] = num / den


def _dec_attn_call(q, k, v, caches, layer, tabs, bt):
    nb = q.shape[0]
    NK = BAND_BLOCK
    views = []
    for cache, (window, dil) in zip(caches, DILATED_GROUPS):
        views.append(cache.reshape(cache.shape[0] * nb, window // dil, dil * 2 * GROUP_WIDTH))
    first = layer * (nb // bt)
    bkt = jnp.stack([
        jnp.broadcast_to(_t5_bucket((NK - jnp.arange(NK)) * dil)[:, None], (NK, 128)).astype(jnp.int32)
        for _, dil in DILATED_GROUPS])
    heads = jnp.arange(GROUP_WIDTH) // HEAD_DIM
    ind = (heads[:, None] == jnp.arange(128)[None, :]).astype(BF16)
    row = lambda width: pl.BlockSpec((bt, width), lambda i: (i, 0))
    return pl.pallas_call(
        functools.partial(_dec_attn_kernel, bt=bt),
        grid=(nb // bt,),
        in_specs=[row(ATT_WIDTH)] * 3
        + [pl.BlockSpec((bt, NK, 2 * GROUP_WIDTH), lambda i: (first + i, 0, 0))] * 3
        + [_const_spec(tabs.shape), _const_spec(bkt.shape), _const_spec(ind.shape), _const_spec(ind.T.shape)],
        out_specs=row(GROUP_WIDTH),
        out_shape=jax.ShapeDtypeStruct((nb, GROUP_WIDTH), F32),
        compiler_params=pltpu.CompilerParams(
            dimension_semantics=("arbitrary",), vmem_limit_bytes=VMEM_LIMIT_BYTES),
        name="decode_attn",
    )(q, k, v, *views, tabs, bkt, ind, ind.T)


def _dec_back_kernel(x_ref, ylg_ref, gatt_ref, o_ref, wba_ref, wo_ref, nf_ref, wfi_ref, wfo_ref, y_ref):
    y_att = _dot(o_ref[...].astype(BF16), wba_ref[...])
    mix = ylg_ref[...] + gatt_ref[...] * y_att
    x1 = x_ref[...] + _dot(mix.astype(BF16), wo_ref[...])
    xb = _rms_rows(x1, nf_ref[...]).astype(BF16)
    acc = x1
    for start, width in FFN_CHUNKS:
        gate = _dot(xb, wfi_ref[:, start:start + width])
        up = _dot(xb, wfi_ref[:, D_FF + start:D_FF + start + width])
        hid = (gate * _sigmoid(gate)) * up
        acc = acc + _dot(hid.astype(BF16), wfo_ref[start:start + width, :])
    y_ref[...] = acc


def _dec_back_call(x, ylg, gatt, o, lw):
    consts = (lw["w_branch_att"], lw["w_o"], lw["norm_ffn"], lw["w_ffn_in"], lw["w_ffn_out"])
    acts = (x, ylg, gatt, o)
    return pl.pallas_call(
        _dec_back_kernel,
        grid=(1,),
        in_specs=[_const_spec(a.shape) for a in acts + consts],
        out_specs=pl.BlockSpec(x.shape, lambda i: (0, 0)),
        out_shape=jax.ShapeDtypeStruct(x.shape, F32),
        compiler_params=pltpu.CompilerParams(
            dimension_semantics=("arbitrary",), vmem_limit_bytes=VMEM_LIMIT_BYTES),
        name="decode_back",
    )(*acts, *consts)


def _layer_weights(l, norm_mix, w_in, w_conv, b_conv, w_rg_a, b_rg_a, w_rg_x, b_rg_x, lru_lambda,
                   q_gain, k_gain, w_merge, b_merge, w_branch_lru, w_branch_att, w_o, norm_ffn,
                   w_ffn_in, w_ffn_out):
    row = lambda t: t[l].reshape(1, -1).astype(F32)
    per_tile = RG_TILE // LRU_BLOCK

    def block_diag(w):
        w = w.reshape(D_RNN // RG_TILE, per_tile, LRU_BLOCK, LRU_BLOCK)
        eye = jnp.eye(per_tile, dtype=w.dtype)
        return jnp.einsum("tnij,nm->tnimj", w, eye).reshape(D_RNN // RG_TILE, RG_TILE, RG_TILE)

    w_rg = jnp.concatenate([block_diag(w_rg_a[l]), block_diag(w_rg_x[l])], axis=2).astype(BF16)
    head = jnp.arange(MXU_DIM) // HEAD_DIM
    pnorm = ((head[:, None] == head[None, :]).astype(F32) / HEAD_DIM).astype(BF16)
    return dict(
        norm_mix=row(norm_mix), w_in=w_in[l].astype(BF16), w_merge=w_merge[l].astype(BF16),
        b_merge=row(b_merge), w_conv=w_conv[l].astype(F32), b_conv=row(b_conv), w_rg=w_rg,
        b_rg_a=row(b_rg_a), b_rg_x=row(b_rg_x), lam=row(lru_lambda),
        q_gain=jnp.tile(q_gain[l].reshape(1, HEAD_DIM), (1, HEADS_PER_GROUP)).astype(F32),
        k_gain=jnp.tile(k_gain[l].reshape(1, HEAD_DIM), (1, HEADS_PER_GROUP)).astype(F32),
        pnorm=pnorm, w_branch_lru=w_branch_lru[l].astype(BF16),
        w_branch_att=w_branch_att[l].astype(BF16), w_o=w_o[l].astype(BF16), norm_ffn=row(norm_ffn),
        w_ffn_in=w_ffn_in[l].astype(BF16), w_ffn_out=w_ffn_out[l].astype(BF16))


def kernel(x_prompt, x_sample, cache_kv_g0, cache_kv_g1, cache_kv_g2, state_conv, state_h, rel_bias,
           norm_mix, w_in, w_conv, b_conv, w_rg_a, b_rg_a, w_rg_x, b_rg_x, lru_lambda, q_gain, k_gain,
           w_merge, b_merge, w_branch_lru, w_branch_att, w_o, norm_ffn, w_ffn_in, w_ffn_out):
    B, S, _ = x_prompt.shape
    nb = x_sample.shape[0]
    depth = norm_mix.shape[0]
    caches = (cache_kv_g0, cache_kv_g1, cache_kv_g2)
    rel_bias = rel_bias.astype(F32)
    tabs = jnp.pad(rel_bias.reshape(NUM_BUCKETS, N_GROUPS, HEADS_PER_GROUP).transpose(1, 0, 2),
                   ((0, 0), (0, 0), (0, 128 - HEADS_PER_GROUP)))

    yp = x_prompt
    ys = x_sample.reshape(nb, D_MODEL)
    kvp = [[] for _ in range(N_GROUPS)]
    kvs = [[] for _ in range(N_GROUPS)]
    conv_p, h_p, conv_s, h_s = [], [], [], []
    for l in range(depth):
        lw = _layer_weights(l, norm_mix, w_in, w_conv, b_conv, w_rg_a, b_rg_a, w_rg_x, b_rg_x,
                            lru_lambda, q_gain, k_gain, w_merge, b_merge, w_branch_lru, w_branch_att,
                            w_o, norm_ffn, w_ffn_in, w_ffn_out)

        (q0, q1, q2, k0, k1, k2, v0, v1, v2, ylg, gatt, kv0, kv1, kv2, ctail, hfin) = _front_call(
            yp, lw, TM_PROMPT)
        os_, lses = [], []
        for g, (qg, kg, vg) in enumerate(((q0, k0, v0), (q1, k1, v1), (q2, k2, v2))):
            tab = rel_bias[:, g * HEADS_PER_GROUP:(g + 1) * HEADS_PER_GROUP]
            o, lse = _attn_call(qg, kg, vg, tab, DILATED_GROUPS[g][1])
            os_.append(o.reshape(B * S, GROUP_WIDTH))
            lses.append(lse.reshape(B * S, GROUP_WIDTH))
        yp = _back_call(yp.reshape(B * S, D_MODEL), ylg.reshape(B * S, D_MODEL),
                        gatt.reshape(B * S, D_MODEL), os_, lses, lw, TM_PROMPT).reshape(B, S, D_MODEL)
        for g, kv in enumerate((kv0, kv1, kv2)):
            kvp[g].append(kv.reshape(B, kv.shape[1], 2, HEADS_PER_GROUP, HEAD_DIM))
        conv_p.append(ctail[:, SUBLANES - (CONV_WIDTH - 1):, :])
        h_p.append(hfin[:, 0, :])

        sc = state_conv[l].astype(F32)
        conv_rows = [sc[:, r, :] for r in range(CONV_WIDTH - 1)]
        qs, ks, vs, ylg_s, gatt_s, lrux_s, hs_s = _dec_front_call(ys, conv_rows, state_h[l].astype(F32), lw)
        o_s = _dec_attn_call(qs, ks, vs, caches, l, tabs, bt=SUBLANES)
        ys = _dec_back_call(ys, ylg_s, gatt_s, o_s, lw)
        for g in range(N_GROUPS):
            sl = slice(g * GROUP_WIDTH, (g + 1) * GROUP_WIDTH)
            kvs[g].append(jnp.stack([ks[:, sl], vs[:, sl]], axis=1).reshape(
                nb, 1, 2, HEADS_PER_GROUP, HEAD_DIM))
        conv_s.append(jnp.stack(conv_rows[1:] + [lrux_s], axis=1))
        h_s.append(hs_s)

    stack = jnp.stack
    return (yp, ys.reshape(nb, 1, D_MODEL),
            stack(kvp[0]), stack(kvp[1]), stack(kvp[2]), stack(conv_p), stack(h_p),
            stack(kvs[0]), stack(kvs[1]), stack(kvs[2]), stack(conv_s), stack(h_s))
```

```python
import functools
import math

import jax
import jax.numpy as jnp
from jax import lax
from jax.experimental import pallas as pl
from jax.experimental.pallas import tpu as pltpu

D_MODEL = 1024
D_RNN = D_MODEL
N_LRU_BLOCKS = 16
LRU_BLOCK = D_RNN // N_LRU_BLOCKS
CONV_WIDTH = 4
LRU_C = 8.0
HEAD_DIM = 64
HEADS_PER_GROUP = 8
DILATED_GROUPS = ((128, 1), (512, 4), (2048, 16))
N_GROUPS = len(DILATED_GROUPS)
N_ATT_HEADS = N_GROUPS * HEADS_PER_GROUP
ATT_WIDTH = N_ATT_HEADS * HEAD_DIM
GROUP_WIDTH = HEADS_PER_GROUP * HEAD_DIM
BAND_BLOCK = 128
NUM_BUCKETS = 32
MAX_DISTANCE = 2048
D_FF = 2816
EPS = 1e-6

F32 = jnp.float32
BF16 = jnp.bfloat16

MXU_DIM = 256
SUBLANES = 8
LANES = 128
VMEM_LIMIT_BYTES = 56 * 1024 * 1024

TM_PROMPT = 256
RG_TILE = MXU_DIM
FFN_CHUNKS = ((0, 1536), (1536, 1280))


def _dot(a, b):
    return jnp.dot(a, b, preferred_element_type=F32)


def _dot_nt(a, b):
    return lax.dot_general(a, b, (((1,), (1,)), ((), ())), preferred_element_type=F32)


def _split_dot(x, m):
    hi = x.astype(BF16)
    lo = (x - hi.astype(F32)).astype(BF16)
    return _dot(hi, m) + _dot(lo, m)


def _sigmoid(x):
    return 0.5 * (jnp.tanh(0.5 * x) + 1.0)


def _gelu_tanh(x):
    c = math.sqrt(2.0 / math.pi)
    return 0.5 * x * (1.0 + jnp.tanh(c * (x + 0.044715 * (x * x * x))))


def _softplus(z):
    return jnp.maximum(z, 0.0) + jnp.log1p(jnp.exp(-jnp.abs(z)))


def _rms_rows(x, g):
    y = x * lax.rsqrt(jnp.mean(x * x, axis=-1, keepdims=True) + EPS)
    return y * g


def _head_rms(t, pn, gain):
    t2 = t * t
    halves = []
    for c in range(GROUP_WIDTH // MXU_DIM):
        halves.append(_split_dot(t2[:, c * MXU_DIM:(c + 1) * MXU_DIM], pn))
    ms = jnp.concatenate(halves, axis=1)
    return (t * lax.rsqrt(ms + EPS)) * gain


def _lru_gates(g, b_a, b_x, sp):
    w = g.shape[1] // 2
    r = _sigmoid(g[:, :w] + b_a)
    ig = _sigmoid(g[:, w:] + b_x)
    log_a = (-LRU_C * r) * sp
    a = jnp.exp(log_a)
    mult = jnp.sqrt(-jnp.tanh(log_a) * (a * a + 1.0))
    return a, mult, ig


def _scan_rows(a, u, h_in):
    rows, c = a.shape
    groups = rows // SUBLANES
    a3 = a.reshape(groups, SUBLANES, c)
    u3 = u.reshape(groups, SUBLANES, c)
    row = lax.broadcasted_iota(jnp.int32, (groups, SUBLANES, c), 1)
    shift = 1
    while shift < SUBLANES:
        ok = row >= shift
        a_sh = jnp.where(ok, pltpu.roll(a3, shift, axis=1), 1.0)
        u_sh = jnp.where(ok, pltpu.roll(u3, shift, axis=1), 0.0)
        u3 = u3 + a3 * u_sh
        a3 = a3 * a_sh
        shift *= 2
    out = []
    h = h_in
    for g in range(groups):
        hg = a3[g] * h + u3[g]
        out.append(hg)
        h = hg[SUBLANES - 1:SUBLANES, :]
    return jnp.concatenate(out, axis=0), h


def _to_slabs(slab_ref, t):
    for c in range(slab_ref.shape[0]):
        slab_ref[c] = t[:, c * LANES:(c + 1) * LANES]


def _from_slabs(slab_ref):
    return jnp.concatenate([slab_ref[c] for c in range(slab_ref.shape[0])], axis=1)


def _gather_residues(slab_ref, dil):
    per = slab_ref.shape[1] // dil
    cols = []
    for c in range(slab_ref.shape[0]):
        cols.append(jnp.concatenate(
            [slab_ref[c, pl.ds(r, per, stride=dil), :] for r in range(dil)], axis=0))
    return jnp.concatenate(cols, axis=1)


def _scatter_residues(slab_ref, t, dil):
    per = slab_ref.shape[1] // dil
    for c in range(slab_ref.shape[0]):
        for r in range(dil):
            slab_ref[c, pl.ds(r, per, stride=dil), :] = t[r * per:(r + 1) * per, c * LANES:(c + 1) * LANES]


def _ffn(x1, nf, wfi_ref, wfo_ref):
    xb = _rms_rows(x1, nf).astype(BF16)
    acc = x1
    for start, width in FFN_CHUNKS:
        gate = _dot(xb, wfi_ref[:, start:start + width])
        up = _dot(xb, wfi_ref[:, D_FF + start:D_FF + start + width])
        hid = (gate * _sigmoid(gate)) * up
        acc = acc + _dot(hid.astype(BF16), wfo_ref[start:start + width, :])
    return acc


def _layer_spec(w, layer):
    tail = (0,) * (w.ndim - 1)
    return pl.BlockSpec((None,) + w.shape[1:], lambda *_: (layer,) + tail, pipeline_mode=pl.Buffered(1))


def _const_spec(shape):
    zeros = (0,) * len(shape)
    return pl.BlockSpec(shape, lambda *_: zeros, pipeline_mode=pl.Buffered(1))


FRONT_WEIGHTS = ("norm_mix", "w_in", "w_merge", "b_merge", "w_conv", "b_conv", "w_rg", "b_rg_a", "b_rg_x",
                 "lam", "q_gain", "k_gain", "w_branch_lru")
BACK_WEIGHTS = ("w_branch_att", "w_o", "norm_ffn", "w_ffn_in", "w_ffn_out")


def _front_kernel(x_ref, nm_ref, win_ref, wmg_ref, bmg_ref, wconv_ref, bconv_ref, wrg_ref,
                  brga_ref, brgx_ref, lam_ref, qg_ref, kg_ref, wbl_ref, pn_ref,
                  q0_ref, q1_ref, q2_ref, k0_ref, k1_ref, k2_ref, v0_ref, v1_ref, v2_ref,
                  ylg_ref, gatt_ref, kvp0_ref, kvp1_ref, kvp2_ref, convp_ref, hp_ref,
                  xn_ref, kv_ref, ext_ref, h_ref, *, tm):
    i = pl.program_id(1)

    @pl.when(i == 0)
    def _():
        ext_ref[0:SUBLANES, :] = jnp.zeros((SUBLANES, D_RNN), F32)
        h_ref[...] = jnp.zeros((SUBLANES, D_RNN), F32)

    xn = _rms_rows(x_ref[...], nm_ref[...])
    _to_slabs(xn_ref, xn)
    xb = xn.astype(BF16)

    lru_x = _dot(xb, win_ref[:, 0:D_RNN])
    ext_ref[SUBLANES:SUBLANES + tm, :] = lru_x
    xc = bconv_ref[...] + wconv_ref[0:1, :] * ext_ref[SUBLANES - 3:SUBLANES - 3 + tm, :]
    xc = xc + wconv_ref[1:2, :] * ext_ref[SUBLANES - 2:SUBLANES - 2 + tm, :]
    xc = xc + wconv_ref[2:3, :] * ext_ref[SUBLANES - 1:SUBLANES - 1 + tm, :]
    xc = xc + wconv_ref[3:4, :] * lru_x
    tail = ext_ref[tm:tm + SUBLANES, :]
    ext_ref[0:SUBLANES, :] = tail
    convp_ref[...] = tail

    lru_g = _dot(xb, win_ref[:, D_RNN:2 * D_RNN])
    xcb = xc.astype(BF16)
    sp = _softplus(-lam_ref[...])
    first_row = (lax.broadcasted_iota(jnp.int32, (tm, RG_TILE), 0) == 0) & (i == 0)
    ys = []
    for t in range(D_RNN // RG_TILE):
        sl = slice(t * RG_TILE, (t + 1) * RG_TILE)
        g = _dot(xcb[:, sl], wrg_ref[t])
        a, mult, ig = _lru_gates(g, brga_ref[:, sl], brgx_ref[:, sl], sp[:, sl])
        mult = jnp.where(first_row, 1.0, mult)
        u = (mult * ig) * xc[:, sl]
        hs, h_last = _scan_rows(a, u, h_ref[0:1, sl])
        h_ref[:, sl] = jnp.broadcast_to(h_last, (SUBLANES, RG_TILE))
        ys.append((hs * _gelu_tanh(lru_g[:, sl])).astype(BF16))
    hp_ref[...] = h_ref[...]
    y_lru = _dot(jnp.concatenate(ys, axis=1), wbl_ref[...])
    g_lru = _sigmoid(_dot(xb, wmg_ref[:, 0:D_MODEL]) + bmg_ref[:, 0:D_MODEL])
    ylg_ref[...] = g_lru * y_lru
    gatt_ref[...] = _sigmoid(_dot(xb, wmg_ref[:, D_MODEL:2 * D_MODEL]) + bmg_ref[:, D_MODEL:2 * D_MODEL])

    pn = pn_ref[...]
    q_refs = (q0_ref, q1_ref, q2_ref)
    k_refs = (k0_ref, k1_ref, k2_ref)
    v_refs = (v0_ref, v1_ref, v2_ref)
    kvp_refs = (kvp0_ref, kvp1_ref, kvp2_ref)
    for g, (window, dil) in enumerate(DILATED_GROUPS):
        per = tm // dil
        xg = xb if dil == 1 else _gather_residues(xn_ref, dil).astype(BF16)
        c = 2 * D_RNN + g * GROUP_WIDTH
        q = _dot(xg, win_ref[:, c:c + GROUP_WIDTH])
        k = _dot(xg, win_ref[:, c + ATT_WIDTH:c + ATT_WIDTH + GROUP_WIDTH])
        v = _dot(xg, win_ref[:, c + 2 * ATT_WIDTH:c + 2 * ATT_WIDTH + GROUP_WIDTH])
        qn = (_head_rms(q, pn, qg_ref[...]) * (HEAD_DIM ** -0.5)).astype(BF16)
        kn = _head_rms(k, pn, kg_ref[...])
        knb = kn.astype(BF16)
        vb = v.astype(BF16)
        for r in range(dil):
            rows = slice(r * per, (r + 1) * per)
            q_refs[g][r] = qn[rows]
            k_refs[g][r] = knb[rows]
            v_refs[g][r] = vb[rows]
        keep = min(window, tm)
        if dil == 1:
            kvp_refs[g][:, 0:GROUP_WIDTH] = kn[tm - keep:, :]
            kvp_refs[g][:, GROUP_WIDTH:2 * GROUP_WIDTH] = v[tm - keep:, :]
        else:
            _scatter_residues(kv_ref, jnp.concatenate([kn, v], axis=1), dil)
            kvp_refs[g][...] = _from_slabs(kv_ref)


def _front_call(x, lw, layer, tm):
    B, S, _ = x.shape
    nt = S // tm
    row_spec = lambda width: pl.BlockSpec((None, tm, width), lambda b, i: (b, i, 0))
    weights = [lw[n] for n in FRONT_WEIGHTS]
    in_specs = ([row_spec(D_MODEL)] + [_layer_spec(w, layer) for w in weights]
                + [_const_spec(lw["pnorm"].shape)])

    out_shape, out_specs = [], []
    for _ in range(3):
        for _, dil in DILATED_GROUPS:
            out_shape.append(jax.ShapeDtypeStruct((B, dil, S // dil, GROUP_WIDTH), BF16))
            out_specs.append(pl.BlockSpec((None, dil, tm // dil, GROUP_WIDTH), lambda b, i: (b, 0, i, 0)))
    out_shape.append(jax.ShapeDtypeStruct((B, S, D_MODEL), F32))
    out_specs.append(row_spec(D_MODEL))
    out_shape.append(jax.ShapeDtypeStruct((B, S, D_MODEL), F32))
    out_specs.append(row_spec(D_MODEL))
    for window, _ in DILATED_GROUPS:
        keep = min(window, tm)
        first = nt - window // keep
        out_shape.append(jax.ShapeDtypeStruct((B, window, 2 * GROUP_WIDTH), F32))
        out_specs.append(pl.BlockSpec(
            (None, keep, 2 * GROUP_WIDTH),
            functools.partial(lambda b, i, first: (b, jnp.maximum(i - first, 0), 0), first=first)))
    for _ in range(2):
        out_shape.append(jax.ShapeDtypeStruct((B, SUBLANES, D_RNN), F32))
        out_specs.append(pl.BlockSpec((None, SUBLANES, D_RNN), lambda b, i: (b, 0, 0)))

    return pl.pallas_call(
        functools.partial(_front_kernel, tm=tm),
        grid=(B, nt),
        in_specs=in_specs,
        out_specs=out_specs,
        out_shape=out_shape,
        scratch_shapes=[pltpu.VMEM((D_MODEL // LANES, tm, LANES), F32),
                        pltpu.VMEM((2 * GROUP_WIDTH // LANES, tm, LANES), F32),
                        pltpu.VMEM((tm + SUBLANES, D_RNN), F32), pltpu.VMEM((SUBLANES, D_RNN), F32)],
        compiler_params=pltpu.CompilerParams(
            dimension_semantics=("arbitrary", "arbitrary"), vmem_limit_bytes=VMEM_LIMIT_BYTES),
        name="prompt_front",
    )(x, *weights, lw["pnorm"])


def _attn_kernel(tab_ref, bkt_ref, q_ref, kp_ref, kc_ref, vp_ref, vc_ref, o_ref, lse_ref, bias_ref):
    b, r, j = pl.program_id(0), pl.program_id(1), pl.program_id(2)
    BB = BAND_BLOCK

    @pl.when((b == 0) & (r == 0) & (j == 0))
    def _():
        bkt = bkt_ref[...]
        for h in range(HEADS_PER_GROUP):
            acc = jnp.full(bkt.shape, -jnp.inf, F32)
            for n in range(NUM_BUCKETS):
                acc = jnp.where(bkt == n, tab_ref[n, h], acc)
            bias_ref[h] = acc

    pen = jnp.where(j == 0, -jnp.inf, 0.0).astype(F32)
    low_half = lax.broadcasted_iota(jnp.int32, (BB, 2 * HEAD_DIM), 1) < HEAD_DIM
    for hp in range(HEADS_PER_GROUP // 2):
        sl = slice(hp * 2 * HEAD_DIM, (hp + 1) * 2 * HEAD_DIM)
        qp = q_ref[:, sl]
        kp, kc, vp, vc = kp_ref[:, sl], kc_ref[:, sl], vp_ref[:, sl], vc_ref[:, sl]
        o_pair = None
        lse_pair = None
        for e in range(2):
            h = 2 * hp + e
            mine = low_half if e == 0 else jnp.logical_not(low_half)
            qm = jnp.where(mine, qp, jnp.zeros_like(qp))
            s_p = _dot_nt(qm, kp) + bias_ref[h, :, 0:BB] + pen
            s_c = _dot_nt(qm, kc) + bias_ref[h, :, BB:2 * BB]
            m = jnp.maximum(jnp.max(s_p, axis=-1, keepdims=True), jnp.max(s_c, axis=-1, keepdims=True))
            p_p = jnp.exp(s_p - m)
            p_c = jnp.exp(s_c - m)
            l = jnp.sum(p_p, axis=-1, keepdims=True) + jnp.sum(p_c, axis=-1, keepdims=True)
            o = (_dot(p_p.astype(BF16), vp) + _dot(p_c.astype(BF16), vc)) / l
            lse = jnp.broadcast_to(m + jnp.log(l), o.shape)
            o_pair = o if e == 0 else jnp.where(low_half, o_pair, o)
            lse_pair = lse if e == 0 else jnp.where(low_half, lse_pair, lse)
        o_ref[:, sl] = o_pair
        lse_ref[:, sl] = lse_pair


def _t5_bucket(dist):
    max_exact = NUM_BUCKETS // 2
    d_f = jnp.maximum(dist, 1).astype(F32)
    large = max_exact + (jnp.log(d_f / max_exact) / math.log(MAX_DISTANCE / max_exact)
                         * (NUM_BUCKETS - max_exact)).astype(jnp.int32)
    large = jnp.minimum(large, NUM_BUCKETS - 1)
    return jnp.where(dist < max_exact, dist, large)


def _band_buckets(dil):
    BB = BAND_BLOCK
    qi = jnp.arange(BB)[:, None]
    ki = jnp.arange(2 * BB)[None, :]
    sub = qi + BB - ki
    in_band = (sub >= 0) & (sub <= BB)
    return jnp.where(in_band, _t5_bucket(jnp.clip(sub, 0, BB) * dil), -1).astype(jnp.int32)


def _attn_call(q, k, v, tab, dil):
    B, _, n, C = q.shape
    BB = BAND_BLOCK
    cur = pl.BlockSpec((None, None, BB, C), lambda b, r, j: (b, r, j, 0))
    prev = pl.BlockSpec((None, None, BB, C), lambda b, r, j: (b, r, jnp.maximum(j - 1, 0), 0))
    return pl.pallas_call(
        _attn_kernel,
        grid=(B, dil, n // BB),
        in_specs=[pl.BlockSpec(memory_space=pltpu.SMEM),
                  pl.BlockSpec((BB, 2 * BB), lambda b, r, j: (0, 0)),
                  cur, prev, cur, prev, cur],
        out_specs=[cur, cur],
        out_shape=[jax.ShapeDtypeStruct(q.shape, F32)] * 2,
        scratch_shapes=[pltpu.VMEM((HEADS_PER_GROUP, BB, 2 * BB), F32)],
        compiler_params=pltpu.CompilerParams(
            dimension_semantics=("arbitrary", "arbitrary", "arbitrary")),
        name="prompt_attn_d%d" % dil,
    )(tab, _band_buckets(dil), q, k, k, v, v)


def _back_kernel(x_ref, ylg_ref, gatt_ref, o0_ref, o1_ref, o2_ref, l0_ref, l1_ref, l2_ref,
                 wba_ref, wo_ref, nf_ref, wfi_ref, wfo_ref, y_ref,
                 so1_ref, so2_ref, sl1_ref, sl2_ref, *, tm):
    for (_, dil), src, dst in ((DILATED_GROUPS[1], o1_ref, so1_ref), (DILATED_GROUPS[2], o2_ref, so2_ref),
                               (DILATED_GROUPS[1], l1_ref, sl1_ref), (DILATED_GROUPS[2], l2_ref, sl2_ref)):
        _scatter_residues(dst, src[...].reshape(tm, GROUP_WIDTH), dil)
    os_ = (o0_ref[0], _from_slabs(so1_ref), _from_slabs(so2_ref))
    lses = (l0_ref[0], _from_slabs(sl1_ref), _from_slabs(sl2_ref))

    top = jnp.maximum(jnp.maximum(lses[0], lses[1]), lses[2])
    num = 0.0
    den = 0.0
    for o, lse in zip(os_, lses):
        z = jnp.exp(lse - top)
        num = num + z * o
        den = den + z
    o = num / den

    y_att = _dot(o.astype(BF16), wba_ref[...])
    mix = ylg_ref[...] + gatt_ref[...] * y_att
    x1 = x_ref[...] + _dot(mix.astype(BF16), wo_ref[...])
    y_ref[...] = _ffn(x1, nf_ref[...], wfi_ref, wfo_ref)


def _back_call(x, ylg, gatt, os_, lses, lw, layer, tm):
    B, S, _ = x.shape
    row_spec = lambda width: pl.BlockSpec((None, tm, width), lambda b, i: (b, i, 0))
    res_specs = [pl.BlockSpec((None, dil, tm // dil, GROUP_WIDTH), lambda b, i: (b, 0, i, 0))
                 for _, dil in DILATED_GROUPS]
    weights = [lw[n] for n in BACK_WEIGHTS]
    in_specs = [row_spec(D_MODEL)] * 3 + res_specs * 2 + [_layer_spec(w, layer) for w in weights]
    return pl.pallas_call(
        functools.partial(_back_kernel, tm=tm),
        grid=(B, S // tm),
        in_specs=in_specs,
        out_specs=row_spec(D_MODEL),
        out_shape=jax.ShapeDtypeStruct((B, S, D_MODEL), F32),
        scratch_shapes=[pltpu.VMEM((GROUP_WIDTH // LANES, tm, LANES), F32)] * 4,
        compiler_params=pltpu.CompilerParams(
            dimension_semantics=("arbitrary", "arbitrary"), vmem_limit_bytes=VMEM_LIMIT_BYTES),
        name="prompt_back",
    )(x, ylg, gatt, *os_, *lses, *weights)


def _dec_front_kernel(x_ref, c0_ref, c1_ref, c2_ref, h0_ref, nm_ref, win_ref, wmg_ref, bmg_ref,
                      wconv_ref, bconv_ref, wrg_ref, brga_ref, brgx_ref, lam_ref, qg_ref, kg_ref,
                      wbl_ref, pn_ref,
                      q_ref, k_ref, v_ref, ylg_ref, gatt_ref, lrux_ref, hs_ref):
    xb = _rms_rows(x_ref[...], nm_ref[...]).astype(BF16)
    lru_x = _dot(xb, win_ref[:, 0:D_RNN])
    lrux_ref[...] = lru_x
    xc = bconv_ref[...] + wconv_ref[0:1, :] * c0_ref[...]
    xc = xc + wconv_ref[1:2, :] * c1_ref[...]
    xc = xc + wconv_ref[2:3, :] * c2_ref[...]
    xc = xc + wconv_ref[3:4, :] * lru_x

    lru_g = _dot(xb, win_ref[:, D_RNN:2 * D_RNN])
    xcb = xc.astype(BF16)
    sp = _softplus(-lam_ref[...])
    ys = []
    for t in range(D_RNN // RG_TILE):
        sl = slice(t * RG_TILE, (t + 1) * RG_TILE)
        g = _dot(xcb[:, sl], wrg_ref[t])
        a, mult, ig = _lru_gates(g, brga_ref[:, sl], brgx_ref[:, sl], sp[:, sl])
        h = a * h0_ref[:, sl] + (mult * ig) * xc[:, sl]
        hs_ref[:, sl] = h
        ys.append((h * _gelu_tanh(lru_g[:, sl])).astype(BF16))
    y_lru = _dot(jnp.concatenate(ys, axis=1), wbl_ref[...])
    g_lru = _sigmoid(_dot(xb, wmg_ref[:, 0:D_MODEL]) + bmg_ref[:, 0:D_MODEL])
    ylg_ref[...] = g_lru * y_lru
    gatt_ref[...] = _sigmoid(_dot(xb, wmg_ref[:, D_MODEL:2 * D_MODEL]) + bmg_ref[:, D_MODEL:2 * D_MODEL])

    pn = pn_ref[...]
    for g in range(N_GROUPS):
        c = 2 * D_RNN + g * GROUP_WIDTH
        sl = slice(g * GROUP_WIDTH, (g + 1) * GROUP_WIDTH)
        q = _dot(xb, win_ref[:, c:c + GROUP_WIDTH])
        k = _dot(xb, win_ref[:, c + ATT_WIDTH:c + ATT_WIDTH + GROUP_WIDTH])
        v = _dot(xb, win_ref[:, c + 2 * ATT_WIDTH:c + 2 * ATT_WIDTH + GROUP_WIDTH])
        q_ref[:, sl] = _head_rms(q, pn, qg_ref[...]) * (HEAD_DIM ** -0.5)
        k_ref[:, sl] = _head_rms(k, pn, kg_ref[...])
        v_ref[:, sl] = v


def _dec_front_call(x, conv_rows, h0, lw, layer):
    nb = x.shape[0]
    weights = [lw[n] for n in FRONT_WEIGHTS]
    acts = (x,) + tuple(conv_rows) + (h0,)
    widths = (ATT_WIDTH, ATT_WIDTH, ATT_WIDTH, D_MODEL, D_MODEL, D_RNN, D_RNN)
    return pl.pallas_call(
        _dec_front_kernel,
        grid=(1,),
        in_specs=([_const_spec(a.shape) for a in acts] + [_layer_spec(w, layer) for w in weights]
                  + [_const_spec(lw["pnorm"].shape)]),
        out_specs=[pl.BlockSpec((nb, w), lambda i: (0, 0)) for w in widths],
        out_shape=[jax.ShapeDtypeStruct((nb, w), F32) for w in widths],
        compiler_params=pltpu.CompilerParams(
            dimension_semantics=("arbitrary",), vmem_limit_bytes=VMEM_LIMIT_BYTES),
        name="decode_front",
    )(*acts, *weights, lw["pnorm"])


def _dec_attn_kernel(qt_ref, kt_ref, vt_ref, c0_ref, c1_ref, c2_ref, tabt_ref, bkt0_ref, bkt1_ref, bkt2_ref,
                     ot_ref, bias0_ref, bias1_ref, bias2_ref):
    b = pl.program_id(0)
    c_refs = (c0_ref, c1_ref, c2_ref)
    bias_refs = (bias0_ref, bias1_ref, bias2_ref)

    @pl.when(b == 0)
    def _():
        ot_ref[...] = jnp.zeros(ot_ref.shape, F32)
        for g, (bkt_ref, bias_ref) in enumerate(zip((bkt0_ref, bkt1_ref, bkt2_ref), bias_refs)):
            bkt = bkt_ref[...]
            acc = jnp.full(bkt.shape, -jnp.inf, F32)
            for n in range(NUM_BUCKETS):
                acc = jnp.where(bkt == n, tabt_ref[g, :, n:n + 1], acc)
            bias_ref[...] = acc

    mine = lax.broadcasted_iota(jnp.int32, qt_ref.shape, 1) == b
    column = lambda ref: jnp.sum(jnp.where(mine, ref[...], 0.0), axis=1, keepdims=True)
    qcol, kcol, vcol = column(qt_ref), column(kt_ref), column(vt_ref)
    mine_o = lax.broadcasted_iota(jnp.int32, (HEAD_DIM, ot_ref.shape[1]), 1) == b

    for h in range(HEADS_PER_GROUP):
        parts = []
        for g in range(N_GROUPS):
            rows = slice(g * GROUP_WIDTH + h * HEAD_DIM, g * GROUP_WIDTH + (h + 1) * HEAD_DIM)
            qh = qcol[rows]
            s = jnp.sum(c_refs[g][0, h] * qh, axis=0, keepdims=True) + bias_refs[g][h:h + 1, :]
            s0 = jnp.sum(kcol[rows] * qh, axis=0, keepdims=True) + tabt_ref[g, h:h + 1, 0:1]
            m = jnp.maximum(jnp.max(s, axis=1, keepdims=True), s0)
            p = jnp.exp(s - m)
            p0 = jnp.exp(s0 - m)
            l = jnp.sum(p, axis=1, keepdims=True) + p0
            acc = jnp.sum(c_refs[g][1, h] * p, axis=1, keepdims=True) + p0 * vcol[rows]
            parts.append((acc, m, l))
        m_max = jnp.maximum(jnp.maximum(parts[0][1], parts[1][1]), parts[2][1])
        num = 0.0
        den = 0.0
        for acc, m, l in parts:
            w = jnp.exp(m - m_max)
            num = num + w * acc
            den = den + l * w
        hs = slice(h * HEAD_DIM, (h + 1) * HEAD_DIM)
        ot_ref[hs, :] = jnp.where(mine_o, num / den, ot_ref[hs, :])


def _dec_attn_call(qt, kt, vt, caches_t, layer, tabt):
    nb = qt.shape[1]
    bkts = []
    for window, dil in DILATED_GROUPS:
        w = jnp.arange(window)
        bkt = jnp.where(w % dil == 0, _t5_bucket(window - w), -1).astype(jnp.int32)
        bkts.append(jnp.broadcast_to(bkt[None, :], (HEADS_PER_GROUP, window)))
    cache_spec = lambda c: pl.BlockSpec((None, None) + c.shape[2:], lambda i: (layer, i, 0, 0, 0, 0))
    return pl.pallas_call(
        _dec_attn_kernel,
        grid=(nb,),
        in_specs=[_const_spec(qt.shape)] * 3 + [cache_spec(c) for c in caches_t]
        + [_const_spec(tabt.shape)] + [_const_spec(b.shape) for b in bkts],
        out_specs=pl.BlockSpec((GROUP_WIDTH, nb), lambda i: (0, 0)),
        out_shape=jax.ShapeDtypeStruct((GROUP_WIDTH, nb), F32),
        scratch_shapes=[pltpu.VMEM((HEADS_PER_GROUP, window), F32) for window, _ in DILATED_GROUPS],
        compiler_params=pltpu.CompilerParams(
            dimension_semantics=("arbitrary",), vmem_limit_bytes=VMEM_LIMIT_BYTES),
        name="decode_attn",
    )(qt, kt, vt, *caches_t, tabt, *bkts)


def _dec_back_kernel(x_ref, ylg_ref, gatt_ref, o_ref, wba_ref, wo_ref, nf_ref, wfi_ref, wfo_ref, y_ref):
    y_att = _dot(o_ref[...].astype(BF16), wba_ref[...])
    mix = ylg_ref[...] + gatt_ref[...] * y_att
    x1 = x_ref[...] + _dot(mix.astype(BF16), wo_ref[...])
    y_ref[...] = _ffn(x1, nf_ref[...], wfi_ref, wfo_ref)


def _dec_back_call(x, ylg, gatt, o, lw, layer):
    weights = [lw[n] for n in BACK_WEIGHTS]
    acts = (x, ylg, gatt, o)
    return pl.pallas_call(
        _dec_back_kernel,
        grid=(1,),
        in_specs=[_const_spec(a.shape) for a in acts] + [_layer_spec(w, layer) for w in weights],
        out_specs=pl.BlockSpec(x.shape, lambda i: (0, 0)),
        out_shape=jax.ShapeDtypeStruct(x.shape, F32),
        compiler_params=pltpu.CompilerParams(
            dimension_semantics=("arbitrary",), vmem_limit_bytes=VMEM_LIMIT_BYTES),
        name="decode_back",
    )(*acts, *weights)


def _prepare_weights(norm_mix, w_in, w_conv, b_conv, w_rg_a, b_rg_a, w_rg_x, b_rg_x, lru_lambda,
                     q_gain, k_gain, w_merge, b_merge, w_branch_lru, w_branch_att, w_o, norm_ffn,
                     w_ffn_in, w_ffn_out):
    depth = norm_mix.shape[0]
    row = lambda t: t.reshape(depth, 1, -1).astype(F32)
    per_tile = RG_TILE // LRU_BLOCK
    tiles = D_RNN // RG_TILE

    def block_diag(w):
        w = w.reshape(depth, tiles, per_tile, LRU_BLOCK, LRU_BLOCK)
        eye = jnp.eye(per_tile, dtype=w.dtype)
        return jnp.einsum("ltnij,nm->ltnimj", w, eye).reshape(depth, tiles, RG_TILE, RG_TILE)

    head = jnp.arange(MXU_DIM) // HEAD_DIM
    tile_gain = lambda t: jnp.tile(t.reshape(depth, 1, HEAD_DIM), (1, 1, HEADS_PER_GROUP)).astype(F32)
    return dict(
        norm_mix=row(norm_mix), w_in=w_in.astype(BF16), w_merge=w_merge.astype(BF16),
        b_merge=row(b_merge), w_conv=w_conv.astype(F32), b_conv=row(b_conv),
        w_rg=jnp.concatenate([block_diag(w_rg_a), block_diag(w_rg_x)], axis=3).astype(BF16),
        b_rg_a=row(b_rg_a), b_rg_x=row(b_rg_x), lam=row(lru_lambda),
        q_gain=tile_gain(q_gain), k_gain=tile_gain(k_gain),
        w_branch_lru=w_branch_lru.astype(BF16),
        pnorm=((head[:, None] == head[None, :]).astype(F32) / HEAD_DIM).astype(BF16),
        w_branch_att=w_branch_att.astype(BF16), w_o=w_o.astype(BF16), norm_ffn=row(norm_ffn),
        w_ffn_in=w_ffn_in.astype(BF16), w_ffn_out=w_ffn_out.astype(BF16))


def kernel(x_prompt, x_sample, cache_kv_g0, cache_kv_g1, cache_kv_g2, state_conv, state_h, rel_bias,
           norm_mix, w_in, w_conv, b_conv, w_rg_a, b_rg_a, w_rg_x, b_rg_x, lru_lambda, q_gain, k_gain,
           w_merge, b_merge, w_branch_lru, w_branch_att, w_o, norm_ffn, w_ffn_in, w_ffn_out):
    B, S, _ = x_prompt.shape
    nb = x_sample.shape[0]
    depth = norm_mix.shape[0]
    lw = _prepare_weights(norm_mix, w_in, w_conv, b_conv, w_rg_a, b_rg_a, w_rg_x, b_rg_x, lru_lambda,
                          q_gain, k_gain, w_merge, b_merge, w_branch_lru, w_branch_att, w_o, norm_ffn,
                          w_ffn_in, w_ffn_out)
    rel_bias = rel_bias.astype(F32)
    tabt = rel_bias.reshape(NUM_BUCKETS, N_GROUPS, HEADS_PER_GROUP).transpose(1, 2, 0)
    caches_t = [jnp.transpose(c.astype(F32), (0, 1, 3, 4, 5, 2))
                for c in (cache_kv_g0, cache_kv_g1, cache_kv_g2)]

    yp = x_prompt
    ys = x_sample.reshape(nb, D_MODEL)
    kvp = [[] for _ in range(N_GROUPS)]
    kvs = [[] for _ in range(N_GROUPS)]
    conv_p, h_p, conv_s, h_s = [], [], [], []
    for l in range(depth):
        (q0, q1, q2, k0, k1, k2, v0, v1, v2, ylg, gatt, kv0, kv1, kv2, ctail, hfin) = _front_call(
            yp, lw, l, TM_PROMPT)
        os_, lses = [], []
        for g, (qg, kg, vg) in enumerate(((q0, k0, v0), (q1, k1, v1), (q2, k2, v2))):
            tab = rel_bias[:, g * HEADS_PER_GROUP:(g + 1) * HEADS_PER_GROUP]
            o, lse = _attn_call(qg, kg, vg, tab, DILATED_GROUPS[g][1])
            os_.append(o)
            lses.append(lse)
        yp = _back_call(yp, ylg, gatt, os_, lses, lw, l, TM_PROMPT)
        for g, kv in enumerate((kv0, kv1, kv2)):
            kvp[g].append(kv.reshape(B, kv.shape[1], 2, HEADS_PER_GROUP, HEAD_DIM))
        conv_p.append(ctail[:, SUBLANES - (CONV_WIDTH - 1):, :])
        h_p.append(hfin[:, 0, :])

        sc = state_conv[l].astype(F32)
        conv_rows = [sc[:, r, :] for r in range(CONV_WIDTH - 1)]
        qs, ks, vs, ylg_s, gatt_s, lrux_s, hs_s = _dec_front_call(ys, conv_rows, state_h[l].astype(F32), lw, l)
        o_t = _dec_attn_call(qs.T, ks.T, vs.T, caches_t, l, tabt)
        ys = _dec_back_call(ys, ylg_s, gatt_s, o_t.T, lw, l)
        for g in range(N_GROUPS):
            sl = slice(g * GROUP_WIDTH, (g + 1) * GROUP_WIDTH)
            kvs[g].append(jnp.stack([ks[:, sl], vs[:, sl]], axis=1).reshape(
                nb, 1, 2, HEADS_PER_GROUP, HEAD_DIM))
        conv_s.append(jnp.stack(conv_rows[1:] + [lrux_s], axis=1))
        h_s.append(hs_s)

    stack = jnp.stack
    return (yp, ys.reshape(nb, 1, D_MODEL),
            stack(kvp[0]), stack(kvp[1]), stack(kvp[2]), stack(conv_p), stack(h_p),
            stack(kvs[0]), stack(kvs[1]), stack(kvs[2]), stack(conv_s), stack(h_s))
```

```python
import functools
import math

import jax
import jax.numpy as jnp
from jax import lax
from jax.experimental import pallas as pl
from jax.experimental.pallas import tpu as pltpu

D_MODEL = 1024
D_RNN = D_MODEL
N_LRU_BLOCKS = 16
LRU_BLOCK = D_RNN // N_LRU_BLOCKS
CONV_WIDTH = 4
LRU_C = 8.0
HEAD_DIM = 64
HEADS_PER_GROUP = 8
DILATED_GROUPS = ((128, 1), (512, 4), (2048, 16))
N_GROUPS = len(DILATED_GROUPS)
N_ATT_HEADS = N_GROUPS * HEADS_PER_GROUP
ATT_WIDTH = N_ATT_HEADS * HEAD_DIM
GROUP_WIDTH = HEADS_PER_GROUP * HEAD_DIM
BAND_BLOCK = 128
NUM_BUCKETS = 32
MAX_DISTANCE = 2048
D_FF = 2816
EPS = 1e-6

F32 = jnp.float32
BF16 = jnp.bfloat16

MXU_DIM = 256
SUBLANES = 8
LANES = 128
VMEM_LIMIT_BYTES = 56 * 1024 * 1024

TM_PROMPT = 256
ATTN_QBLOCKS = 4
RG_TILE = MXU_DIM
FFN_CHUNKS = ((0, 1536), (1536, 1280))


def _dot(a, b):
    return jnp.dot(a, b, preferred_element_type=F32)


def _dot_nt(a, b):
    return lax.dot_general(a, b, (((1,), (1,)), ((), ())), preferred_element_type=F32)


def _split_dot(x, m):
    hi = x.astype(BF16)
    lo = (x - hi.astype(F32)).astype(BF16)
    return _dot(hi, m) + _dot(lo, m)


def _sigmoid(x):
    return 0.5 * (jnp.tanh(0.5 * x) + 1.0)


def _gelu_tanh(x):
    c = math.sqrt(2.0 / math.pi)
    return 0.5 * x * (1.0 + jnp.tanh(c * (x + 0.044715 * (x * x * x))))


def _softplus(z):
    return jnp.maximum(z, 0.0) + jnp.log1p(jnp.exp(-jnp.abs(z)))


def _rms_rows(x, g):
    y = x * lax.rsqrt(jnp.mean(x * x, axis=-1, keepdims=True) + EPS)
    return y * g


def _head_rms(t, pn, gain):
    t2 = t * t
    halves = []
    for c in range(GROUP_WIDTH // MXU_DIM):
        halves.append(_split_dot(t2[:, c * MXU_DIM:(c + 1) * MXU_DIM], pn))
    ms = jnp.concatenate(halves, axis=1)
    return (t * lax.rsqrt(ms + EPS)) * gain


def _lru_gates(g, b_a, b_x, sp):
    w = g.shape[1] // 2
    r = _sigmoid(g[:, :w] + b_a)
    ig = _sigmoid(g[:, w:] + b_x)
    log_a = (-LRU_C * r) * sp
    a = jnp.exp(log_a)
    mult = jnp.sqrt(-jnp.tanh(log_a) * (a * a + 1.0))
    return a, mult, ig


def _scan_rows(a, u, h_in):
    rows, c = a.shape
    groups = rows // SUBLANES
    a3 = a.reshape(groups, SUBLANES, c)
    u3 = u.reshape(groups, SUBLANES, c)
    row = lax.broadcasted_iota(jnp.int32, (groups, SUBLANES, c), 1)
    shift = 1
    while shift < SUBLANES:
        ok = row >= shift
        a_sh = jnp.where(ok, pltpu.roll(a3, shift, axis=1), 1.0)
        u_sh = jnp.where(ok, pltpu.roll(u3, shift, axis=1), 0.0)
        u3 = u3 + a3 * u_sh
        a3 = a3 * a_sh
        shift *= 2
    out = []
    h = h_in
    for g in range(groups):
        hg = a3[g] * h + u3[g]
        out.append(hg)
        h = hg[SUBLANES - 1:SUBLANES, :]
    return jnp.concatenate(out, axis=0), h


def _to_slabs(slab_ref, t):
    for c in range(slab_ref.shape[0]):
        slab_ref[c] = t[:, c * LANES:(c + 1) * LANES]


def _from_slabs(slab_ref):
    return jnp.concatenate([slab_ref[c] for c in range(slab_ref.shape[0])], axis=1)


def _gather_residues(slab_ref, dil):
    per = slab_ref.shape[1] // dil
    cols = []
    for c in range(slab_ref.shape[0]):
        cols.append(jnp.concatenate(
            [slab_ref[c, pl.ds(r, per, stride=dil), :] for r in range(dil)], axis=0))
    return jnp.concatenate(cols, axis=1)


def _scatter_residues(slab_ref, t, dil):
    per = slab_ref.shape[1] // dil
    for c in range(slab_ref.shape[0]):
        for r in range(dil):
            slab_ref[c, pl.ds(r, per, stride=dil), :] = t[r * per:(r + 1) * per, c * LANES:(c + 1) * LANES]


def _ffn(x1, nf, wfi_ref, wfo_ref):
    xb = _rms_rows(x1, nf).astype(BF16)
    acc = x1
    for start, width in FFN_CHUNKS:
        gate = _dot(xb, wfi_ref[:, start:start + width])
        up = _dot(xb, wfi_ref[:, D_FF + start:D_FF + start + width])
        hid = (gate * _sigmoid(gate)) * up
        acc = acc + _dot(hid.astype(BF16), wfo_ref[start:start + width, :])
    return acc


def _layer_spec(w, layer):
    tail = (0,) * (w.ndim - 1)
    return pl.BlockSpec((None,) + w.shape[1:], lambda *_: (layer,) + tail, pipeline_mode=pl.Buffered(1))


def _const_spec(shape):
    zeros = (0,) * len(shape)
    return pl.BlockSpec(shape, lambda *_: zeros, pipeline_mode=pl.Buffered(1))


FRONT_WEIGHTS = ("norm_mix", "w_in", "w_merge", "b_merge", "w_conv", "b_conv", "w_rg", "b_rg_a", "b_rg_x",
                 "lam", "q_gain", "k_gain", "w_branch_lru")
BACK_WEIGHTS = ("w_branch_att", "w_o", "norm_ffn", "w_ffn_in", "w_ffn_out")


def _front_kernel(x_ref, nm_ref, win_ref, wmg_ref, bmg_ref, wconv_ref, bconv_ref, wrg_ref,
                  brga_ref, brgx_ref, lam_ref, qg_ref, kg_ref, wbl_ref, pn_ref,
                  q0_ref, q1_ref, q2_ref, k0_ref, k1_ref, k2_ref, v0_ref, v1_ref, v2_ref,
                  ylg_ref, gatt_ref, kvp0_ref, kvp1_ref, kvp2_ref, convp_ref, hp_ref,
                  xn_ref, kv_ref, ext_ref, h_ref, *, tm):
    i = pl.program_id(1)

    @pl.when(i == 0)
    def _():
        ext_ref[0:SUBLANES, :] = jnp.zeros((SUBLANES, D_RNN), F32)
        h_ref[...] = jnp.zeros((SUBLANES, D_RNN), F32)

    xn = _rms_rows(x_ref[...], nm_ref[...])
    _to_slabs(xn_ref, xn)
    xb = xn.astype(BF16)

    lru_x = _dot(xb, win_ref[:, 0:D_RNN])
    ext_ref[SUBLANES:SUBLANES + tm, :] = lru_x
    xc = bconv_ref[...] + wconv_ref[0:1, :] * ext_ref[SUBLANES - 3:SUBLANES - 3 + tm, :]
    xc = xc + wconv_ref[1:2, :] * ext_ref[SUBLANES - 2:SUBLANES - 2 + tm, :]
    xc = xc + wconv_ref[2:3, :] * ext_ref[SUBLANES - 1:SUBLANES - 1 + tm, :]
    xc = xc + wconv_ref[3:4, :] * lru_x
    tail = ext_ref[tm:tm + SUBLANES, :]
    ext_ref[0:SUBLANES, :] = tail
    convp_ref[...] = tail

    lru_g = _dot(xb, win_ref[:, D_RNN:2 * D_RNN])
    xcb = xc.astype(BF16)
    sp = _softplus(-lam_ref[...])
    first_row = (lax.broadcasted_iota(jnp.int32, (tm, RG_TILE), 0) == 0) & (i == 0)
    ys = []
    for t in range(D_RNN // RG_TILE):
        sl = slice(t * RG_TILE, (t + 1) * RG_TILE)
        g = _dot(xcb[:, sl], wrg_ref[t])
        a, mult, ig = _lru_gates(g, brga_ref[:, sl], brgx_ref[:, sl], sp[:, sl])
        mult = jnp.where(first_row, 1.0, mult)
        u = (mult * ig) * xc[:, sl]
        hs, h_last = _scan_rows(a, u, h_ref[0:1, sl])
        h_ref[:, sl] = jnp.broadcast_to(h_last, (SUBLANES, RG_TILE))
        ys.append((hs * _gelu_tanh(lru_g[:, sl])).astype(BF16))
    hp_ref[...] = h_ref[...]
    y_lru = _dot(jnp.concatenate(ys, axis=1), wbl_ref[...])
    g_lru = _sigmoid(_dot(xb, wmg_ref[:, 0:D_MODEL]) + bmg_ref[:, 0:D_MODEL])
    ylg_ref[...] = g_lru * y_lru
    gatt_ref[...] = _sigmoid(_dot(xb, wmg_ref[:, D_MODEL:2 * D_MODEL]) + bmg_ref[:, D_MODEL:2 * D_MODEL])

    pn = pn_ref[...]
    q_refs = (q0_ref, q1_ref, q2_ref)
    k_refs = (k0_ref, k1_ref, k2_ref)
    v_refs = (v0_ref, v1_ref, v2_ref)
    kvp_refs = (kvp0_ref, kvp1_ref, kvp2_ref)
    for g, (window, dil) in enumerate(DILATED_GROUPS):
        per = tm // dil
        xg = xb if dil == 1 else _gather_residues(xn_ref, dil).astype(BF16)
        c = 2 * D_RNN + g * GROUP_WIDTH
        q = _dot(xg, win_ref[:, c:c + GROUP_WIDTH])
        k = _dot(xg, win_ref[:, c + ATT_WIDTH:c + ATT_WIDTH + GROUP_WIDTH])
        v = _dot(xg, win_ref[:, c + 2 * ATT_WIDTH:c + 2 * ATT_WIDTH + GROUP_WIDTH])
        qn = (_head_rms(q, pn, qg_ref[...]) * (HEAD_DIM ** -0.5)).astype(BF16)
        kn = _head_rms(k, pn, kg_ref[...])
        knb = kn.astype(BF16)
        vb = v.astype(BF16)
        for r in range(dil):
            rows = slice(r * per, (r + 1) * per)
            q_refs[g][r] = qn[rows]
            k_refs[g][r] = knb[rows]
            v_refs[g][r] = vb[rows]
        keep = min(window, tm)
        if dil == 1:
            kvp_refs[g][:, 0:GROUP_WIDTH] = kn[tm - keep:, :]
            kvp_refs[g][:, GROUP_WIDTH:2 * GROUP_WIDTH] = v[tm - keep:, :]
        else:
            _scatter_residues(kv_ref, jnp.concatenate([kn, v], axis=1), dil)
            kvp_refs[g][...] = _from_slabs(kv_ref)


def _front_call(x, lw, layer, tm):
    B, S, _ = x.shape
    nt = S // tm
    row_spec = lambda width: pl.BlockSpec((None, tm, width), lambda b, i: (b, i, 0))
    weights = [lw[n] for n in FRONT_WEIGHTS]
    in_specs = ([row_spec(D_MODEL)] + [_layer_spec(w, layer) for w in weights]
                + [_const_spec(lw["pnorm"].shape)])

    out_shape, out_specs = [], []
    for _ in range(3):
        for _, dil in DILATED_GROUPS:
            out_shape.append(jax.ShapeDtypeStruct((B, dil, S // dil, GROUP_WIDTH), BF16))
            out_specs.append(pl.BlockSpec((None, dil, tm // dil, GROUP_WIDTH), lambda b, i: (b, 0, i, 0)))
    out_shape.append(jax.ShapeDtypeStruct((B, S, D_MODEL), F32))
    out_specs.append(row_spec(D_MODEL))
    out_shape.append(jax.ShapeDtypeStruct((B, S, D_MODEL), F32))
    out_specs.append(row_spec(D_MODEL))
    for window, _ in DILATED_GROUPS:
        keep = min(window, tm)
        first = nt - window // keep
        out_shape.append(jax.ShapeDtypeStruct((B, window, 2 * GROUP_WIDTH), F32))
        out_specs.append(pl.BlockSpec(
            (None, keep, 2 * GROUP_WIDTH),
            functools.partial(lambda b, i, first: (b, jnp.maximum(i - first, 0), 0), first=first)))
    for _ in range(2):
        out_shape.append(jax.ShapeDtypeStruct((B, SUBLANES, D_RNN), F32))
        out_specs.append(pl.BlockSpec((None, SUBLANES, D_RNN), lambda b, i: (b, 0, 0)))

    return pl.pallas_call(
        functools.partial(_front_kernel, tm=tm),
        grid=(B, nt),
        in_specs=in_specs,
        out_specs=out_specs,
        out_shape=out_shape,
        scratch_shapes=[pltpu.VMEM((D_MODEL // LANES, tm, LANES), F32),
                        pltpu.VMEM((2 * GROUP_WIDTH // LANES, tm, LANES), F32),
                        pltpu.VMEM((tm + SUBLANES, D_RNN), F32), pltpu.VMEM((SUBLANES, D_RNN), F32)],
        compiler_params=pltpu.CompilerParams(
            dimension_semantics=("arbitrary", "arbitrary"), vmem_limit_bytes=VMEM_LIMIT_BYTES),
        name="prompt_front",
    )(x, *weights, lw["pnorm"])


def _attn_kernel(tab_ref, bkt_ref, q_ref, kp_ref, kc_ref, vp_ref, vc_ref, o_ref, lse_ref,
                 bias_ref, kbuf_ref, vbuf_ref, *, qblocks):
    b, r, j = pl.program_id(0), pl.program_id(1), pl.program_id(2)
    BB = BAND_BLOCK
    PAIR = 2 * HEAD_DIM

    @pl.when((b == 0) & (r == 0) & (j == 0))
    def _():
        bkt = bkt_ref[...]
        for h in range(HEADS_PER_GROUP):
            acc = jnp.full(bkt.shape, -jnp.inf, F32)
            for n in range(NUM_BUCKETS):
                acc = jnp.where(bkt == n, tab_ref[n, h], acc)
            bias_ref[h // 2, (h % 2) * BB:(h % 2 + 1) * BB, :] = acc

    kbuf_ref[0:BB, :] = kp_ref[...]
    kbuf_ref[BB:, :] = kc_ref[...]
    vbuf_ref[0:BB, :] = vp_ref[...]
    vbuf_ref[BB:, :] = vc_ref[...]

    first_col = lax.broadcasted_iota(jnp.int32, (1, 2 * BB), 1) < BB
    pen = jnp.where(first_col & (j == 0), -jnp.inf, 0.0).astype(F32)
    low_half = lax.broadcasted_iota(jnp.int32, (BB, PAIR), 1) < HEAD_DIM
    ones = jnp.ones((2 * BB, PAIR), BF16)
    for t in range(qblocks):
        rows = slice(t * BB, (t + 1) * BB)
        keys = slice(t * BB, (t + 2) * BB)
        for hp in range(HEADS_PER_GROUP // 2):
            sl = slice(hp * PAIR, (hp + 1) * PAIR)
            qp = q_ref[rows, sl]
            zero = jnp.zeros_like(qp)
            lhs = jnp.concatenate([jnp.where(low_half, qp, zero), jnp.where(low_half, zero, qp)], axis=0)
            s = _dot_nt(lhs, kbuf_ref[keys, sl]) + bias_ref[hp]
            if t == 0:
                s = s + pen
            m = jnp.max(s, axis=-1, keepdims=True)
            p = jnp.exp(s - m).astype(BF16)
            oe = _dot(p, jnp.concatenate([vbuf_ref[keys, sl], ones], axis=1))
            l = oe[:, PAIR:]
            o = oe[:, :PAIR] / l
            lse = m + jnp.log(l)
            o_ref[rows, sl] = jnp.where(low_half, o[:BB], o[BB:])
            lse_ref[rows, sl] = jnp.where(low_half, lse[:BB], lse[BB:])


def _t5_bucket(dist):
    max_exact = NUM_BUCKETS // 2
    d_f = jnp.maximum(dist, 1).astype(F32)
    large = max_exact + (jnp.log(d_f / max_exact) / math.log(MAX_DISTANCE / max_exact)
                         * (NUM_BUCKETS - max_exact)).astype(jnp.int32)
    large = jnp.minimum(large, NUM_BUCKETS - 1)
    return jnp.where(dist < max_exact, dist, large)


def _band_buckets(dil):
    BB = BAND_BLOCK
    qi = jnp.arange(BB)[:, None]
    ki = jnp.arange(2 * BB)[None, :]
    sub = qi + BB - ki
    in_band = (sub >= 0) & (sub <= BB)
    return jnp.where(in_band, _t5_bucket(jnp.clip(sub, 0, BB) * dil), -1).astype(jnp.int32)


def _attn_call(q, k, v, tab, dil):
    B, _, n, C = q.shape
    BB = BAND_BLOCK
    qb = ATTN_QBLOCKS
    cur = pl.BlockSpec((None, None, qb * BB, C), lambda b, r, j: (b, r, j, 0))
    prev = pl.BlockSpec((None, None, BB, C), lambda b, r, j: (b, r, jnp.maximum(j * qb - 1, 0), 0))
    return pl.pallas_call(
        functools.partial(_attn_kernel, qblocks=qb),
        grid=(B, dil, n // (qb * BB)),
        in_specs=[pl.BlockSpec(memory_space=pltpu.SMEM),
                  pl.BlockSpec((BB, 2 * BB), lambda b, r, j: (0, 0)),
                  cur, prev, cur, prev, cur],
        out_specs=[cur, cur],
        out_shape=[jax.ShapeDtypeStruct(q.shape, F32)] * 2,
        scratch_shapes=[pltpu.VMEM((HEADS_PER_GROUP // 2, 2 * BB, 2 * BB), F32),
                        pltpu.VMEM(((qb + 1) * BB, C), BF16), pltpu.VMEM(((qb + 1) * BB, C), BF16)],
        compiler_params=pltpu.CompilerParams(
            dimension_semantics=("arbitrary", "arbitrary", "arbitrary")),
        name="prompt_attn_d%d" % dil,
    )(tab, _band_buckets(dil), q, k, k, v, v)


def _back_kernel(x_ref, ylg_ref, gatt_ref, o0_ref, o1_ref, o2_ref, l0_ref, l1_ref, l2_ref,
                 wba_ref, wo_ref, nf_ref, wfi_ref, wfo_ref, y_ref,
                 so1_ref, so2_ref, sl1_ref, sl2_ref, *, tm):
    for (_, dil), src, dst in ((DILATED_GROUPS[1], o1_ref, so1_ref), (DILATED_GROUPS[2], o2_ref, so2_ref),
                               (DILATED_GROUPS[1], l1_ref, sl1_ref), (DILATED_GROUPS[2], l2_ref, sl2_ref)):
        _scatter_residues(dst, src[...].reshape(tm, GROUP_WIDTH), dil)
    os_ = (o0_ref[0], _from_slabs(so1_ref), _from_slabs(so2_ref))
    lses = (l0_ref[0], _from_slabs(sl1_ref), _from_slabs(sl2_ref))

    top = jnp.maximum(jnp.maximum(lses[0], lses[1]), lses[2])
    num = 0.0
    den = 0.0
    for o, lse in zip(os_, lses):
        z = jnp.exp(lse - top)
        num = num + z * o
        den = den + z
    o = num / den

    y_att = _dot(o.astype(BF16), wba_ref[...])
    mix = ylg_ref[...] + gatt_ref[...] * y_att
    x1 = x_ref[...] + _dot(mix.astype(BF16), wo_ref[...])
    y_ref[...] = _ffn(x1, nf_ref[...], wfi_ref, wfo_ref)


def _back_call(x, ylg, gatt, os_, lses, lw, layer, tm):
    B, S, _ = x.shape
    row_spec = lambda width: pl.BlockSpec((None, tm, width), lambda b, i: (b, i, 0))
    res_specs = [pl.BlockSpec((None, dil, tm // dil, GROUP_WIDTH), lambda b, i: (b, 0, i, 0))
                 for _, dil in DILATED_GROUPS]
    weights = [lw[n] for n in BACK_WEIGHTS]
    in_specs = [row_spec(D_MODEL)] * 3 + res_specs * 2 + [_layer_spec(w, layer) for w in weights]
    return pl.pallas_call(
        functools.partial(_back_kernel, tm=tm),
        grid=(B, S // tm),
        in_specs=in_specs,
        out_specs=row_spec(D_MODEL),
        out_shape=jax.ShapeDtypeStruct((B, S, D_MODEL), F32),
        scratch_shapes=[pltpu.VMEM((GROUP_WIDTH // LANES, tm, LANES), F32)] * 4,
        compiler_params=pltpu.CompilerParams(
            dimension_semantics=("arbitrary", "arbitrary"), vmem_limit_bytes=VMEM_LIMIT_BYTES),
        name="prompt_back",
    )(x, ylg, gatt, *os_, *lses, *weights)


def _dec_front_kernel(x_ref, c0_ref, c1_ref, c2_ref, h0_ref, nm_ref, win_ref, wmg_ref, bmg_ref,
                      wconv_ref, bconv_ref, wrg_ref, brga_ref, brgx_ref, lam_ref, qg_ref, kg_ref,
                      wbl_ref, pn_ref,
                      q_ref, k_ref, v_ref, ylg_ref, gatt_ref, lrux_ref, hs_ref):
    xb = _rms_rows(x_ref[...], nm_ref[...]).astype(BF16)
    lru_x = _dot(xb, win_ref[:, 0:D_RNN])
    lrux_ref[...] = lru_x
    xc = bconv_ref[...] + wconv_ref[0:1, :] * c0_ref[...]
    xc = xc + wconv_ref[1:2, :] * c1_ref[...]
    xc = xc + wconv_ref[2:3, :] * c2_ref[...]
    xc = xc + wconv_ref[3:4, :] * lru_x

    lru_g = _dot(xb, win_ref[:, D_RNN:2 * D_RNN])
    xcb = xc.astype(BF16)
    sp = _softplus(-lam_ref[...])
    ys = []
    for t in range(D_RNN // RG_TILE):
        sl = slice(t * RG_TILE, (t + 1) * RG_TILE)
        g = _dot(xcb[:, sl], wrg_ref[t])
        a, mult, ig = _lru_gates(g, brga_ref[:, sl], brgx_ref[:, sl], sp[:, sl])
        h = a * h0_ref[:, sl] + (mult * ig) * xc[:, sl]
        hs_ref[:, sl] = h
        ys.append((h * _gelu_tanh(lru_g[:, sl])).astype(BF16))
    y_lru = _dot(jnp.concatenate(ys, axis=1), wbl_ref[...])
    g_lru = _sigmoid(_dot(xb, wmg_ref[:, 0:D_MODEL]) + bmg_ref[:, 0:D_MODEL])
    ylg_ref[...] = g_lru * y_lru
    gatt_ref[...] = _sigmoid(_dot(xb, wmg_ref[:, D_MODEL:2 * D_MODEL]) + bmg_ref[:, D_MODEL:2 * D_MODEL])

    pn = pn_ref[...]
    for g in range(N_GROUPS):
        c = 2 * D_RNN + g * GROUP_WIDTH
        sl = slice(g * GROUP_WIDTH, (g + 1) * GROUP_WIDTH)
        q = _dot(xb, win_ref[:, c:c + GROUP_WIDTH])
        k = _dot(xb, win_ref[:, c + ATT_WIDTH:c + ATT_WIDTH + GROUP_WIDTH])
        v = _dot(xb, win_ref[:, c + 2 * ATT_WIDTH:c + 2 * ATT_WIDTH + GROUP_WIDTH])
        q_ref[:, sl] = _head_rms(q, pn, qg_ref[...]) * (HEAD_DIM ** -0.5)
        k_ref[:, sl] = _head_rms(k, pn, kg_ref[...])
        v_ref[:, sl] = v


def _dec_front_call(x, conv_rows, h0, lw, layer):
    nb = x.shape[0]
    weights = [lw[n] for n in FRONT_WEIGHTS]
    acts = (x,) + tuple(conv_rows) + (h0,)
    widths = (ATT_WIDTH, ATT_WIDTH, ATT_WIDTH, D_MODEL, D_MODEL, D_RNN, D_RNN)
    return pl.pallas_call(
        _dec_front_kernel,
        grid=(1,),
        in_specs=([_const_spec(a.shape) for a in acts] + [_layer_spec(w, layer) for w in weights]
                  + [_const_spec(lw["pnorm"].shape)]),
        out_specs=[pl.BlockSpec((nb, w), lambda i: (0, 0)) for w in widths],
        out_shape=[jax.ShapeDtypeStruct((nb, w), F32) for w in widths],
        compiler_params=pltpu.CompilerParams(
            dimension_semantics=("arbitrary",), vmem_limit_bytes=VMEM_LIMIT_BYTES),
        name="decode_front",
    )(*acts, *weights, lw["pnorm"])


def _dec_attn_kernel(qt_ref, kt_ref, vt_ref, c0_ref, c1_ref, c2_ref, tabt_ref, bkt0_ref, bkt1_ref, bkt2_ref,
                     ot_ref, bias0_ref, bias1_ref, bias2_ref):
    b = pl.program_id(0)
    c_refs = (c0_ref, c1_ref, c2_ref)
    bias_refs = (bias0_ref, bias1_ref, bias2_ref)

    @pl.when(b == 0)
    def _():
        ot_ref[...] = jnp.zeros(ot_ref.shape, F32)
        for g, (bkt_ref, bias_ref) in enumerate(zip((bkt0_ref, bkt1_ref, bkt2_ref), bias_refs)):
            bkt = bkt_ref[...]
            acc = jnp.full(bkt.shape, -jnp.inf, F32)
            for n in range(NUM_BUCKETS):
                acc = jnp.where(bkt == n, tabt_ref[g, :, n:n + 1], acc)
            bias_ref[...] = acc

    mine = lax.broadcasted_iota(jnp.int32, qt_ref.shape, 1) == b
    column = lambda ref: jnp.sum(jnp.where(mine, ref[...], 0.0), axis=1, keepdims=True)
    qcol, kcol, vcol = column(qt_ref), column(kt_ref), column(vt_ref)
    mine_o = lax.broadcasted_iota(jnp.int32, (HEAD_DIM, ot_ref.shape[1]), 1) == b

    for h in range(HEADS_PER_GROUP):
        parts = []
        for g in range(N_GROUPS):
            rows = slice(g * GROUP_WIDTH + h * HEAD_DIM, g * GROUP_WIDTH + (h + 1) * HEAD_DIM)
            qh = qcol[rows]
            s = jnp.sum(c_refs[g][0, h] * qh, axis=0, keepdims=True) + bias_refs[g][h:h + 1, :]
            s0 = jnp.sum(kcol[rows] * qh, axis=0, keepdims=True) + tabt_ref[g, h:h + 1, 0:1]
            m = jnp.maximum(jnp.max(s, axis=1, keepdims=True), s0)
            p = jnp.exp(s - m)
            p0 = jnp.exp(s0 - m)
            l = jnp.sum(p, axis=1, keepdims=True) + p0
            acc = jnp.sum(c_refs[g][1, h] * p, axis=1, keepdims=True) + p0 * vcol[rows]
            parts.append((acc, m, l))
        m_max = jnp.maximum(jnp.maximum(parts[0][1], parts[1][1]), parts[2][1])
        num = 0.0
        den = 0.0
        for acc, m, l in parts:
            w = jnp.exp(m - m_max)
            num = num + w * acc
            den = den + l * w
        hs = slice(h * HEAD_DIM, (h + 1) * HEAD_DIM)
        ot_ref[hs, :] = jnp.where(mine_o, num / den, ot_ref[hs, :])


def _dec_attn_call(qt, kt, vt, caches_t, layer, tabt):
    nb = qt.shape[1]
    bkts = []
    for window, dil in DILATED_GROUPS:
        w = jnp.arange(window)
        bkt = jnp.where(w % dil == 0, _t5_bucket(window - w), -1).astype(jnp.int32)
        bkts.append(jnp.broadcast_to(bkt[None, :], (HEADS_PER_GROUP, window)))
    cache_spec = lambda c: pl.BlockSpec((None, None) + c.shape[2:], lambda i: (layer, i, 0, 0, 0, 0))
    return pl.pallas_call(
        _dec_attn_kernel,
        grid=(nb,),
        in_specs=[_const_spec(qt.shape)] * 3 + [cache_spec(c) for c in caches_t]
        + [_const_spec(tabt.shape)] + [_const_spec(b.shape) for b in bkts],
        out_specs=pl.BlockSpec((GROUP_WIDTH, nb), lambda i: (0, 0)),
        out_shape=jax.ShapeDtypeStruct((GROUP_WIDTH, nb), F32),
        scratch_shapes=[pltpu.VMEM((HEADS_PER_GROUP, window), F32) for window, _ in DILATED_GROUPS],
        compiler_params=pltpu.CompilerParams(
            dimension_semantics=("arbitrary",), vmem_limit_bytes=VMEM_LIMIT_BYTES),
        name="decode_attn",
    )(qt, kt, vt, *caches_t, tabt, *bkts)


def _dec_back_kernel(x_ref, ylg_ref, gatt_ref, o_ref, wba_ref, wo_ref, nf_ref, wfi_ref, wfo_ref, y_ref):
    y_att = _dot(o_ref[...].astype(BF16), wba_ref[...])
    mix = ylg_ref[...] + gatt_ref[...] * y_att
    x1 = x_ref[...] + _dot(mix.astype(BF16), wo_ref[...])
    y_ref[...] = _ffn(x1, nf_ref[...], wfi_ref, wfo_ref)


def _dec_back_call(x, ylg, gatt, o, lw, layer):
    weights = [lw[n] for n in BACK_WEIGHTS]
    acts = (x, ylg, gatt, o)
    return pl.pallas_call(
        _dec_back_kernel,
        grid=(1,),
        in_specs=[_const_spec(a.shape) for a in acts] + [_layer_spec(w, layer) for w in weights],
        out_specs=pl.BlockSpec(x.shape, lambda i: (0, 0)),
        out_shape=jax.ShapeDtypeStruct(x.shape, F32),
        compiler_params=pltpu.CompilerParams(
            dimension_semantics=("arbitrary",), vmem_limit_bytes=VMEM_LIMIT_BYTES),
        name="decode_back",
    )(*acts, *weights)


def _prepare_weights(norm_mix, w_in, w_conv, b_conv, w_rg_a, b_rg_a, w_rg_x, b_rg_x, lru_lambda,
                     q_gain, k_gain, w_merge, b_merge, w_branch_lru, w_branch_att, w_o, norm_ffn,
                     w_ffn_in, w_ffn_out):
    depth = norm_mix.shape[0]
    row = lambda t: t.reshape(depth, 1, -1).astype(F32)
    per_tile = RG_TILE // LRU_BLOCK
    tiles = D_RNN // RG_TILE

    def block_diag(w):
        w = w.reshape(depth, tiles, per_tile, LRU_BLOCK, LRU_BLOCK)
        eye = jnp.eye(per_tile, dtype=w.dtype)
        return jnp.einsum("ltnij,nm->ltnimj", w, eye).reshape(depth, tiles, RG_TILE, RG_TILE)

    head = jnp.arange(MXU_DIM) // HEAD_DIM
    tile_gain = lambda t: jnp.tile(t.reshape(depth, 1, HEAD_DIM), (1, 1, HEADS_PER_GROUP)).astype(F32)
    return dict(
        norm_mix=row(norm_mix), w_in=w_in.astype(BF16), w_merge=w_merge.astype(BF16),
        b_merge=row(b_merge), w_conv=w_conv.astype(F32), b_conv=row(b_conv),
        w_rg=jnp.concatenate([block_diag(w_rg_a), block_diag(w_rg_x)], axis=3).astype(BF16),
        b_rg_a=row(b_rg_a), b_rg_x=row(b_rg_x), lam=row(lru_lambda),
        q_gain=tile_gain(q_gain), k_gain=tile_gain(k_gain),
        w_branch_lru=w_branch_lru.astype(BF16),
        pnorm=((head[:, None] == head[None, :]).astype(F32) / HEAD_DIM).astype(BF16),
        w_branch_att=w_branch_att.astype(BF16), w_o=w_o.astype(BF16), norm_ffn=row(norm_ffn),
        w_ffn_in=w_ffn_in.astype(BF16), w_ffn_out=w_ffn_out.astype(BF16))


def kernel(x_prompt, x_sample, cache_kv_g0, cache_kv_g1, cache_kv_g2, state_conv, state_h, rel_bias,
           norm_mix, w_in, w_conv, b_conv, w_rg_a, b_rg_a, w_rg_x, b_rg_x, lru_lambda, q_gain, k_gain,
           w_merge, b_merge, w_branch_lru, w_branch_att, w_o, norm_ffn, w_ffn_in, w_ffn_out):
    B, S, _ = x_prompt.shape
    nb = x_sample.shape[0]
    depth = norm_mix.shape[0]
    lw = _prepare_weights(norm_mix, w_in, w_conv, b_conv, w_rg_a, b_rg_a, w_rg_x, b_rg_x, lru_lambda,
                          q_gain, k_gain, w_merge, b_merge, w_branch_lru, w_branch_att, w_o, norm_ffn,
                          w_ffn_in, w_ffn_out)
    rel_bias = rel_bias.astype(F32)
    tabt = rel_bias.reshape(NUM_BUCKETS, N_GROUPS, HEADS_PER_GROUP).transpose(1, 2, 0)
    caches_t = [jnp.transpose(c.astype(F32), (0, 1, 3, 4, 5, 2))
                for c in (cache_kv_g0, cache_kv_g1, cache_kv_g2)]

    yp = x_prompt
    ys = x_sample.reshape(nb, D_MODEL)
    kvp = [[] for _ in range(N_GROUPS)]
    kvs = [[] for _ in range(N_GROUPS)]
    conv_p, h_p, conv_s, h_s = [], [], [], []
    for l in range(depth):
        (q0, q1, q2, k0, k1, k2, v0, v1, v2, ylg, gatt, kv0, kv1, kv2, ctail, hfin) = _front_call(
            yp, lw, l, TM_PROMPT)
        os_, lses = [], []
        for g, (qg, kg, vg) in enumerate(((q0, k0, v0), (q1, k1, v1), (q2, k2, v2))):
            tab = rel_bias[:, g * HEADS_PER_GROUP:(g + 1) * HEADS_PER_GROUP]
            o, lse = _attn_call(qg, kg, vg, tab, DILATED_GROUPS[g][1])
            os_.append(o)
            lses.append(lse)
        yp = _back_call(yp, ylg, gatt, os_, lses, lw, l, TM_PROMPT)
        for g, kv in enumerate((kv0, kv1, kv2)):
            kvp[g].append(kv.reshape(B, kv.shape[1], 2, HEADS_PER_GROUP, HEAD_DIM))
        conv_p.append(ctail[:, SUBLANES - (CONV_WIDTH - 1):, :])
        h_p.append(hfin[:, 0, :])

        sc = state_conv[l].astype(F32)
        conv_rows = [sc[:, r, :] for r in range(CONV_WIDTH - 1)]
        qs, ks, vs, ylg_s, gatt_s, lrux_s, hs_s = _dec_front_call(ys, conv_rows, state_h[l].astype(F32), lw, l)
        o_t = _dec_attn_call(qs.T, ks.T, vs.T, caches_t, l, tabt)
        ys = _dec_back_call(ys, ylg_s, gatt_s, o_t.T, lw, l)
        for g in range(N_GROUPS):
            sl = slice(g * GROUP_WIDTH, (g + 1) * GROUP_WIDTH)
            kvs[g].append(jnp.stack([ks[:, sl], vs[:, sl]], axis=1).reshape(
                nb, 1, 2, HEADS_PER_GROUP, HEAD_DIM))
        conv_s.append(jnp.stack(conv_rows[1:] + [lrux_s], axis=1))
        h_s.append(hs_s)

    stack = jnp.stack
    return (yp, ys.reshape(nb, 1, D_MODEL),
            stack(kvp[0]), stack(kvp[1]), stack(kvp[2]), stack(conv_p), stack(h_p),
            stack(kvs[0]), stack(kvs[1]), stack(kvs[2]), stack(conv_s), stack(h_s))
```

```python
import functools
import math

import jax
import jax.numpy as jnp
from jax import lax
from jax.experimental import pallas as pl
from jax.experimental.pallas import tpu as pltpu

D_MODEL = 1024
D_RNN = D_MODEL
N_LRU_BLOCKS = 16
LRU_BLOCK = D_RNN // N_LRU_BLOCKS
CONV_WIDTH = 4
LRU_C = 8.0
HEAD_DIM = 64
HEADS_PER_GROUP = 8
DILATED_GROUPS = ((128, 1), (512, 4), (2048, 16))
N_GROUPS = len(DILATED_GROUPS)
N_ATT_HEADS = N_GROUPS * HEADS_PER_GROUP
ATT_WIDTH = N_ATT_HEADS * HEAD_DIM
GROUP_WIDTH = HEADS_PER_GROUP * HEAD_DIM
BAND_BLOCK = 128
NUM_BUCKETS = 32
MAX_DISTANCE = 2048
D_FF = 2816
EPS = 1e-6

F32 = jnp.float32
BF16 = jnp.bfloat16

MXU_DIM = 256
SUBLANES = 8
LANES = 128
VMEM_LIMIT_BYTES = 56 * 1024 * 1024

TM_PROMPT = 256
ATTN_QBLOCKS = 4
RG_TILE = MXU_DIM
FFN_CHUNKS = ((0, 1536), (1536, 1280))


def _dot(a, b):
    return jnp.dot(a, b, preferred_element_type=F32)


def _dot_nt(a, b):
    return lax.dot_general(a, b, (((1,), (1,)), ((), ())), preferred_element_type=F32)


def _split_dot(x, m):
    hi = x.astype(BF16)
    lo = (x - hi.astype(F32)).astype(BF16)
    return _dot(hi, m) + _dot(lo, m)


def _sigmoid(x):
    return 0.5 * (jnp.tanh(0.5 * x) + 1.0)


def _gelu_tanh(x):
    c = math.sqrt(2.0 / math.pi)
    return 0.5 * x * (1.0 + jnp.tanh(c * (x + 0.044715 * (x * x * x))))


def _softplus(z):
    return jnp.maximum(z, 0.0) + jnp.log1p(jnp.exp(-jnp.abs(z)))


def _rms_rows(x, g):
    y = x * lax.rsqrt(jnp.mean(x * x, axis=-1, keepdims=True) + EPS)
    return y * g


def _head_rms(t, pn, gain):
    t2 = (t * t).astype(BF16)
    tiles = [_dot(t2[:, c * MXU_DIM:(c + 1) * MXU_DIM], pn) for c in range(t.shape[1] // MXU_DIM)]
    ms = tiles[0] if len(tiles) == 1 else jnp.concatenate(tiles, axis=1)
    return (t * lax.rsqrt(ms + EPS)) * gain


def _lru_gates(g, b_a, b_x, sp):
    w = g.shape[1] // 2
    r = _sigmoid(g[:, :w] + b_a)
    ig = _sigmoid(g[:, w:] + b_x)
    log_a = (-LRU_C * r) * sp
    a = jnp.exp(log_a)
    mult = jnp.sqrt(-jnp.tanh(log_a) * (a * a + 1.0))
    return a, mult, ig


def _scan_rows(a, u, h_in):
    rows, c = a.shape
    groups = rows // SUBLANES
    a3 = a.reshape(groups, SUBLANES, c)
    u3 = u.reshape(groups, SUBLANES, c)
    row = lax.broadcasted_iota(jnp.int32, (groups, SUBLANES, c), 1)
    shift = 1
    while shift < SUBLANES:
        ok = row >= shift
        a_sh = jnp.where(ok, pltpu.roll(a3, shift, axis=1), 1.0)
        u_sh = jnp.where(ok, pltpu.roll(u3, shift, axis=1), 0.0)
        u3 = u3 + a3 * u_sh
        a3 = a3 * a_sh
        shift *= 2
    out = []
    h = h_in
    for g in range(groups):
        hg = a3[g] * h + u3[g]
        out.append(hg)
        h = hg[SUBLANES - 1:SUBLANES, :]
    return jnp.concatenate(out, axis=0), h


def _to_slabs(slab_ref, t):
    for c in range(slab_ref.shape[0]):
        slab_ref[c] = t[:, c * LANES:(c + 1) * LANES]


def _from_slabs(slab_ref):
    return jnp.concatenate([slab_ref[c] for c in range(slab_ref.shape[0])], axis=1)


def _gather_residues(slab_ref, dil):
    per = slab_ref.shape[1] // dil
    cols = []
    for c in range(slab_ref.shape[0]):
        cols.append(jnp.concatenate(
            [slab_ref[c, pl.ds(r, per, stride=dil), :] for r in range(dil)], axis=0))
    return jnp.concatenate(cols, axis=1)


def _scatter_residues(slab_ref, t, dil, first_slab=0):
    per = slab_ref.shape[1] // dil
    for c in range(t.shape[1] // LANES):
        for r in range(dil):
            slab_ref[first_slab + c, pl.ds(r, per, stride=dil), :] = (
                t[r * per:(r + 1) * per, c * LANES:(c + 1) * LANES])


def _ffn(x1, nf, wfi_ref, wfo_ref):
    xb = _rms_rows(x1, nf).astype(BF16)
    acc = x1
    for start, width in FFN_CHUNKS:
        gate = _dot(xb, wfi_ref[:, start:start + width])
        up = _dot(xb, wfi_ref[:, D_FF + start:D_FF + start + width])
        hid = (gate * _sigmoid(gate)) * up
        acc = acc + _dot(hid.astype(BF16), wfo_ref[start:start + width, :])
    return acc


def _layer_spec(w, layer):
    tail = (0,) * (w.ndim - 1)
    return pl.BlockSpec((None,) + w.shape[1:], lambda *_: (layer,) + tail, pipeline_mode=pl.Buffered(1))


def _const_spec(shape):
    zeros = (0,) * len(shape)
    return pl.BlockSpec(shape, lambda *_: zeros, pipeline_mode=pl.Buffered(1))


FRONT_WEIGHTS = ("norm_mix", "w_in", "w_merge", "b_merge", "w_conv", "b_conv", "w_rg", "b_rg_a", "b_rg_x",
                 "lam", "q_gain", "k_gain", "w_branch_lru")
BACK_WEIGHTS = ("w_branch_att", "w_o", "norm_ffn", "w_ffn_in", "w_ffn_out")


def _front_kernel(x_ref, nm_ref, win_ref, wmg_ref, bmg_ref, wconv_ref, bconv_ref, wrg_ref,
                  brga_ref, brgx_ref, lam_ref, qg_ref, kg_ref, wbl_ref, pn_ref,
                  q0_ref, q1_ref, q2_ref, k0_ref, k1_ref, k2_ref, v0_ref, v1_ref, v2_ref,
                  ylg_ref, gatt_ref, kvp0_ref, kvp1_ref, kvp2_ref, convp_ref, hp_ref,
                  xn_ref, kv_ref, ext_ref, h_ref, xb_ref, xg_ref, xc_ref, xcb_ref, gl_ref, y_ref, *, tm):
    i = pl.program_id(1)
    n_tiles = D_RNN // RG_TILE

    @pl.when(i == 0)
    def _():
        ext_ref[0:SUBLANES, :] = jnp.zeros((SUBLANES, D_RNN), F32)
        h_ref[...] = jnp.zeros((SUBLANES, D_RNN), F32)

    xn = _rms_rows(x_ref[...], nm_ref[...])
    _to_slabs(xn_ref, xn)
    xb_ref[...] = xn.astype(BF16)

    for t in range(n_tiles):
        sl = slice(t * RG_TILE, (t + 1) * RG_TILE)
        lru_x = _dot(xb_ref[...], win_ref[:, sl])
        ext_ref[SUBLANES:SUBLANES + tm, sl] = lru_x
        xc = bconv_ref[:, sl] + wconv_ref[0:1, sl] * ext_ref[SUBLANES - 3:SUBLANES - 3 + tm, sl]
        xc = xc + wconv_ref[1:2, sl] * ext_ref[SUBLANES - 2:SUBLANES - 2 + tm, sl]
        xc = xc + wconv_ref[2:3, sl] * ext_ref[SUBLANES - 1:SUBLANES - 1 + tm, sl]
        xc = xc + wconv_ref[3:4, sl] * lru_x
        xc_ref[:, sl] = xc
        xcb_ref[:, sl] = xc.astype(BF16)
        gl_ref[:, sl] = _gelu_tanh(_dot(xb_ref[...], win_ref[:, D_RNN + t * RG_TILE:D_RNN + (t + 1) * RG_TILE]))
    tail = ext_ref[tm:tm + SUBLANES, :]
    ext_ref[0:SUBLANES, :] = tail
    convp_ref[...] = tail

    sp = _softplus(-lam_ref[...])
    first_row = (lax.broadcasted_iota(jnp.int32, (tm, RG_TILE), 0) == 0) & (i == 0)

    def lru_tile(t):
        sl = slice(t * RG_TILE, (t + 1) * RG_TILE)
        g = _dot(xcb_ref[:, sl], wrg_ref[t])
        a, mult, ig = _lru_gates(g, brga_ref[:, sl], brgx_ref[:, sl], sp[:, sl])
        mult = jnp.where(first_row, 1.0, mult)
        u = (mult * ig) * xc_ref[:, sl]
        hs, h_last = _scan_rows(a, u, h_ref[0:1, sl])
        h_ref[:, sl] = jnp.broadcast_to(h_last, (SUBLANES, RG_TILE))
        y_ref[:, sl] = (hs * gl_ref[:, sl]).astype(BF16)

    pn = pn_ref[...]
    q_refs = (q0_ref, q1_ref, q2_ref)
    k_refs = (k0_ref, k1_ref, k2_ref)
    v_refs = (v0_ref, v1_ref, v2_ref)
    kvp_refs = (kvp0_ref, kvp1_ref, kvp2_ref)

    def attn_operands(g):
        window, dil = DILATED_GROUPS[g]
        per = tm // dil
        keep = min(window, tm)
        if dil == 1:
            src = xb_ref
        else:
            xg_ref[...] = _gather_residues(xn_ref, dil).astype(BF16)
            src = xg_ref
        for half in range(GROUP_WIDTH // MXU_DIM):
            c = 2 * D_RNN + g * GROUP_WIDTH + half * MXU_DIM
            hl = slice(half * MXU_DIM, (half + 1) * MXU_DIM)
            vl = slice(GROUP_WIDTH + half * MXU_DIM, GROUP_WIDTH + (half + 1) * MXU_DIM)
            q = _dot(src[...], win_ref[:, c:c + MXU_DIM])
            k = _dot(src[...], win_ref[:, c + ATT_WIDTH:c + ATT_WIDTH + MXU_DIM])
            v = _dot(src[...], win_ref[:, c + 2 * ATT_WIDTH:c + 2 * ATT_WIDTH + MXU_DIM])
            qn = (_head_rms(q, pn, qg_ref[:, hl]) * (HEAD_DIM ** -0.5)).astype(BF16)
            kn = _head_rms(k, pn, kg_ref[:, hl])
            knb = kn.astype(BF16)
            vb = v.astype(BF16)
            for r in range(dil):
                rows = slice(r * per, (r + 1) * per)
                q_refs[g][r, :, hl] = qn[rows]
                k_refs[g][r, :, hl] = knb[rows]
                v_refs[g][r, :, hl] = vb[rows]
            if dil == 1:
                kvp_refs[g][:, hl] = kn[tm - keep:, :]
                kvp_refs[g][:, vl] = v[tm - keep:, :]
            else:
                _scatter_residues(kv_ref, kn, dil, first_slab=hl.start // LANES)
                _scatter_residues(kv_ref, v, dil, first_slab=vl.start // LANES)
        if dil != 1:
            kvp_refs[g][...] = _from_slabs(kv_ref)

    for t in range(n_tiles):
        lru_tile(t)
        if t < N_GROUPS:
            attn_operands(t)
    hp_ref[...] = h_ref[...]
    for t in range(n_tiles):
        sl = slice(t * RG_TILE, (t + 1) * RG_TILE)
        al = slice(D_MODEL + t * RG_TILE, D_MODEL + (t + 1) * RG_TILE)
        gatt_ref[:, sl] = _sigmoid(_dot(xb_ref[...], wmg_ref[:, al]) + bmg_ref[:, al]).astype(gatt_ref.dtype)
        g_lru = _sigmoid(_dot(xb_ref[...], wmg_ref[:, sl]) + bmg_ref[:, sl])
        ylg_ref[:, sl] = (g_lru * _dot(y_ref[...], wbl_ref[:, sl])).astype(ylg_ref.dtype)


def _front_call(x, lw, layer, tm):
    B, S, _ = x.shape
    nt = S // tm
    row_spec = lambda width: pl.BlockSpec((None, tm, width), lambda b, i: (b, i, 0))
    weights = [lw[n] for n in FRONT_WEIGHTS]
    in_specs = ([row_spec(D_MODEL)] + [_layer_spec(w, layer) for w in weights]
                + [_const_spec(lw["pnorm"].shape)])

    out_shape, out_specs = [], []
    for _ in range(3):
        for _, dil in DILATED_GROUPS:
            out_shape.append(jax.ShapeDtypeStruct((B, dil, S // dil, GROUP_WIDTH), BF16))
            out_specs.append(pl.BlockSpec((None, dil, tm // dil, GROUP_WIDTH), lambda b, i: (b, 0, i, 0)))
    out_shape.append(jax.ShapeDtypeStruct((B, S, D_MODEL), BF16))
    out_specs.append(row_spec(D_MODEL))
    out_shape.append(jax.ShapeDtypeStruct((B, S, D_MODEL), BF16))
    out_specs.append(row_spec(D_MODEL))
    for window, _ in DILATED_GROUPS:
        keep = min(window, tm)
        first = nt - window // keep
        out_shape.append(jax.ShapeDtypeStruct((B, window, 2 * GROUP_WIDTH), F32))
        out_specs.append(pl.BlockSpec(
            (None, keep, 2 * GROUP_WIDTH),
            functools.partial(lambda b, i, first: (b, jnp.maximum(i - first, 0), 0), first=first)))
    for _ in range(2):
        out_shape.append(jax.ShapeDtypeStruct((B, SUBLANES, D_RNN), F32))
        out_specs.append(pl.BlockSpec((None, SUBLANES, D_RNN), lambda b, i: (b, 0, 0)))

    return pl.pallas_call(
        functools.partial(_front_kernel, tm=tm),
        grid=(B, nt),
        in_specs=in_specs,
        out_specs=out_specs,
        out_shape=out_shape,
        scratch_shapes=[pltpu.VMEM((D_MODEL // LANES, tm, LANES), F32),
                        pltpu.VMEM((2 * GROUP_WIDTH // LANES, tm, LANES), F32),
                        pltpu.VMEM((tm + SUBLANES, D_RNN), F32), pltpu.VMEM((SUBLANES, D_RNN), F32),
                        pltpu.VMEM((tm, D_MODEL), BF16), pltpu.VMEM((tm, D_MODEL), BF16),
                        pltpu.VMEM((tm, D_RNN), F32), pltpu.VMEM((tm, D_RNN), BF16),
                        pltpu.VMEM((tm, D_RNN), F32), pltpu.VMEM((tm, D_RNN), BF16)],
        compiler_params=pltpu.CompilerParams(
            dimension_semantics=("arbitrary", "arbitrary"), vmem_limit_bytes=VMEM_LIMIT_BYTES),
        name="prompt_front",
    )(x, *weights, lw["pnorm"])


def _attn_kernel(tab_ref, bkt_ref, q_ref, kp_ref, kc_ref, vp_ref, vc_ref, o_ref, lse_ref,
                 bias_ref, kbuf_ref, vbuf_ref, *, qblocks):
    b, r, j = pl.program_id(0), pl.program_id(1), pl.program_id(2)
    BB = BAND_BLOCK
    PAIR = 2 * HEAD_DIM

    @pl.when((b == 0) & (r == 0) & (j == 0))
    def _():
        bkt = bkt_ref[...]
        for h in range(HEADS_PER_GROUP):
            acc = jnp.full(bkt.shape, -jnp.inf, F32)
            for n in range(NUM_BUCKETS):
                acc = jnp.where(bkt == n, tab_ref[n, h], acc)
            bias_ref[h // 2, (h % 2) * BB:(h % 2 + 1) * BB, :] = acc

    kbuf_ref[0:BB, :] = kp_ref[...]
    kbuf_ref[BB:, :] = kc_ref[...]
    vbuf_ref[0:BB, :] = vp_ref[...]
    vbuf_ref[BB:, :] = vc_ref[...]

    first_col = lax.broadcasted_iota(jnp.int32, (1, 2 * BB), 1) < BB
    pen = jnp.where(first_col & (j == 0), -jnp.inf, 0.0).astype(F32)
    low_half = lax.broadcasted_iota(jnp.int32, (BB, PAIR), 1) < HEAD_DIM
    ones = jnp.ones((2 * BB, PAIR), BF16)
    for t in range(qblocks):
        rows = slice(t * BB, (t + 1) * BB)
        keys = slice(t * BB, (t + 2) * BB)
        for hp in range(HEADS_PER_GROUP // 2):
            sl = slice(hp * PAIR, (hp + 1) * PAIR)
            qp = q_ref[rows, sl]
            zero = jnp.zeros_like(qp)
            lhs = jnp.concatenate([jnp.where(low_half, qp, zero), jnp.where(low_half, zero, qp)], axis=0)
            s = _dot_nt(lhs, kbuf_ref[keys, sl]) + bias_ref[hp]
            if t == 0:
                s = s + pen
            m = jnp.max(s, axis=-1, keepdims=True)
            p = jnp.exp(s - m).astype(BF16)
            oe = _dot(p, jnp.concatenate([vbuf_ref[keys, sl], ones], axis=1))
            l = oe[:, PAIR:]
            o = oe[:, :PAIR] / l
            lse = m + jnp.log(l)
            o_ref[rows, sl] = jnp.where(low_half, o[:BB], o[BB:])
            lse_ref[rows, sl] = jnp.where(low_half, lse[:BB], lse[BB:])


def _t5_bucket(dist):
    max_exact = NUM_BUCKETS // 2
    d_f = jnp.maximum(dist, 1).astype(F32)
    large = max_exact + (jnp.log(d_f / max_exact) / math.log(MAX_DISTANCE / max_exact)
                         * (NUM_BUCKETS - max_exact)).astype(jnp.int32)
    large = jnp.minimum(large, NUM_BUCKETS - 1)
    return jnp.where(dist < max_exact, dist, large)


def _band_buckets(dil):
    BB = BAND_BLOCK
    qi = jnp.arange(BB)[:, None]
    ki = jnp.arange(2 * BB)[None, :]
    sub = qi + BB - ki
    in_band = (sub >= 0) & (sub <= BB)
    return jnp.where(in_band, _t5_bucket(jnp.clip(sub, 0, BB) * dil), -1).astype(jnp.int32)


def _attn_call(q, k, v, tab, dil):
    B, _, n, C = q.shape
    BB = BAND_BLOCK
    qb = ATTN_QBLOCKS
    cur = pl.BlockSpec((None, None, qb * BB, C), lambda b, r, j: (b, r, j, 0))
    prev = pl.BlockSpec((None, None, BB, C), lambda b, r, j: (b, r, jnp.maximum(j * qb - 1, 0), 0))
    return pl.pallas_call(
        functools.partial(_attn_kernel, qblocks=qb),
        grid=(B, dil, n // (qb * BB)),
        in_specs=[pl.BlockSpec(memory_space=pltpu.SMEM),
                  pl.BlockSpec((BB, 2 * BB), lambda b, r, j: (0, 0)),
                  cur, prev, cur, prev, cur],
        out_specs=[cur, cur],
        out_shape=[jax.ShapeDtypeStruct(q.shape, F32)] * 2,
        scratch_shapes=[pltpu.VMEM((HEADS_PER_GROUP // 2, 2 * BB, 2 * BB), F32),
                        pltpu.VMEM(((qb + 1) * BB, C), BF16), pltpu.VMEM(((qb + 1) * BB, C), BF16)],
        compiler_params=pltpu.CompilerParams(
            dimension_semantics=("arbitrary", "arbitrary", "arbitrary")),
        name="prompt_attn_d%d" % dil,
    )(tab, _band_buckets(dil), q, k, k, v, v)


def _back_kernel(x_ref, ylg_ref, gatt_ref, o0_ref, o1_ref, o2_ref, l0_ref, l1_ref, l2_ref,
                 wba_ref, wo_ref, nf_ref, wfi_ref, wfo_ref, y_ref,
                 so1_ref, so2_ref, sl1_ref, sl2_ref, *, tm):
    for (_, dil), src, dst in ((DILATED_GROUPS[1], o1_ref, so1_ref), (DILATED_GROUPS[2], o2_ref, so2_ref),
                               (DILATED_GROUPS[1], l1_ref, sl1_ref), (DILATED_GROUPS[2], l2_ref, sl2_ref)):
        _scatter_residues(dst, src[...].reshape(tm, GROUP_WIDTH), dil)
    os_ = (o0_ref[0], _from_slabs(so1_ref), _from_slabs(so2_ref))
    lses = (l0_ref[0], _from_slabs(sl1_ref), _from_slabs(sl2_ref))

    top = jnp.maximum(jnp.maximum(lses[0], lses[1]), lses[2])
    num = 0.0
    den = 0.0
    for o, lse in zip(os_, lses):
        z = jnp.exp(lse - top)
        num = num + z * o
        den = den + z
    o = num / den

    y_att = _dot(o.astype(BF16), wba_ref[...])
    mix = ylg_ref[...] + gatt_ref[...] * y_att
    x1 = x_ref[...] + _dot(mix.astype(BF16), wo_ref[...])
    y_ref[...] = _ffn(x1, nf_ref[...], wfi_ref, wfo_ref)


def _back_call(x, ylg, gatt, os_, lses, lw, layer, tm):
    B, S, _ = x.shape
    row_spec = lambda width: pl.BlockSpec((None, tm, width), lambda b, i: (b, i, 0))
    res_specs = [pl.BlockSpec((None, dil, tm // dil, GROUP_WIDTH), lambda b, i: (b, 0, i, 0))
                 for _, dil in DILATED_GROUPS]
    weights = [lw[n] for n in BACK_WEIGHTS]
    in_specs = [row_spec(D_MODEL)] * 3 + res_specs * 2 + [_layer_spec(w, layer) for w in weights]
    return pl.pallas_call(
        functools.partial(_back_kernel, tm=tm),
        grid=(B, S // tm),
        in_specs=in_specs,
        out_specs=row_spec(D_MODEL),
        out_shape=jax.ShapeDtypeStruct((B, S, D_MODEL), F32),
        scratch_shapes=[pltpu.VMEM((GROUP_WIDTH // LANES, tm, LANES), F32)] * 4,
        compiler_params=pltpu.CompilerParams(
            dimension_semantics=("arbitrary", "arbitrary"), vmem_limit_bytes=VMEM_LIMIT_BYTES),
        name="prompt_back",
    )(x, ylg, gatt, *os_, *lses, *weights)


def _dec_front_kernel(x_ref, c0_ref, c1_ref, c2_ref, h0_ref, nm_ref, win_ref, wmg_ref, bmg_ref,
                      wconv_ref, bconv_ref, wrg_ref, brga_ref, brgx_ref, lam_ref, qg_ref, kg_ref,
                      wbl_ref, pn_ref,
                      q_ref, k_ref, v_ref, ylg_ref, gatt_ref, lrux_ref, hs_ref):
    xb = _rms_rows(x_ref[...], nm_ref[...]).astype(BF16)
    lru_x = _dot(xb, win_ref[:, 0:D_RNN])
    lrux_ref[...] = lru_x
    xc = bconv_ref[...] + wconv_ref[0:1, :] * c0_ref[...]
    xc = xc + wconv_ref[1:2, :] * c1_ref[...]
    xc = xc + wconv_ref[2:3, :] * c2_ref[...]
    xc = xc + wconv_ref[3:4, :] * lru_x

    lru_g = _dot(xb, win_ref[:, D_RNN:2 * D_RNN])
    xcb = xc.astype(BF16)
    sp = _softplus(-lam_ref[...])
    ys = []
    for t in range(D_RNN // RG_TILE):
        sl = slice(t * RG_TILE, (t + 1) * RG_TILE)
        g = _dot(xcb[:, sl], wrg_ref[t])
        a, mult, ig = _lru_gates(g, brga_ref[:, sl], brgx_ref[:, sl], sp[:, sl])
        h = a * h0_ref[:, sl] + (mult * ig) * xc[:, sl]
        hs_ref[:, sl] = h
        ys.append((h * _gelu_tanh(lru_g[:, sl])).astype(BF16))
    y_lru = _dot(jnp.concatenate(ys, axis=1), wbl_ref[...])
    g_lru = _sigmoid(_dot(xb, wmg_ref[:, 0:D_MODEL]) + bmg_ref[:, 0:D_MODEL])
    ylg_ref[...] = g_lru * y_lru
    gatt_ref[...] = _sigmoid(_dot(xb, wmg_ref[:, D_MODEL:2 * D_MODEL]) + bmg_ref[:, D_MODEL:2 * D_MODEL])

    pn = pn_ref[...]
    for g in range(N_GROUPS):
        c = 2 * D_RNN + g * GROUP_WIDTH
        sl = slice(g * GROUP_WIDTH, (g + 1) * GROUP_WIDTH)
        q = _dot(xb, win_ref[:, c:c + GROUP_WIDTH])
        k = _dot(xb, win_ref[:, c + ATT_WIDTH:c + ATT_WIDTH + GROUP_WIDTH])
        v = _dot(xb, win_ref[:, c + 2 * ATT_WIDTH:c + 2 * ATT_WIDTH + GROUP_WIDTH])
        q_ref[:, sl] = _head_rms(q, pn, qg_ref[...]) * (HEAD_DIM ** -0.5)
        k_ref[:, sl] = _head_rms(k, pn, kg_ref[...])
        v_ref[:, sl] = v


def _dec_front_call(x, conv_rows, h0, lw, layer):
    nb = x.shape[0]
    weights = [lw[n] for n in FRONT_WEIGHTS]
    acts = (x,) + tuple(conv_rows) + (h0,)
    widths = (ATT_WIDTH, ATT_WIDTH, ATT_WIDTH, D_MODEL, D_MODEL, D_RNN, D_RNN)
    return pl.pallas_call(
        _dec_front_kernel,
        grid=(1,),
        in_specs=([_const_spec(a.shape) for a in acts] + [_layer_spec(w, layer) for w in weights]
                  + [_const_spec(lw["pnorm"].shape)]),
        out_specs=[pl.BlockSpec((nb, w), lambda i: (0, 0)) for w in widths],
        out_shape=[jax.ShapeDtypeStruct((nb, w), F32) for w in widths],
        compiler_params=pltpu.CompilerParams(
            dimension_semantics=("arbitrary",), vmem_limit_bytes=VMEM_LIMIT_BYTES),
        name="decode_front",
    )(*acts, *weights, lw["pnorm"])


def _dec_attn_kernel(qt_ref, kt_ref, vt_ref, c0_ref, c1_ref, c2_ref, tabt_ref, bkt0_ref, bkt1_ref, bkt2_ref,
                     ot_ref, bias0_ref, bias1_ref, bias2_ref, s0_ref, s1_ref, s2_ref):
    b = pl.program_id(0)
    c_refs = (c0_ref, c1_ref, c2_ref)
    bias_refs = (bias0_ref, bias1_ref, bias2_ref)
    s_refs = (s0_ref, s1_ref, s2_ref)

    @pl.when(b == 0)
    def _():
        ot_ref[...] = jnp.zeros(ot_ref.shape, F32)
        for g, (bkt_ref, bias_ref) in enumerate(zip((bkt0_ref, bkt1_ref, bkt2_ref), bias_refs)):
            bkt = bkt_ref[...]
            acc = jnp.full(bkt.shape, -jnp.inf, F32)
            for n in range(NUM_BUCKETS):
                acc = jnp.where(bkt == n, tabt_ref[g, :, n:n + 1], acc)
            bias_ref[...] = acc

    mine = lax.broadcasted_iota(jnp.int32, qt_ref.shape, 1) == b
    column = lambda ref: jnp.sum(jnp.where(mine, ref[...], 0.0), axis=1, keepdims=True)
    qcol, kcol, vcol = column(qt_ref), column(kt_ref), column(vt_ref)
    mine_o = lax.broadcasted_iota(jnp.int32, (HEAD_DIM, ot_ref.shape[1]), 1) == b
    head_rows = lambda g, h: slice(g * GROUP_WIDTH + h * HEAD_DIM, g * GROUP_WIDTH + (h + 1) * HEAD_DIM)

    def fold_lanes(t, op):
        out = t[:, 0:LANES]
        for c in range(1, t.shape[1] // LANES):
            out = op(out, t[:, c * LANES:(c + 1) * LANES])
        return out

    qk = qcol * kcol
    s0 = []
    for g in range(N_GROUPS):
        for h in range(HEADS_PER_GROUP):
            s_refs[g][h:h + 1, :] = jnp.sum(c_refs[g][0, h] * qcol[head_rows(g, h)], axis=0, keepdims=True)
        s0.append(jnp.concatenate(
            [jnp.sum(qk[head_rows(g, h)], axis=0, keepdims=True) for h in range(HEADS_PER_GROUP)], axis=0)
            + tabt_ref[g, :, 0:1])

    ss = [s_refs[g][...] + bias_refs[g][...] for g in range(N_GROUPS)]
    m_max = jnp.maximum(jnp.maximum(s0[0], s0[1]), s0[2])
    for s in ss:
        m_max = jnp.maximum(m_max, jnp.max(fold_lanes(s, jnp.maximum), axis=1, keepdims=True))
    p0 = [jnp.exp(s - m_max) for s in s0]
    den = p0[0] + p0[1] + p0[2]
    for g, s in enumerate(ss):
        p = jnp.exp(s - m_max)
        s_refs[g][...] = p
        den = den + jnp.sum(fold_lanes(p, jnp.add), axis=1, keepdims=True)
    inv_den = 1.0 / den

    for h in range(HEADS_PER_GROUP):
        part = None
        new = 0.0
        for g in range(N_GROUPS):
            f = fold_lanes(c_refs[g][1, h] * s_refs[g][h:h + 1, :], jnp.add)
            part = f if part is None else part + f
            new = new + p0[g][h:h + 1, :] * vcol[head_rows(g, h)]
        num = jnp.sum(part, axis=1, keepdims=True) + new
        hs = slice(h * HEAD_DIM, (h + 1) * HEAD_DIM)
        ot_ref[hs, :] = jnp.where(mine_o, num * inv_den[h:h + 1, :], ot_ref[hs, :])


def _dec_attn_call(qt, kt, vt, caches_t, layer, tabt):
    nb = qt.shape[1]
    bkts = []
    for window, dil in DILATED_GROUPS:
        w = jnp.arange(window)
        bkt = jnp.where(w % dil == 0, _t5_bucket(window - w), -1).astype(jnp.int32)
        bkts.append(jnp.broadcast_to(bkt[None, :], (HEADS_PER_GROUP, window)))
    cache_spec = lambda c: pl.BlockSpec((None, None) + c.shape[2:], lambda i: (layer, i, 0, 0, 0, 0))
    return pl.pallas_call(
        _dec_attn_kernel,
        grid=(nb,),
        in_specs=[_const_spec(qt.shape)] * 3 + [cache_spec(c) for c in caches_t]
        + [_const_spec(tabt.shape)] + [_const_spec(b.shape) for b in bkts],
        out_specs=pl.BlockSpec((GROUP_WIDTH, nb), lambda i: (0, 0)),
        out_shape=jax.ShapeDtypeStruct((GROUP_WIDTH, nb), F32),
        scratch_shapes=[pltpu.VMEM((HEADS_PER_GROUP, window), F32) for window, _ in DILATED_GROUPS] * 2,
        compiler_params=pltpu.CompilerParams(
            dimension_semantics=("arbitrary",), vmem_limit_bytes=VMEM_LIMIT_BYTES),
        name="decode_attn",
    )(qt, kt, vt, *caches_t, tabt, *bkts)


def _dec_back_kernel(x_ref, ylg_ref, gatt_ref, o_ref, wba_ref, wo_ref, nf_ref, wfi_ref, wfo_ref, y_ref):
    y_att = _dot(o_ref[...].astype(BF16), wba_ref[...])
    mix = ylg_ref[...] + gatt_ref[...] * y_att
    x1 = x_ref[...] + _dot(mix.astype(BF16), wo_ref[...])
    y_ref[...] = _ffn(x1, nf_ref[...], wfi_ref, wfo_ref)


def _dec_back_call(x, ylg, gatt, o, lw, layer):
    weights = [lw[n] for n in BACK_WEIGHTS]
    acts = (x, ylg, gatt, o)
    return pl.pallas_call(
        _dec_back_kernel,
        grid=(1,),
        in_specs=[_const_spec(a.shape) for a in acts] + [_layer_spec(w, layer) for w in weights],
        out_specs=pl.BlockSpec(x.shape, lambda i: (0, 0)),
        out_shape=jax.ShapeDtypeStruct(x.shape, F32),
        compiler_params=pltpu.CompilerParams(
            dimension_semantics=("arbitrary",), vmem_limit_bytes=VMEM_LIMIT_BYTES),
        name="decode_back",
    )(*acts, *weights)


def _prepare_weights(norm_mix, w_in, w_conv, b_conv, w_rg_a, b_rg_a, w_rg_x, b_rg_x, lru_lambda,
                     q_gain, k_gain, w_merge, b_merge, w_branch_lru, w_branch_att, w_o, norm_ffn,
                     w_ffn_in, w_ffn_out):
    depth = norm_mix.shape[0]
    row = lambda t: t.reshape(depth, 1, -1).astype(F32)
    per_tile = RG_TILE // LRU_BLOCK
    tiles = D_RNN // RG_TILE

    def block_diag(w):
        w = w.reshape(depth, tiles, per_tile, LRU_BLOCK, LRU_BLOCK)
        eye = jnp.eye(per_tile, dtype=w.dtype)
        return jnp.einsum("ltnij,nm->ltnimj", w, eye).reshape(depth, tiles, RG_TILE, RG_TILE)

    head = jnp.arange(MXU_DIM) // HEAD_DIM
    tile_gain = lambda t: jnp.tile(t.reshape(depth, 1, HEAD_DIM), (1, 1, HEADS_PER_GROUP)).astype(F32)
    return dict(
        norm_mix=row(norm_mix), w_in=w_in.astype(BF16), w_merge=w_merge.astype(BF16),
        b_merge=row(b_merge), w_conv=w_conv.astype(F32), b_conv=row(b_conv),
        w_rg=jnp.concatenate([block_diag(w_rg_a), block_diag(w_rg_x)], axis=3).astype(BF16),
        b_rg_a=row(b_rg_a), b_rg_x=row(b_rg_x), lam=row(lru_lambda),
        q_gain=tile_gain(q_gain), k_gain=tile_gain(k_gain),
        w_branch_lru=w_branch_lru.astype(BF16),
        pnorm=((head[:, None] == head[None, :]).astype(F32) / HEAD_DIM).astype(BF16),
        w_branch_att=w_branch_att.astype(BF16), w_o=w_o.astype(BF16), norm_ffn=row(norm_ffn),
        w_ffn_in=w_ffn_in.astype(BF16), w_ffn_out=w_ffn_out.astype(BF16))


def kernel(x_prompt, x_sample, cache_kv_g0, cache_kv_g1, cache_kv_g2, state_conv, state_h, rel_bias,
           norm_mix, w_in, w_conv, b_conv, w_rg_a, b_rg_a, w_rg_x, b_rg_x, lru_lambda, q_gain, k_gain,
           w_merge, b_merge, w_branch_lru, w_branch_att, w_o, norm_ffn, w_ffn_in, w_ffn_out):
    B, S, _ = x_prompt.shape
    nb = x_sample.shape[0]
    depth = norm_mix.shape[0]
    lw = _prepare_weights(norm_mix, w_in, w_conv, b_conv, w_rg_a, b_rg_a, w_rg_x, b_rg_x, lru_lambda,
                          q_gain, k_gain, w_merge, b_merge, w_branch_lru, w_branch_att, w_o, norm_ffn,
                          w_ffn_in, w_ffn_out)
    rel_bias = rel_bias.astype(F32)
    tabt = rel_bias.reshape(NUM_BUCKETS, N_GROUPS, HEADS_PER_GROUP).transpose(1, 2, 0)
    caches_t = [jnp.transpose(c.astype(F32), (0, 1, 3, 4, 5, 2))
                for c in (cache_kv_g0, cache_kv_g1, cache_kv_g2)]

    yp = x_prompt
    ys = x_sample.reshape(nb, D_MODEL)
    kvp = [[] for _ in range(N_GROUPS)]
    kvs = [[] for _ in range(N_GROUPS)]
    conv_p, h_p, conv_s, h_s = [], [], [], []
    for l in range(depth):
        (q0, q1, q2, k0, k1, k2, v0, v1, v2, ylg, gatt, kv0, kv1, kv2, ctail, hfin) = _front_call(
            yp, lw, l, TM_PROMPT)
        os_, lses = [], []
        for g, (qg, kg, vg) in enumerate(((q0, k0, v0), (q1, k1, v1), (q2, k2, v2))):
            tab = rel_bias[:, g * HEADS_PER_GROUP:(g + 1) * HEADS_PER_GROUP]
            o, lse = _attn_call(qg, kg, vg, tab, DILATED_GROUPS[g][1])
            os_.append(o)
            lses.append(lse)
        yp = _back_call(yp, ylg, gatt, os_, lses, lw, l, TM_PROMPT)
        for g, kv in enumerate((kv0, kv1, kv2)):
            kvp[g].append(kv.reshape(B, kv.shape[1], 2, HEADS_PER_GROUP, HEAD_DIM))
        conv_p.append(ctail[:, SUBLANES - (CONV_WIDTH - 1):, :])
        h_p.append(hfin[:, 0, :])

        sc = state_conv[l].astype(F32)
        conv_rows = [sc[:, r, :] for r in range(CONV_WIDTH - 1)]
        qs, ks, vs, ylg_s, gatt_s, lrux_s, hs_s = _dec_front_call(ys, conv_rows, state_h[l].astype(F32), lw, l)
        o_t = _dec_attn_call(qs.T, ks.T, vs.T, caches_t, l, tabt)
        ys = _dec_back_call(ys, ylg_s, gatt_s, o_t.T, lw, l)
        for g in range(N_GROUPS):
            sl = slice(g * GROUP_WIDTH, (g + 1) * GROUP_WIDTH)
            kvs[g].append(jnp.stack([ks[:, sl], vs[:, sl]], axis=1).reshape(
                nb, 1, 2, HEADS_PER_GROUP, HEAD_DIM))
        conv_s.append(jnp.stack(conv_rows[1:] + [lrux_s], axis=1))
        h_s.append(hs_s)

    stack = jnp.stack
    return (yp, ys.reshape(nb, 1, D_MODEL),
            stack(kvp[0]), stack(kvp[1]), stack(kvp[2]), stack(conv_p), stack(h_p),
            stack(kvs[0]), stack(kvs[1]), stack(kvs[2]), stack(conv_s), stack(h_s))
```

```python
import functools
import math

import jax
import jax.numpy as jnp
from jax import lax
from jax.experimental import pallas as pl
from jax.experimental.pallas import tpu as pltpu

D_MODEL = 1024
D_RNN = D_MODEL
N_LRU_BLOCKS = 16
LRU_BLOCK = D_RNN // N_LRU_BLOCKS
CONV_WIDTH = 4
LRU_C = 8.0
HEAD_DIM = 64
HEADS_PER_GROUP = 8
DILATED_GROUPS = ((128, 1), (512, 4), (2048, 16))
N_GROUPS = len(DILATED_GROUPS)
N_ATT_HEADS = N_GROUPS * HEADS_PER_GROUP
ATT_WIDTH = N_ATT_HEADS * HEAD_DIM
GROUP_WIDTH = HEADS_PER_GROUP * HEAD_DIM
BAND_BLOCK = 128
NUM_BUCKETS = 32
MAX_DISTANCE = 2048
D_FF = 2816
EPS = 1e-6

F32 = jnp.float32
BF16 = jnp.bfloat16

MXU_DIM = 256
SUBLANES = 8
LANES = 128
VMEM_LIMIT_BYTES = 56 * 1024 * 1024

TM_PROMPT = 256
ATTN_QBLOCKS = 4
RG_TILE = MXU_DIM
FFN_CHUNKS = ((0, 1536), (1536, 1280))


def _dot(a, b):
    return jnp.dot(a, b, preferred_element_type=F32)


def _dot_nt(a, b):
    return lax.dot_general(a, b, (((1,), (1,)), ((), ())), preferred_element_type=F32)


def _split_dot(x, m):
    hi = x.astype(BF16)
    lo = (x - hi.astype(F32)).astype(BF16)
    return _dot(hi, m) + _dot(lo, m)


def _sigmoid(x):
    return 0.5 * (jnp.tanh(0.5 * x) + 1.0)


def _gelu_tanh(x):
    c = math.sqrt(2.0 / math.pi)
    return 0.5 * x * (1.0 + jnp.tanh(c * (x + 0.044715 * (x * x * x))))


def _softplus(z):
    return jnp.maximum(z, 0.0) + jnp.log1p(jnp.exp(-jnp.abs(z)))


def _rms_rows(x, g):
    y = x * lax.rsqrt(jnp.mean(x * x, axis=-1, keepdims=True) + EPS)
    return y * g


def _head_rms(t, pn, gain):
    t2 = (t * t).astype(BF16)
    tiles = [_dot(t2[:, c * MXU_DIM:(c + 1) * MXU_DIM], pn) for c in range(t.shape[1] // MXU_DIM)]
    ms = tiles[0] if len(tiles) == 1 else jnp.concatenate(tiles, axis=1)
    return (t * lax.rsqrt(ms + EPS)) * gain


def _lru_gates(g, b_a, b_x, sp):
    w = g.shape[1] // 2
    r = _sigmoid(g[:, :w] + b_a)
    ig = _sigmoid(g[:, w:] + b_x)
    log_a = (-LRU_C * r) * sp
    a = jnp.exp(log_a)
    mult = jnp.sqrt(-jnp.tanh(log_a) * (a * a + 1.0))
    return a, mult, ig


def _scan_rows(a, u, h_in):
    rows, c = a.shape
    groups = rows // SUBLANES
    a3 = a.reshape(groups, SUBLANES, c)
    u3 = u.reshape(groups, SUBLANES, c)
    row = lax.broadcasted_iota(jnp.int32, (groups, SUBLANES, c), 1)
    shift = 1
    while shift < SUBLANES:
        ok = row >= shift
        a_sh = jnp.where(ok, pltpu.roll(a3, shift, axis=1), 1.0)
        u_sh = jnp.where(ok, pltpu.roll(u3, shift, axis=1), 0.0)
        u3 = u3 + a3 * u_sh
        a3 = a3 * a_sh
        shift *= 2
    out = []
    h = h_in
    for g in range(groups):
        hg = a3[g] * h + u3[g]
        out.append(hg)
        h = hg[SUBLANES - 1:SUBLANES, :]
    return jnp.concatenate(out, axis=0), h


def _to_slabs(slab_ref, t, first_slab=0):
    for c in range(t.shape[1] // LANES):
        slab_ref[first_slab + c] = t[:, c * LANES:(c + 1) * LANES]


def _from_slabs(slab_ref, first_slab=0, n_slabs=None):
    n_slabs = slab_ref.shape[0] - first_slab if n_slabs is None else n_slabs
    return jnp.concatenate([slab_ref[first_slab + c] for c in range(n_slabs)], axis=1)


def _gather_residues(slab_ref, dil, first_slab=0, n_slabs=None):
    n_slabs = slab_ref.shape[0] - first_slab if n_slabs is None else n_slabs
    per = slab_ref.shape[1] // dil
    cols = []
    for c in range(first_slab, first_slab + n_slabs):
        cols.append(jnp.concatenate(
            [slab_ref[c, pl.ds(r, per, stride=dil), :] for r in range(dil)], axis=0))
    return jnp.concatenate(cols, axis=1)


def _scatter_residues(slab_ref, t, dil, first_slab=0):
    per = slab_ref.shape[1] // dil
    for c in range(t.shape[1] // LANES):
        for r in range(dil):
            slab_ref[first_slab + c, pl.ds(r, per, stride=dil), :] = (
                t[r * per:(r + 1) * per, c * LANES:(c + 1) * LANES])


def _ffn(x1, xb, wfi_ref, wfo_ref):
    acc = x1
    for start, width in FFN_CHUNKS:
        gate = _dot(xb, wfi_ref[:, start:start + width])
        up = _dot(xb, wfi_ref[:, D_FF + start:D_FF + start + width])
        hid = (gate * _sigmoid(gate)) * up
        acc = acc + _dot(hid.astype(BF16), wfo_ref[start:start + width, :])
    return acc


def _layer_spec(w, layer):
    tail = (0,) * (w.ndim - 1)
    return pl.BlockSpec((None,) + w.shape[1:], lambda *_: (layer,) + tail, pipeline_mode=pl.Buffered(1))


def _const_spec(shape):
    zeros = (0,) * len(shape)
    return pl.BlockSpec(shape, lambda *_: zeros, pipeline_mode=pl.Buffered(1))


FRONT_WEIGHTS = ("norm_mix", "w_in", "w_merge", "b_merge", "w_conv", "b_conv", "w_rg", "b_rg_a", "b_rg_x",
                 "lam", "q_gain", "k_gain", "w_branch_lru")
BACK_WEIGHTS = ("w_branch_att", "w_o", "norm_ffn", "w_ffn_in", "w_ffn_out")


def _front_kernel(x_ref, nm_ref, win_ref, wmg_ref, bmg_ref, wconv_ref, bconv_ref, wrg_ref,
                  brga_ref, brgx_ref, lam_ref, qg_ref, kg_ref, wbl_ref, pn_ref,
                  q0_ref, q1_ref, q2_ref, k0_ref, k1_ref, k2_ref, v0_ref, v1_ref, v2_ref,
                  ylg_ref, gatt_ref, kvp0_ref, kvp1_ref, kvp2_ref, convp_ref, hp_ref,
                  xn_ref, kv_ref, ext_ref, h_ref, xb_ref, xg_ref, xc_ref, xcb_ref, gl_ref, y_ref, *, tm):
    i = pl.program_id(1)
    n_tiles = D_RNN // RG_TILE

    @pl.when(i == 0)
    def _():
        ext_ref[0:SUBLANES, :] = jnp.zeros((SUBLANES, D_RNN), F32)
        h_ref[...] = jnp.zeros((SUBLANES, D_RNN), F32)

    xn = _rms_rows(x_ref[...], nm_ref[...])
    _to_slabs(xn_ref, xn)
    xb_ref[...] = xn.astype(BF16)

    for t in range(n_tiles):
        sl = slice(t * RG_TILE, (t + 1) * RG_TILE)
        lru_x = _dot(xb_ref[...], win_ref[:, sl])
        ext_ref[SUBLANES:SUBLANES + tm, sl] = lru_x
        xc = bconv_ref[:, sl] + wconv_ref[0:1, sl] * ext_ref[SUBLANES - 3:SUBLANES - 3 + tm, sl]
        xc = xc + wconv_ref[1:2, sl] * ext_ref[SUBLANES - 2:SUBLANES - 2 + tm, sl]
        xc = xc + wconv_ref[2:3, sl] * ext_ref[SUBLANES - 1:SUBLANES - 1 + tm, sl]
        xc = xc + wconv_ref[3:4, sl] * lru_x
        xc_ref[:, sl] = xc
        xcb_ref[:, sl] = xc.astype(BF16)
        gl_ref[:, sl] = _gelu_tanh(_dot(xb_ref[...], win_ref[:, D_RNN + t * RG_TILE:D_RNN + (t + 1) * RG_TILE]))
    tail = ext_ref[tm:tm + SUBLANES, :]
    ext_ref[0:SUBLANES, :] = tail
    convp_ref[...] = tail

    sp = _softplus(-lam_ref[...])
    first_row = (lax.broadcasted_iota(jnp.int32, (tm, RG_TILE), 0) == 0) & (i == 0)

    def lru_tile(t):
        sl = slice(t * RG_TILE, (t + 1) * RG_TILE)
        g = _dot(xcb_ref[:, sl], wrg_ref[t])
        a, mult, ig = _lru_gates(g, brga_ref[:, sl], brgx_ref[:, sl], sp[:, sl])
        mult = jnp.where(first_row, 1.0, mult)
        u = (mult * ig) * xc_ref[:, sl]
        hs, h_last = _scan_rows(a, u, h_ref[0:1, sl])
        h_ref[:, sl] = jnp.broadcast_to(h_last, (SUBLANES, RG_TILE))
        y_ref[:, sl] = (hs * gl_ref[:, sl]).astype(BF16)

    pn = pn_ref[...]
    q_refs = (q0_ref, q1_ref, q2_ref)
    k_refs = (k0_ref, k1_ref, k2_ref)
    v_refs = (v0_ref, v1_ref, v2_ref)
    kvp_refs = (kvp0_ref, kvp1_ref, kvp2_ref)

    def attn_operands(g):
        window, dil = DILATED_GROUPS[g]
        per = tm // dil
        keep = min(window, tm)
        if dil == 1:
            src = xb_ref
        else:
            xg_ref[...] = _gather_residues(xn_ref, dil).astype(BF16)
            src = xg_ref
        for half in range(GROUP_WIDTH // MXU_DIM):
            c = 2 * D_RNN + g * GROUP_WIDTH + half * MXU_DIM
            hl = slice(half * MXU_DIM, (half + 1) * MXU_DIM)
            vl = slice(GROUP_WIDTH + half * MXU_DIM, GROUP_WIDTH + (half + 1) * MXU_DIM)
            q = _dot(src[...], win_ref[:, c:c + MXU_DIM])
            k = _dot(src[...], win_ref[:, c + ATT_WIDTH:c + ATT_WIDTH + MXU_DIM])
            v = _dot(src[...], win_ref[:, c + 2 * ATT_WIDTH:c + 2 * ATT_WIDTH + MXU_DIM])
            qn = (_head_rms(q, pn, qg_ref[:, hl]) * (HEAD_DIM ** -0.5)).astype(BF16)
            kn = _head_rms(k, pn, kg_ref[:, hl])
            knb = kn.astype(BF16)
            vb = v.astype(BF16)
            for r in range(dil):
                rows = slice(r * per, (r + 1) * per)
                q_refs[g][r, :, hl] = qn[rows]
                k_refs[g][r, :, hl] = knb[rows]
                v_refs[g][r, :, hl] = vb[rows]
            if dil == 1:
                kvp_refs[g][:, hl] = kn[tm - keep:, :]
                kvp_refs[g][:, vl] = v[tm - keep:, :]
            else:
                _scatter_residues(kv_ref, kn, dil, first_slab=hl.start // LANES)
                _scatter_residues(kv_ref, v, dil, first_slab=vl.start // LANES)
        if dil != 1:
            kvp_refs[g][...] = _from_slabs(kv_ref)

    for t in range(n_tiles):
        lru_tile(t)
        if t < N_GROUPS:
            attn_operands(t)
    hp_ref[...] = h_ref[...]
    for t in range(n_tiles):
        sl = slice(t * RG_TILE, (t + 1) * RG_TILE)
        al = slice(D_MODEL + t * RG_TILE, D_MODEL + (t + 1) * RG_TILE)
        gatt_ref[:, sl] = _sigmoid(_dot(xb_ref[...], wmg_ref[:, al]) + bmg_ref[:, al]).astype(gatt_ref.dtype)
        g_lru = _sigmoid(_dot(xb_ref[...], wmg_ref[:, sl]) + bmg_ref[:, sl])
        ylg_ref[:, sl] = (g_lru * _dot(y_ref[...], wbl_ref[:, sl])).astype(ylg_ref.dtype)


def _front_call(x, lw, layer, tm):
    B, S, _ = x.shape
    nt = S // tm
    row_spec = lambda width: pl.BlockSpec((None, tm, width), lambda b, i: (b, i, 0))
    weights = [lw[n] for n in FRONT_WEIGHTS]
    in_specs = ([row_spec(D_MODEL)] + [_layer_spec(w, layer) for w in weights]
                + [_const_spec(lw["pnorm"].shape)])

    out_shape, out_specs = [], []
    for _ in range(3):
        for _, dil in DILATED_GROUPS:
            out_shape.append(jax.ShapeDtypeStruct((B, dil, S // dil, GROUP_WIDTH), BF16))
            out_specs.append(pl.BlockSpec((None, dil, tm // dil, GROUP_WIDTH), lambda b, i: (b, 0, i, 0)))
    out_shape.append(jax.ShapeDtypeStruct((B, S, D_MODEL), BF16))
    out_specs.append(row_spec(D_MODEL))
    out_shape.append(jax.ShapeDtypeStruct((B, S, D_MODEL), BF16))
    out_specs.append(row_spec(D_MODEL))
    for window, _ in DILATED_GROUPS:
        keep = min(window, tm)
        first = nt - window // keep
        out_shape.append(jax.ShapeDtypeStruct((B, window, 2 * GROUP_WIDTH), F32))
        out_specs.append(pl.BlockSpec(
            (None, keep, 2 * GROUP_WIDTH),
            functools.partial(lambda b, i, first: (b, jnp.maximum(i - first, 0), 0), first=first)))
    for _ in range(2):
        out_shape.append(jax.ShapeDtypeStruct((B, SUBLANES, D_RNN), F32))
        out_specs.append(pl.BlockSpec((None, SUBLANES, D_RNN), lambda b, i: (b, 0, 0)))

    return pl.pallas_call(
        functools.partial(_front_kernel, tm=tm),
        grid=(B, nt),
        in_specs=in_specs,
        out_specs=out_specs,
        out_shape=out_shape,
        scratch_shapes=[pltpu.VMEM((D_MODEL // LANES, tm, LANES), F32),
                        pltpu.VMEM((2 * GROUP_WIDTH // LANES, tm, LANES), F32),
                        pltpu.VMEM((tm + SUBLANES, D_RNN), F32), pltpu.VMEM((SUBLANES, D_RNN), F32),
                        pltpu.VMEM((tm, D_MODEL), BF16), pltpu.VMEM((tm, D_MODEL), BF16),
                        pltpu.VMEM((tm, D_RNN), F32), pltpu.VMEM((tm, D_RNN), BF16),
                        pltpu.VMEM((tm, D_RNN), F32), pltpu.VMEM((tm, D_RNN), BF16)],
        compiler_params=pltpu.CompilerParams(
            dimension_semantics=("arbitrary", "arbitrary"), vmem_limit_bytes=VMEM_LIMIT_BYTES),
        name="prompt_front",
    )(x, *weights, lw["pnorm"])


def _attn_kernel(tab_ref, bkt_ref, q_ref, kp_ref, kc_ref, vp_ref, vc_ref, o_ref, lse_ref,
                 bias_ref, kbuf_ref, vbuf_ref, *, qblocks):
    b, r, j = pl.program_id(0), pl.program_id(1), pl.program_id(2)
    BB = BAND_BLOCK
    PAIR = 2 * HEAD_DIM

    @pl.when((b == 0) & (r == 0) & (j == 0))
    def _():
        bkt = bkt_ref[...]
        for h in range(HEADS_PER_GROUP):
            acc = jnp.full(bkt.shape, -jnp.inf, F32)
            for n in range(NUM_BUCKETS):
                acc = jnp.where(bkt == n, tab_ref[n, h], acc)
            bias_ref[h // 2, (h % 2) * BB:(h % 2 + 1) * BB, :] = acc

    kbuf_ref[0:BB, :] = kp_ref[...]
    kbuf_ref[BB:, :] = kc_ref[...]
    vbuf_ref[0:BB, :] = vp_ref[...]
    vbuf_ref[BB:, :] = vc_ref[...]

    first_col = lax.broadcasted_iota(jnp.int32, (1, 2 * BB), 1) < BB
    pen = jnp.where(first_col & (j == 0), -jnp.inf, 0.0).astype(F32)
    low_half = lax.broadcasted_iota(jnp.int32, (BB, PAIR), 1) < HEAD_DIM
    ones = jnp.ones((2 * BB, PAIR), BF16)
    for t in range(qblocks):
        rows = slice(t * BB, (t + 1) * BB)
        keys = slice(t * BB, (t + 2) * BB)
        for hp in range(HEADS_PER_GROUP // 2):
            sl = slice(hp * PAIR, (hp + 1) * PAIR)
            qp = q_ref[rows, sl]
            zero = jnp.zeros_like(qp)
            lhs = jnp.concatenate([jnp.where(low_half, qp, zero), jnp.where(low_half, zero, qp)], axis=0)
            s = _dot_nt(lhs, kbuf_ref[keys, sl]) + bias_ref[hp]
            if t == 0:
                s = s + pen
            m = jnp.max(s, axis=-1, keepdims=True)
            p = jnp.exp(s - m).astype(BF16)
            oe = _dot(p, jnp.concatenate([vbuf_ref[keys, sl], ones], axis=1))
            l = oe[:, PAIR:]
            o = oe[:, :PAIR] / l
            lse = m + jnp.log(l)
            for e in range(2):
                hl = slice(hp * PAIR + e * HEAD_DIM, hp * PAIR + (e + 1) * HEAD_DIM)
                o_ref[rows, hl] = o[e * BB:(e + 1) * BB, e * HEAD_DIM:(e + 1) * HEAD_DIM]
                lse_ref[rows, hl] = lse[e * BB:(e + 1) * BB, e * HEAD_DIM:(e + 1) * HEAD_DIM]


def _t5_bucket(dist):
    max_exact = NUM_BUCKETS // 2
    d_f = jnp.maximum(dist, 1).astype(F32)
    large = max_exact + (jnp.log(d_f / max_exact) / math.log(MAX_DISTANCE / max_exact)
                         * (NUM_BUCKETS - max_exact)).astype(jnp.int32)
    large = jnp.minimum(large, NUM_BUCKETS - 1)
    return jnp.where(dist < max_exact, dist, large)


def _band_buckets(dil):
    BB = BAND_BLOCK
    qi = jnp.arange(BB)[:, None]
    ki = jnp.arange(2 * BB)[None, :]
    sub = qi + BB - ki
    in_band = (sub >= 0) & (sub <= BB)
    return jnp.where(in_band, _t5_bucket(jnp.clip(sub, 0, BB) * dil), -1).astype(jnp.int32)


def _attn_call(q, k, v, tab, dil):
    B, _, n, C = q.shape
    BB = BAND_BLOCK
    qb = ATTN_QBLOCKS
    cur = pl.BlockSpec((None, None, qb * BB, C), lambda b, r, j: (b, r, j, 0))
    prev = pl.BlockSpec((None, None, BB, C), lambda b, r, j: (b, r, jnp.maximum(j * qb - 1, 0), 0))
    return pl.pallas_call(
        functools.partial(_attn_kernel, qblocks=qb),
        grid=(B, dil, n // (qb * BB)),
        in_specs=[pl.BlockSpec(memory_space=pltpu.SMEM),
                  pl.BlockSpec((BB, 2 * BB), lambda b, r, j: (0, 0)),
                  cur, prev, cur, prev, cur],
        out_specs=[cur, cur],
        out_shape=[jax.ShapeDtypeStruct(q.shape, F32)] * 2,
        scratch_shapes=[pltpu.VMEM((HEADS_PER_GROUP // 2, 2 * BB, 2 * BB), F32),
                        pltpu.VMEM(((qb + 1) * BB, C), BF16), pltpu.VMEM(((qb + 1) * BB, C), BF16)],
        compiler_params=pltpu.CompilerParams(
            dimension_semantics=("arbitrary", "arbitrary", "arbitrary")),
        name="prompt_attn_d%d" % dil,
    )(tab, _band_buckets(dil), q, k, k, v, v)


def _back_kernel(x_ref, ylg_ref, gatt_ref, o0_ref, o1_ref, o2_ref, l0_ref, l1_ref, l2_ref,
                 wba_ref, wo_ref, nf_ref, wfi_ref, wfo_ref,
                 qt_ref, kt_ref, vt_ref, c0_ref, c1_ref, c2_ref, tabt_ref, bkt0_ref, bkt1_ref, bkt2_ref,
                 y_ref, ot_ref,
                 so1_ref, so2_ref, sl1_ref, sl2_ref, bias0_ref, bias1_ref, bias2_ref, s0_ref, s1_ref, s2_ref,
                 *, tm, dec_parts):
    step = pl.program_id(0) * pl.num_programs(1) + pl.program_id(1)
    bias_refs = (bias0_ref, bias1_ref, bias2_ref)
    s_refs = (s0_ref, s1_ref, s2_ref)

    @pl.when(step == 0)
    def _():
        _decode_attention_init(tabt_ref, (bkt0_ref, bkt1_ref, bkt2_ref), bias_refs, s_refs, ot_ref)

    _decode_attention_step(step // dec_parts, step % dec_parts, HEADS_PER_GROUP // dec_parts,
                           qt_ref, kt_ref, vt_ref, (c0_ref, c1_ref, c2_ref), tabt_ref, bias_refs, s_refs, ot_ref)

    for (_, dil), src, dst in ((DILATED_GROUPS[1], o1_ref, so1_ref), (DILATED_GROUPS[2], o2_ref, so2_ref),
                               (DILATED_GROUPS[1], l1_ref, sl1_ref), (DILATED_GROUPS[2], l2_ref, sl2_ref)):
        _scatter_residues(dst, src[...].reshape(tm, GROUP_WIDTH), dil)
    os_ = (o0_ref[0], _from_slabs(so1_ref), _from_slabs(so2_ref))
    lses = (l0_ref[0], _from_slabs(sl1_ref), _from_slabs(sl2_ref))

    top = jnp.maximum(jnp.maximum(lses[0], lses[1]), lses[2])
    num = 0.0
    den = 0.0
    for o, lse in zip(os_, lses):
        z = jnp.exp(lse - top)
        num = num + z * o
        den = den + z
    o = num / den

    y_att = _dot(o.astype(BF16), wba_ref[...])
    mix = ylg_ref[...] + gatt_ref[...] * y_att
    x1 = x_ref[...] + _dot(mix.astype(BF16), wo_ref[...])
    y_ref[...] = _ffn(x1, _rms_rows(x1, nf_ref[...]).astype(BF16), wfi_ref, wfo_ref)


def _back_call(x, ylg, gatt, os_, lses, lw, layer, tm, qt, kt, vt, caches_t, rel_bias):
    B, S, _ = x.shape
    nt = S // tm
    nb = qt.shape[1]
    dec_parts = (B * nt) // nb
    assert dec_parts * nb == B * nt and HEADS_PER_GROUP % dec_parts == 0
    n_heads = HEADS_PER_GROUP // dec_parts
    tabt, bkts = _decode_attention_operands(rel_bias, dec_parts)

    row_spec = lambda width: pl.BlockSpec((None, tm, width), lambda b, i: (b, i, 0))
    res_specs = [pl.BlockSpec((None, dil, tm // dil, GROUP_WIDTH), lambda b, i: (b, 0, i, 0))
                 for _, dil in DILATED_GROUPS]
    cache_spec = lambda c: pl.BlockSpec(
        (None, None, 2, n_heads) + c.shape[4:],
        lambda b, i: (layer, (b * nt + i) // dec_parts, 0, (b * nt + i) % dec_parts, 0, 0))
    weights = [lw[n] for n in BACK_WEIGHTS]
    in_specs = ([row_spec(D_MODEL)] * 3 + res_specs * 2 + [_layer_spec(w, layer) for w in weights]
                + [_const_spec(qt.shape)] * 3 + [cache_spec(c) for c in caches_t]
                + [_const_spec(tabt.shape)] + [_const_spec(b.shape) for b in bkts])
    return pl.pallas_call(
        functools.partial(_back_kernel, tm=tm, dec_parts=dec_parts),
        grid=(B, nt),
        in_specs=in_specs,
        out_specs=[row_spec(D_MODEL), pl.BlockSpec((GROUP_WIDTH, nb), lambda b, i: (0, 0))],
        out_shape=[jax.ShapeDtypeStruct((B, S, D_MODEL), F32), jax.ShapeDtypeStruct((GROUP_WIDTH, nb), F32)],
        scratch_shapes=[pltpu.VMEM((GROUP_WIDTH // LANES, tm, LANES), F32)] * 4
        + [pltpu.VMEM((dec_parts, SUBLANES, window), F32) for window, _ in DILATED_GROUPS]
        + [pltpu.VMEM((SUBLANES, window), F32) for window, _ in DILATED_GROUPS],
        compiler_params=pltpu.CompilerParams(
            dimension_semantics=("arbitrary", "arbitrary"), vmem_limit_bytes=VMEM_LIMIT_BYTES),
        name="prompt_back",
    )(x, ylg, gatt, *os_, *lses, *weights, qt, kt, vt, *caches_t, tabt, *bkts)


def _dec_front_kernel(x_ref, c0_ref, c1_ref, c2_ref, h0_ref, nm_ref, win_ref, wmg_ref, bmg_ref,
                      wconv_ref, bconv_ref, wrg_ref, brga_ref, brgx_ref, lam_ref, qg_ref, kg_ref,
                      wbl_ref, pn_ref,
                      q_ref, k_ref, v_ref, ylg_ref, gatt_ref, lrux_ref, hs_ref):
    xb = _rms_rows(x_ref[...], nm_ref[...]).astype(BF16)
    lru_x = _dot(xb, win_ref[:, 0:D_RNN])
    lrux_ref[...] = lru_x
    xc = bconv_ref[...] + wconv_ref[0:1, :] * c0_ref[...]
    xc = xc + wconv_ref[1:2, :] * c1_ref[...]
    xc = xc + wconv_ref[2:3, :] * c2_ref[...]
    xc = xc + wconv_ref[3:4, :] * lru_x

    lru_g = _dot(xb, win_ref[:, D_RNN:2 * D_RNN])
    xcb = xc.astype(BF16)
    sp = _softplus(-lam_ref[...])
    ys = []
    for t in range(D_RNN // RG_TILE):
        sl = slice(t * RG_TILE, (t + 1) * RG_TILE)
        g = _dot(xcb[:, sl], wrg_ref[t])
        a, mult, ig = _lru_gates(g, brga_ref[:, sl], brgx_ref[:, sl], sp[:, sl])
        h = a * h0_ref[:, sl] + (mult * ig) * xc[:, sl]
        hs_ref[:, sl] = h
        ys.append((h * _gelu_tanh(lru_g[:, sl])).astype(BF16))
    y_lru = _dot(jnp.concatenate(ys, axis=1), wbl_ref[...])
    g_lru = _sigmoid(_dot(xb, wmg_ref[:, 0:D_MODEL]) + bmg_ref[:, 0:D_MODEL])
    ylg_ref[...] = g_lru * y_lru
    gatt_ref[...] = _sigmoid(_dot(xb, wmg_ref[:, D_MODEL:2 * D_MODEL]) + bmg_ref[:, D_MODEL:2 * D_MODEL])

    pn = pn_ref[...]
    for g in range(N_GROUPS):
        c = 2 * D_RNN + g * GROUP_WIDTH
        sl = slice(g * GROUP_WIDTH, (g + 1) * GROUP_WIDTH)
        q = _dot(xb, win_ref[:, c:c + GROUP_WIDTH])
        k = _dot(xb, win_ref[:, c + ATT_WIDTH:c + ATT_WIDTH + GROUP_WIDTH])
        v = _dot(xb, win_ref[:, c + 2 * ATT_WIDTH:c + 2 * ATT_WIDTH + GROUP_WIDTH])
        q_ref[:, sl] = _head_rms(q, pn, qg_ref[...]) * (HEAD_DIM ** -0.5)
        k_ref[:, sl] = _head_rms(k, pn, kg_ref[...])
        v_ref[:, sl] = v


def _dec_front_call(x, conv_rows, h0, lw, layer):
    nb = x.shape[0]
    weights = [lw[n] for n in FRONT_WEIGHTS]
    acts = (x,) + tuple(conv_rows) + (h0,)
    widths = (ATT_WIDTH, ATT_WIDTH, ATT_WIDTH, D_MODEL, D_MODEL, D_RNN, D_RNN)
    return pl.pallas_call(
        _dec_front_kernel,
        grid=(1,),
        in_specs=([_const_spec(a.shape) for a in acts] + [_layer_spec(w, layer) for w in weights]
                  + [_const_spec(lw["pnorm"].shape)]),
        out_specs=[pl.BlockSpec((nb, w), lambda i: (0, 0)) for w in widths],
        out_shape=[jax.ShapeDtypeStruct((nb, w), F32) for w in widths],
        compiler_params=pltpu.CompilerParams(
            dimension_semantics=("arbitrary",), vmem_limit_bytes=VMEM_LIMIT_BYTES),
        name="decode_front",
    )(*acts, *weights, lw["pnorm"])


def _decode_attention_init(tabt_ref, bkt_refs, bias_refs, s_refs, ot_ref):
    ot_ref[...] = jnp.zeros(ot_ref.shape, F32)
    for g in range(N_GROUPS):
        s_refs[g][...] = jnp.zeros(s_refs[g].shape, F32)
        bkt = bkt_refs[g][...]
        for part in range(bias_refs[g].shape[0]):
            acc = jnp.full(bkt.shape, -jnp.inf, F32)
            for n in range(NUM_BUCKETS):
                acc = jnp.where(bkt == n, tabt_ref[g, part, :, n:n + 1], acc)
            bias_refs[g][part] = acc


def _decode_attention_step(seq, part, n_heads, qt_ref, kt_ref, vt_ref, c_refs, tabt_ref, bias_refs, s_refs,
                           ot_ref):
    mine = lax.broadcasted_iota(jnp.int32, (HEAD_DIM, qt_ref.shape[1]), 1) == seq

    def column(ref, g, h):
        row0 = pl.multiple_of(g * GROUP_WIDTH + (part * n_heads + h) * HEAD_DIM, HEAD_DIM)
        return jnp.sum(jnp.where(mine, ref[pl.ds(row0, HEAD_DIM), :], 0.0), axis=1, keepdims=True)

    def fold_lanes(t, op):
        out = t[:, 0:LANES]
        for c in range(1, t.shape[1] // LANES):
            out = op(out, t[:, c * LANES:(c + 1) * LANES])
        return out

    pad = jnp.zeros((SUBLANES - n_heads, 1), F32)
    s0, vcols = [], []
    for g in range(N_GROUPS):
        rows = []
        for h in range(n_heads):
            qh = column(qt_ref, g, h)
            s_refs[g][h:h + 1, :] = jnp.sum(c_refs[g][0, h] * qh, axis=0, keepdims=True)
            rows.append(jnp.sum(qh * column(kt_ref, g, h), axis=0, keepdims=True))
            vcols.append(column(vt_ref, g, h))
        s0.append(jnp.concatenate(rows + [pad], axis=0) + tabt_ref[g, part, :, 0:1])

    ss = [s_refs[g][...] + bias_refs[g][part] for g in range(N_GROUPS)]
    m_max = jnp.maximum(jnp.maximum(s0[0], s0[1]), s0[2])
    for s in ss:
        m_max = jnp.maximum(m_max, jnp.max(fold_lanes(s, jnp.maximum), axis=1, keepdims=True))
    p0 = [jnp.exp(s - m_max) for s in s0]
    den = p0[0] + p0[1] + p0[2]
    for g, s in enumerate(ss):
        p = jnp.exp(s - m_max)
        s_refs[g][...] = p
        den = den + jnp.sum(fold_lanes(p, jnp.add), axis=1, keepdims=True)
    inv_den = 1.0 / den

    for h in range(n_heads):
        acc = None
        new = 0.0
        for g in range(N_GROUPS):
            f = fold_lanes(c_refs[g][1, h] * s_refs[g][h:h + 1, :], jnp.add)
            acc = f if acc is None else acc + f
            new = new + p0[g][h:h + 1, :] * vcols[g * n_heads + h]
        num = jnp.sum(acc, axis=1, keepdims=True) + new
        row0 = pl.multiple_of((part * n_heads + h) * HEAD_DIM, HEAD_DIM)
        ot_ref[pl.ds(row0, HEAD_DIM), :] = jnp.where(mine, num * inv_den[h:h + 1, :],
                                                      ot_ref[pl.ds(row0, HEAD_DIM), :])


def _decode_attention_operands(rel_bias, parts):
    n_heads = HEADS_PER_GROUP // parts
    tab = rel_bias.reshape(NUM_BUCKETS, N_GROUPS, parts, n_heads).transpose(1, 2, 3, 0)
    tab = jnp.pad(tab, ((0, 0), (0, 0), (0, SUBLANES - n_heads), (0, 0)))
    bkts = []
    for window, dil in DILATED_GROUPS:
        w = jnp.arange(window)
        bkt = jnp.where(w % dil == 0, _t5_bucket(window - w), -1).astype(jnp.int32)
        bkts.append(jnp.broadcast_to(bkt[None, :], (SUBLANES, window)))
    return tab, bkts


def _dec_back_kernel(x_ref, ylg_ref, gatt_ref, o_ref, wba_ref, wo_ref, nf_ref, wfi_ref, wfo_ref, y_ref):
    y_att = _dot(o_ref[...].astype(BF16), wba_ref[...])
    mix = ylg_ref[...] + gatt_ref[...] * y_att
    x1 = x_ref[...] + _dot(mix.astype(BF16), wo_ref[...])
    y_ref[...] = _ffn(x1, _rms_rows(x1, nf_ref[...]).astype(BF16), wfi_ref, wfo_ref)


def _dec_back_call(x, ylg, gatt, o, lw, layer):
    weights = [lw[n] for n in BACK_WEIGHTS]
    acts = (x, ylg, gatt, o)
    return pl.pallas_call(
        _dec_back_kernel,
        grid=(1,),
        in_specs=[_const_spec(a.shape) for a in acts] + [_layer_spec(w, layer) for w in weights],
        out_specs=pl.BlockSpec(x.shape, lambda i: (0, 0)),
        out_shape=jax.ShapeDtypeStruct(x.shape, F32),
        compiler_params=pltpu.CompilerParams(
            dimension_semantics=("arbitrary",), vmem_limit_bytes=VMEM_LIMIT_BYTES),
        name="decode_back",
    )(*acts, *weights)


def _prepare_weights(norm_mix, w_in, w_conv, b_conv, w_rg_a, b_rg_a, w_rg_x, b_rg_x, lru_lambda,
                     q_gain, k_gain, w_merge, b_merge, w_branch_lru, w_branch_att, w_o, norm_ffn,
                     w_ffn_in, w_ffn_out):
    depth = norm_mix.shape[0]
    row = lambda t: t.reshape(depth, 1, -1).astype(F32)
    per_tile = RG_TILE // LRU_BLOCK
    tiles = D_RNN // RG_TILE

    def block_diag(w):
        w = w.reshape(depth, tiles, per_tile, LRU_BLOCK, LRU_BLOCK)
        eye = jnp.eye(per_tile, dtype=w.dtype)
        return jnp.einsum("ltnij,nm->ltnimj", w, eye).reshape(depth, tiles, RG_TILE, RG_TILE)

    head = jnp.arange(MXU_DIM) // HEAD_DIM
    tile_gain = lambda t: jnp.tile(t.reshape(depth, 1, HEAD_DIM), (1, 1, HEADS_PER_GROUP)).astype(F32)
    return dict(
        norm_mix=row(norm_mix), w_in=w_in.astype(BF16), w_merge=w_merge.astype(BF16),
        b_merge=row(b_merge), w_conv=w_conv.astype(F32), b_conv=row(b_conv),
        w_rg=jnp.concatenate([block_diag(w_rg_a), block_diag(w_rg_x)], axis=3).astype(BF16),
        b_rg_a=row(b_rg_a), b_rg_x=row(b_rg_x), lam=row(lru_lambda),
        q_gain=tile_gain(q_gain), k_gain=tile_gain(k_gain),
        w_branch_lru=w_branch_lru.astype(BF16),
        pnorm=((head[:, None] == head[None, :]).astype(F32) / HEAD_DIM).astype(BF16),
        w_branch_att=w_branch_att.astype(BF16), w_o=w_o.astype(BF16), norm_ffn=row(norm_ffn),
        w_ffn_in=w_ffn_in.astype(BF16), w_ffn_out=w_ffn_out.astype(BF16))


def kernel(x_prompt, x_sample, cache_kv_g0, cache_kv_g1, cache_kv_g2, state_conv, state_h, rel_bias,
           norm_mix, w_in, w_conv, b_conv, w_rg_a, b_rg_a, w_rg_x, b_rg_x, lru_lambda, q_gain, k_gain,
           w_merge, b_merge, w_branch_lru, w_branch_att, w_o, norm_ffn, w_ffn_in, w_ffn_out):
    B, S, _ = x_prompt.shape
    nb = x_sample.shape[0]
    depth = norm_mix.shape[0]
    lw = _prepare_weights(norm_mix, w_in, w_conv, b_conv, w_rg_a, b_rg_a, w_rg_x, b_rg_x, lru_lambda,
                          q_gain, k_gain, w_merge, b_merge, w_branch_lru, w_branch_att, w_o, norm_ffn,
                          w_ffn_in, w_ffn_out)
    rel_bias = rel_bias.astype(F32)
    caches_t = [jnp.transpose(c.astype(F32), (0, 1, 3, 4, 5, 2))
                for c in (cache_kv_g0, cache_kv_g1, cache_kv_g2)]

    yp = x_prompt
    ys = x_sample.reshape(nb, D_MODEL)
    kvp = [[] for _ in range(N_GROUPS)]
    kvs = [[] for _ in range(N_GROUPS)]
    conv_p, h_p, conv_s, h_s = [], [], [], []
    for l in range(depth):
        sc = state_conv[l].astype(F32)
        conv_rows = [sc[:, r, :] for r in range(CONV_WIDTH - 1)]
        qs, ks, vs, ylg_s, gatt_s, lrux_s, hs_s = _dec_front_call(ys, conv_rows, state_h[l].astype(F32), lw, l)

        (q0, q1, q2, k0, k1, k2, v0, v1, v2, ylg, gatt, kv0, kv1, kv2, ctail, hfin) = _front_call(
            yp, lw, l, TM_PROMPT)
        os_, lses = [], []
        for g, (qg, kg, vg) in enumerate(((q0, k0, v0), (q1, k1, v1), (q2, k2, v2))):
            tab = rel_bias[:, g * HEADS_PER_GROUP:(g + 1) * HEADS_PER_GROUP]
            o, lse = _attn_call(qg, kg, vg, tab, DILATED_GROUPS[g][1])
            os_.append(o)
            lses.append(lse)
        yp, o_t = _back_call(yp, ylg, gatt, os_, lses, lw, l, TM_PROMPT, qs.T, ks.T, vs.T, caches_t, rel_bias)
        for g, kv in enumerate((kv0, kv1, kv2)):
            kvp[g].append(kv.reshape(B, kv.shape[1], 2, HEADS_PER_GROUP, HEAD_DIM))
        conv_p.append(ctail[:, SUBLANES - (CONV_WIDTH - 1):, :])
        h_p.append(hfin[:, 0, :])

        ys = _dec_back_call(ys, ylg_s, gatt_s, o_t.T, lw, l)
        for g in range(N_GROUPS):
            sl = slice(g * GROUP_WIDTH, (g + 1) * GROUP_WIDTH)
            kvs[g].append(jnp.stack([ks[:, sl], vs[:, sl]], axis=1).reshape(
                nb, 1, 2, HEADS_PER_GROUP, HEAD_DIM))
        conv_s.append(jnp.stack(conv_rows[1:] + [lrux_s], axis=1))
        h_s.append(hs_s)

    stack = jnp.stack
    return (yp, ys.reshape(nb, 1, D_MODEL),
            stack(kvp[0]), stack(kvp[1]), stack(kvp[2]), stack(conv_p), stack(h_p),
            stack(kvs[0]), stack(kvs[1]), stack(kvs[2]), stack(conv_s), stack(h_s))
```

```python
import functools
import math

import numpy as np
import jax
import jax.numpy as jnp
from jax import lax
from jax.experimental import pallas as pl
from jax.experimental.pallas import tpu as pltpu

D_MODEL = 1024
D_RNN = D_MODEL
N_LRU_BLOCKS = 16
LRU_BLOCK = D_RNN // N_LRU_BLOCKS
CONV_WIDTH = 4
LRU_C = 8.0
HEAD_DIM = 64
HEADS_PER_GROUP = 8
DILATED_GROUPS = ((128, 1), (512, 4), (2048, 16))
N_GROUPS = len(DILATED_GROUPS)
N_ATT_HEADS = N_GROUPS * HEADS_PER_GROUP
ATT_WIDTH = N_ATT_HEADS * HEAD_DIM
GROUP_WIDTH = HEADS_PER_GROUP * HEAD_DIM
BAND_BLOCK = 128
NUM_BUCKETS = 32
MAX_DISTANCE = 2048
D_FF = 2816
EPS = 1e-6

F32 = jnp.float32
BF16 = jnp.bfloat16

MXU_DIM = 256
SUBLANES = 8
LANES = 128
VMEM_LIMIT_BYTES = 56 * 1024 * 1024

TM_PROMPT = 256
ATTN_QBLOCKS = 8
RG_TILE = MXU_DIM
FFN_CHUNKS = ((0, 1536), (1536, 1280))


def _dot(a, b):
    return jnp.dot(a, b, preferred_element_type=F32)


def _dot_nt(a, b):
    return lax.dot_general(a, b, (((1,), (1,)), ((), ())), preferred_element_type=F32)


def _sigmoid(x):
    return 0.5 * (jnp.tanh(0.5 * x) + 1.0)


def _gelu_tanh(x):
    c = math.sqrt(2.0 / math.pi)
    return 0.5 * x * (1.0 + jnp.tanh(c * (x + 0.044715 * (x * x * x))))


def _softplus(z):
    return jnp.maximum(z, 0.0) + jnp.log1p(jnp.exp(-jnp.abs(z)))


def _rms_rows(x, g):
    y = x * lax.rsqrt(jnp.mean(x * x, axis=-1, keepdims=True) + EPS)
    return y * g


def _head_rms(t, pn, gain):
    t2 = (t * t).astype(BF16)
    tiles = [_dot(t2[:, c * MXU_DIM:(c + 1) * MXU_DIM], pn) for c in range(t.shape[1] // MXU_DIM)]
    ms = tiles[0] if len(tiles) == 1 else jnp.concatenate(tiles, axis=1)
    return (t * lax.rsqrt(ms + EPS)) * gain


def _lru_gates(g, b_a, b_x, sp):
    w = g.shape[1] // 2
    r = _sigmoid(g[:, :w] + b_a)
    ig = _sigmoid(g[:, w:] + b_x)
    log_a = (-LRU_C * r) * sp
    a = jnp.exp(log_a)
    mult = jnp.sqrt(-jnp.tanh(log_a) * (a * a + 1.0))
    return a, mult, ig


def _scan_rows(a, u, h_in):
    rows, c = a.shape
    groups = rows // SUBLANES
    a3 = a.reshape(groups, SUBLANES, c)
    u3 = u.reshape(groups, SUBLANES, c)
    row = lax.broadcasted_iota(jnp.int32, (groups, SUBLANES, c), 1)
    shift = 1
    while shift < SUBLANES:
        ok = row >= shift
        a_sh = jnp.where(ok, pltpu.roll(a3, shift, axis=1), 1.0)
        u_sh = jnp.where(ok, pltpu.roll(u3, shift, axis=1), 0.0)
        u3 = u3 + a3 * u_sh
        a3 = a3 * a_sh
        shift *= 2
    out = []
    h = h_in
    for g in range(groups):
        hg = a3[g] * h + u3[g]
        out.append(hg)
        h = hg[SUBLANES - 1:SUBLANES, :]
    return jnp.concatenate(out, axis=0), h


def _to_slabs(slab_ref, t, first_slab=0):
    for c in range(t.shape[1] // LANES):
        slab_ref[first_slab + c] = t[:, c * LANES:(c + 1) * LANES]


def _from_slabs(slab_ref, first_slab=0, n_slabs=None):
    n_slabs = slab_ref.shape[0] - first_slab if n_slabs is None else n_slabs
    return jnp.concatenate([slab_ref[first_slab + c] for c in range(n_slabs)], axis=1)


def _gather_residues(slab_ref, dil, first_slab=0, n_slabs=None):
    n_slabs = slab_ref.shape[0] - first_slab if n_slabs is None else n_slabs
    per = slab_ref.shape[1] // dil
    cols = []
    for c in range(first_slab, first_slab + n_slabs):
        cols.append(jnp.concatenate(
            [slab_ref[c, pl.ds(r, per, stride=dil), :] for r in range(dil)], axis=0))
    return jnp.concatenate(cols, axis=1)


def _scatter_residues(slab_ref, t, dil, first_slab=0):
    per = slab_ref.shape[1] // dil
    for c in range(t.shape[1] // LANES):
        for r in range(dil):
            slab_ref[first_slab + c, pl.ds(r, per, stride=dil), :] = (
                t[r * per:(r + 1) * per, c * LANES:(c + 1) * LANES])


def _ffn(x1, xb, wfi_ref, wfo_ref):
    acc = x1
    for start, width in FFN_CHUNKS:
        gate = _dot(xb, wfi_ref[:, start:start + width])
        up = _dot(xb, wfi_ref[:, D_FF + start:D_FF + start + width])
        hid = (gate * _sigmoid(gate)) * up
        acc = acc + _dot(hid.astype(BF16), wfo_ref[start:start + width, :])
    return acc


def _layer_spec(w, layer):
    tail = (0,) * (w.ndim - 1)
    return pl.BlockSpec((None,) + w.shape[1:], lambda *_: (layer,) + tail, pipeline_mode=pl.Buffered(1))


def _const_spec(shape):
    zeros = (0,) * len(shape)
    return pl.BlockSpec(shape, lambda *_: zeros, pipeline_mode=pl.Buffered(1))


FRONT_WEIGHTS = ("norm_mix", "w_in", "w_merge", "b_merge", "w_conv", "b_conv", "w_rg", "b_rg_a", "b_rg_x",
                 "lam", "q_gain", "k_gain", "w_branch_lru")
BACK_WEIGHTS = ("w_branch_att", "w_o", "norm_ffn", "w_ffn_in", "w_ffn_out")


def _front_kernel(x_ref, nm_ref, win_ref, wmg_ref, bmg_ref, wconv_ref, bconv_ref, wrg_ref,
                  brga_ref, brgx_ref, lam_ref, qg_ref, kg_ref, wbl_ref, pn_ref,
                  q0_ref, q1_ref, q2_ref, k0_ref, k1_ref, k2_ref, v0_ref, v1_ref, v2_ref,
                  ylg_ref, gatt_ref, kvp0_ref, kvp1_ref, kvp2_ref, convp_ref, hp_ref,
                  xn_ref, kv_ref, ext_ref, h_ref, xb_ref, xg_ref, xc_ref, xcb_ref, gl_ref, y_ref, *, tm):
    i = pl.program_id(1)
    n_tiles = D_RNN // RG_TILE

    @pl.when(i == 0)
    def _():
        ext_ref[0:SUBLANES, :] = jnp.zeros((SUBLANES, D_RNN), F32)
        h_ref[...] = jnp.zeros((SUBLANES, D_RNN), F32)

    xn = _rms_rows(x_ref[...], nm_ref[...])
    _to_slabs(xn_ref, xn)
    xb_ref[...] = xn.astype(BF16)

    for t in range(n_tiles):
        sl = slice(t * RG_TILE, (t + 1) * RG_TILE)
        lru_x = _dot(xb_ref[...], win_ref[:, sl])
        ext_ref[SUBLANES:SUBLANES + tm, sl] = lru_x
        xc = bconv_ref[:, sl] + wconv_ref[0:1, sl] * ext_ref[SUBLANES - 3:SUBLANES - 3 + tm, sl]
        xc = xc + wconv_ref[1:2, sl] * ext_ref[SUBLANES - 2:SUBLANES - 2 + tm, sl]
        xc = xc + wconv_ref[2:3, sl] * ext_ref[SUBLANES - 1:SUBLANES - 1 + tm, sl]
        xc = xc + wconv_ref[3:4, sl] * lru_x
        xc_ref[:, sl] = xc
        xcb_ref[:, sl] = xc.astype(BF16)
        gl_ref[:, sl] = _gelu_tanh(_dot(xb_ref[...], win_ref[:, D_RNN + t * RG_TILE:D_RNN + (t + 1) * RG_TILE]))
    tail = ext_ref[tm:tm + SUBLANES, :]
    ext_ref[0:SUBLANES, :] = tail
    convp_ref[...] = tail

    sp = _softplus(-lam_ref[...])
    first_row = (lax.broadcasted_iota(jnp.int32, (tm, RG_TILE), 0) == 0) & (i == 0)
    gates = {}

    def lru_gates(t):
        sl = slice(t * RG_TILE, (t + 1) * RG_TILE)
        g = _dot(xcb_ref[:, sl], wrg_ref[t])
        a, mult, ig = _lru_gates(g, brga_ref[:, sl], brgx_ref[:, sl], sp[:, sl])
        mult = jnp.where(first_row, 1.0, mult)
        gates[t] = (a, (mult * ig) * xc_ref[:, sl])

    def lru_scan(t):
        sl = slice(t * RG_TILE, (t + 1) * RG_TILE)
        a, u = gates.pop(t)
        hs, h_last = _scan_rows(a, u, h_ref[0:1, sl])
        h_ref[:, sl] = jnp.broadcast_to(h_last, (SUBLANES, RG_TILE))
        y_ref[:, sl] = (hs * gl_ref[:, sl]).astype(BF16)

    pn = pn_ref[...]
    q_refs = (q0_ref, q1_ref, q2_ref)
    k_refs = (k0_ref, k1_ref, k2_ref)
    v_refs = (v0_ref, v1_ref, v2_ref)
    kvp_refs = (kvp0_ref, kvp1_ref, kvp2_ref)

    def attn_operands(g, half):
        window, dil = DILATED_GROUPS[g]
        per = tm // dil
        keep = min(window, tm)
        if dil == 1:
            src = xb_ref
        else:
            if half == 0:
                xg_ref[...] = _gather_residues(xn_ref, dil).astype(BF16)
            src = xg_ref
        c = 2 * D_RNN + g * GROUP_WIDTH + half * MXU_DIM
        hl = slice(half * MXU_DIM, (half + 1) * MXU_DIM)
        vl = slice(GROUP_WIDTH + half * MXU_DIM, GROUP_WIDTH + (half + 1) * MXU_DIM)
        q = _dot(src[...], win_ref[:, c:c + MXU_DIM])
        k = _dot(src[...], win_ref[:, c + ATT_WIDTH:c + ATT_WIDTH + MXU_DIM])
        v = _dot(src[...], win_ref[:, c + 2 * ATT_WIDTH:c + 2 * ATT_WIDTH + MXU_DIM])
        qn = (_head_rms(q, pn, qg_ref[:, hl]) * (HEAD_DIM ** -0.5)).astype(BF16)
        kn = _head_rms(k, pn, kg_ref[:, hl])
        knb = kn.astype(BF16)
        vb = v.astype(BF16)
        for r in range(dil):
            rows = slice(r * per, (r + 1) * per)
            q_refs[g][r, :, hl] = qn[rows]
            k_refs[g][r, :, hl] = knb[rows]
            v_refs[g][r, :, hl] = vb[rows]
        if dil == 1:
            kvp_refs[g][:, hl] = kn[tm - keep:, :]
            kvp_refs[g][:, vl] = v[tm - keep:, :]
        else:
            _scatter_residues(kv_ref, kn, dil, first_slab=hl.start // LANES)
            _scatter_residues(kv_ref, v, dil, first_slab=vl.start // LANES)
            if half == GROUP_WIDTH // MXU_DIM - 1:
                kvp_refs[g][...] = _from_slabs(kv_ref)

    def att_gate(t):
        sl = slice(t * RG_TILE, (t + 1) * RG_TILE)
        al = slice(D_MODEL + t * RG_TILE, D_MODEL + (t + 1) * RG_TILE)
        gatt_ref[:, sl] = _sigmoid(_dot(xb_ref[...], wmg_ref[:, al]) + bmg_ref[:, al]).astype(gatt_ref.dtype)

    for t in range(n_tiles):
        lru_gates(t)
        if t < N_GROUPS:
            attn_operands(t, 0)
        else:
            att_gate(0)
            att_gate(1)
        lru_scan(t)
        if t < N_GROUPS:
            attn_operands(t, 1)
        else:
            att_gate(2)
            att_gate(3)
    hp_ref[...] = h_ref[...]
    for t in range(n_tiles):
        sl = slice(t * RG_TILE, (t + 1) * RG_TILE)
        g_lru = _sigmoid(_dot(xb_ref[...], wmg_ref[:, sl]) + bmg_ref[:, sl])
        ylg_ref[:, sl] = (g_lru * _dot(y_ref[...], wbl_ref[:, sl])).astype(ylg_ref.dtype)


def _front_call(x, lw, layer, tm):
    B, S, _ = x.shape
    nt = S // tm
    row_spec = lambda width: pl.BlockSpec((None, tm, width), lambda b, i: (b, i, 0))
    weights = [lw[n] for n in FRONT_WEIGHTS]
    in_specs = ([row_spec(D_MODEL)] + [_layer_spec(w, layer) for w in weights]
                + [_const_spec(lw["pnorm"].shape)])

    out_shape, out_specs = [], []
    for _ in range(3):
        for _, dil in DILATED_GROUPS:
            out_shape.append(jax.ShapeDtypeStruct((B, dil, S // dil, GROUP_WIDTH), BF16))
            out_specs.append(pl.BlockSpec((None, dil, tm // dil, GROUP_WIDTH), lambda b, i: (b, 0, i, 0)))
    out_shape.append(jax.ShapeDtypeStruct((B, S, D_MODEL), BF16))
    out_specs.append(row_spec(D_MODEL))
    out_shape.append(jax.ShapeDtypeStruct((B, S, D_MODEL), BF16))
    out_specs.append(row_spec(D_MODEL))
    for window, _ in DILATED_GROUPS:
        keep = min(window, tm)
        first = nt - window // keep
        out_shape.append(jax.ShapeDtypeStruct((B, window, 2 * GROUP_WIDTH), F32))
        out_specs.append(pl.BlockSpec(
            (None, keep, 2 * GROUP_WIDTH),
            functools.partial(lambda b, i, first: (b, jnp.maximum(i - first, 0), 0), first=first)))
    for _ in range(2):
        out_shape.append(jax.ShapeDtypeStruct((B, SUBLANES, D_RNN), F32))
        out_specs.append(pl.BlockSpec((None, SUBLANES, D_RNN), lambda b, i: (b, 0, 0)))

    return pl.pallas_call(
        functools.partial(_front_kernel, tm=tm),
        grid=(B, nt),
        in_specs=in_specs,
        out_specs=out_specs,
        out_shape=out_shape,
        scratch_shapes=[pltpu.VMEM((D_MODEL // LANES, tm, LANES), F32),
                        pltpu.VMEM((2 * GROUP_WIDTH // LANES, tm, LANES), F32),
                        pltpu.VMEM((tm + SUBLANES, D_RNN), F32), pltpu.VMEM((SUBLANES, D_RNN), F32),
                        pltpu.VMEM((tm, D_MODEL), BF16), pltpu.VMEM((tm, D_MODEL), BF16),
                        pltpu.VMEM((tm, D_RNN), F32), pltpu.VMEM((tm, D_RNN), BF16),
                        pltpu.VMEM((tm, D_RNN), F32), pltpu.VMEM((tm, D_RNN), BF16)],
        compiler_params=pltpu.CompilerParams(
            dimension_semantics=("arbitrary", "arbitrary"), vmem_limit_bytes=VMEM_LIMIT_BYTES),
        name="prompt_front",
    )(x, *weights, lw["pnorm"])


def _attn_kernel(tab_ref, bkt_ref, q_ref, kp_ref, kc_ref, vp_ref, vc_ref, o_ref, lse_ref,
                 bias_ref, kbuf_ref, vbuf_ref, *, qblocks):
    b, r, j = pl.program_id(0), pl.program_id(1), pl.program_id(2)
    BB = BAND_BLOCK
    PAIR = 2 * HEAD_DIM

    @pl.when((b == 0) & (r == 0) & (j == 0))
    def _():
        bkt = bkt_ref[...]
        for h in range(HEADS_PER_GROUP):
            acc = jnp.full(bkt.shape, -jnp.inf, F32)
            for n in range(NUM_BUCKETS):
                acc = jnp.where(bkt == n, tab_ref[n, h], acc)
            bias_ref[h // 2, (h % 2) * BB:(h % 2 + 1) * BB, :] = acc

    kbuf_ref[0:BB, :] = kp_ref[...]
    kbuf_ref[BB:, :] = kc_ref[...]
    vbuf_ref[0:BB, :] = vp_ref[...]
    vbuf_ref[BB:, :] = vc_ref[...]

    first_col = lax.broadcasted_iota(jnp.int32, (1, 2 * BB), 1) < BB
    pen = jnp.where(first_col & (j == 0), -jnp.inf, 0.0).astype(F32)
    low_half = lax.broadcasted_iota(jnp.int32, (BB, PAIR), 1) < HEAD_DIM
    ones = jnp.ones((2 * BB, PAIR), BF16)
    for t in range(qblocks):
        rows = slice(t * BB, (t + 1) * BB)
        keys = slice(t * BB, (t + 2) * BB)
        for hp in range(HEADS_PER_GROUP // 2):
            sl = slice(hp * PAIR, (hp + 1) * PAIR)
            qp = q_ref[rows, sl]
            zero = jnp.zeros_like(qp)
            lhs = jnp.concatenate([jnp.where(low_half, qp, zero), jnp.where(low_half, zero, qp)], axis=0)
            s = _dot_nt(lhs, kbuf_ref[keys, sl]) + bias_ref[hp]
            if t == 0:
                s = s + pen
            m = jnp.max(s, axis=-1, keepdims=True)
            p = jnp.exp((s - m).astype(BF16))
            oe = _dot(p, jnp.concatenate([vbuf_ref[keys, sl], ones], axis=1))
            l = oe[:, PAIR:]
            o = oe[:, :PAIR] / l
            lse = m + jnp.log(l)
            for e in range(2):
                hl = slice(hp * PAIR + e * HEAD_DIM, hp * PAIR + (e + 1) * HEAD_DIM)
                o_ref[rows, hl] = o[e * BB:(e + 1) * BB, e * HEAD_DIM:(e + 1) * HEAD_DIM]
                lse_ref[rows, hl] = lse[e * BB:(e + 1) * BB, e * HEAD_DIM:(e + 1) * HEAD_DIM]


def _t5_bucket(dist):
    max_exact = NUM_BUCKETS // 2
    d_f = np.maximum(dist, 1).astype(np.float32)
    large = max_exact + (np.log(d_f / np.float32(max_exact)) / np.float32(math.log(MAX_DISTANCE / max_exact))
                         * np.float32(NUM_BUCKETS - max_exact)).astype(np.int32)
    large = np.minimum(large, NUM_BUCKETS - 1)
    return np.where(dist < max_exact, dist, large).astype(np.int32)


def _band_buckets(dil):
    BB = BAND_BLOCK
    qi = np.arange(BB)[:, None]
    ki = np.arange(2 * BB)[None, :]
    sub = qi + BB - ki
    in_band = (sub >= 0) & (sub <= BB)
    return jnp.asarray(np.where(in_band, _t5_bucket(np.clip(sub, 0, BB) * dil), -1).astype(np.int32))


def _attn_call(q, k, v, tab, dil):
    B, _, n, C = q.shape
    BB = BAND_BLOCK
    qb = min(ATTN_QBLOCKS, n // BB)
    cur = pl.BlockSpec((None, None, qb * BB, C), lambda b, r, j: (b, r, j, 0))
    prev = pl.BlockSpec((None, None, BB, C), lambda b, r, j: (b, r, jnp.maximum(j * qb - 1, 0), 0))
    return pl.pallas_call(
        functools.partial(_attn_kernel, qblocks=qb),
        grid=(B, dil, n // (qb * BB)),
        in_specs=[pl.BlockSpec(memory_space=pltpu.SMEM),
                  pl.BlockSpec((BB, 2 * BB), lambda b, r, j: (0, 0)),
                  cur, prev, cur, prev, cur],
        out_specs=[cur, cur],
        out_shape=[jax.ShapeDtypeStruct(q.shape, F32)] * 2,
        scratch_shapes=[pltpu.VMEM((HEADS_PER_GROUP // 2, 2 * BB, 2 * BB), F32),
                        pltpu.VMEM(((qb + 1) * BB, C), BF16), pltpu.VMEM(((qb + 1) * BB, C), BF16)],
        compiler_params=pltpu.CompilerParams(
            dimension_semantics=("arbitrary", "arbitrary", "arbitrary")),
        name="prompt_attn_d%d" % dil,
    )(tab, _band_buckets(dil), q, k, k, v, v)


def _back_kernel(x_ref, ylg_ref, gatt_ref, o0_ref, o1_ref, o2_ref, l0_ref, l1_ref, l2_ref,
                 wba_ref, wo_ref, nf_ref, wfi_ref, wfo_ref,
                 qt_ref, kt_ref, vt_ref, c0_ref, c1_ref, c2_ref, tabt_ref, bkt0_ref, bkt1_ref, bkt2_ref,
                 y_ref, ot_ref,
                 so1_ref, so2_ref, sl1_ref, sl2_ref, bias0_ref, bias1_ref, bias2_ref, s0_ref, s1_ref, s2_ref,
                 *, tm, dec_parts):
    step = pl.program_id(0) * pl.num_programs(1) + pl.program_id(1)
    bias_refs = (bias0_ref, bias1_ref, bias2_ref)
    s_refs = (s0_ref, s1_ref, s2_ref)

    @pl.when(step == 0)
    def _():
        _decode_attention_init(tabt_ref, (bkt0_ref, bkt1_ref, bkt2_ref), bias_refs, s_refs, ot_ref)

    _decode_attention_step(step // dec_parts, step % dec_parts, HEADS_PER_GROUP // dec_parts,
                           qt_ref, kt_ref, vt_ref, (c0_ref, c1_ref, c2_ref), tabt_ref, bias_refs, s_refs, ot_ref)

    for (_, dil), src, dst in ((DILATED_GROUPS[1], o1_ref, so1_ref), (DILATED_GROUPS[2], o2_ref, so2_ref),
                               (DILATED_GROUPS[1], l1_ref, sl1_ref), (DILATED_GROUPS[2], l2_ref, sl2_ref)):
        _scatter_residues(dst, src[...].reshape(tm, GROUP_WIDTH), dil)
    os_ = (o0_ref[0], _from_slabs(so1_ref), _from_slabs(so2_ref))
    lses = (l0_ref[0], _from_slabs(sl1_ref), _from_slabs(sl2_ref))

    top = jnp.maximum(jnp.maximum(lses[0], lses[1]), lses[2])
    num = 0.0
    den = 0.0
    for o, lse in zip(os_, lses):
        z = jnp.exp(lse - top)
        num = num + z * o
        den = den + z
    o = num / den

    y_att = _dot(o.astype(BF16), wba_ref[...])
    mix = ylg_ref[...] + gatt_ref[...] * y_att
    x1 = x_ref[...] + _dot(mix.astype(BF16), wo_ref[...])
    y_ref[...] = _ffn(x1, _rms_rows(x1, nf_ref[...]).astype(BF16), wfi_ref, wfo_ref)


def _back_call(x, ylg, gatt, os_, lses, lw, layer, tm, qt, kt, vt, caches_t, rel_bias):
    B, S, _ = x.shape
    nt = S // tm
    nb = qt.shape[1]
    dec_parts = (B * nt) // nb
    assert dec_parts * nb == B * nt and HEADS_PER_GROUP % dec_parts == 0
    n_heads = HEADS_PER_GROUP // dec_parts
    tabt, bkts = _decode_attention_operands(rel_bias, dec_parts)

    row_spec = lambda width: pl.BlockSpec((None, tm, width), lambda b, i: (b, i, 0))
    res_specs = [pl.BlockSpec((None, dil, tm // dil, GROUP_WIDTH), lambda b, i: (b, 0, i, 0))
                 for _, dil in DILATED_GROUPS]
    cache_spec = lambda c: pl.BlockSpec(
        (None, None, 2, n_heads) + c.shape[4:],
        lambda b, i: (layer, (b * nt + i) // dec_parts, 0, (b * nt + i) % dec_parts, 0, 0))
    weights = [lw[n] for n in BACK_WEIGHTS]
    in_specs = ([row_spec(D_MODEL)] * 3 + res_specs * 2 + [_layer_spec(w, layer) for w in weights]
                + [_const_spec(qt.shape)] * 3 + [cache_spec(c) for c in caches_t]
                + [_const_spec(tabt.shape)] + [_const_spec(b.shape) for b in bkts])
    return pl.pallas_call(
        functools.partial(_back_kernel, tm=tm, dec_parts=dec_parts),
        grid=(B, nt),
        in_specs=in_specs,
        out_specs=[row_spec(D_MODEL), pl.BlockSpec((GROUP_WIDTH, nb), lambda b, i: (0, 0))],
        out_shape=[jax.ShapeDtypeStruct((B, S, D_MODEL), F32), jax.ShapeDtypeStruct((GROUP_WIDTH, nb), F32)],
        scratch_shapes=[pltpu.VMEM((GROUP_WIDTH // LANES, tm, LANES), F32)] * 4
        + [pltpu.VMEM((dec_parts, SUBLANES, window), F32) for window, _ in DILATED_GROUPS]
        + [pltpu.VMEM((SUBLANES, window), F32) for window, _ in DILATED_GROUPS],
        compiler_params=pltpu.CompilerParams(
            dimension_semantics=("arbitrary", "arbitrary"), vmem_limit_bytes=VMEM_LIMIT_BYTES),
        name="prompt_back",
    )(x, ylg, gatt, *os_, *lses, *weights, qt, kt, vt, *caches_t, tabt, *bkts)


def _dec_front_kernel(x_ref, c0_ref, c1_ref, c2_ref, h0_ref, nm_ref, win_ref, wmg_ref, bmg_ref,
                      wconv_ref, bconv_ref, wrg_ref, brga_ref, brgx_ref, lam_ref, qg_ref, kg_ref,
                      wbl_ref, pn_ref,
                      q_ref, k_ref, v_ref, ylg_ref, gatt_ref, lrux_ref, hs_ref):
    xb = _rms_rows(x_ref[...], nm_ref[...]).astype(BF16)
    lru_x = _dot(xb, win_ref[:, 0:D_RNN])
    lrux_ref[...] = lru_x
    xc = bconv_ref[...] + wconv_ref[0:1, :] * c0_ref[...]
    xc = xc + wconv_ref[1:2, :] * c1_ref[...]
    xc = xc + wconv_ref[2:3, :] * c2_ref[...]
    xc = xc + wconv_ref[3:4, :] * lru_x

    lru_g = _dot(xb, win_ref[:, D_RNN:2 * D_RNN])
    xcb = xc.astype(BF16)
    sp = _softplus(-lam_ref[...])
    ys = []
    for t in range(D_RNN // RG_TILE):
        sl = slice(t * RG_TILE, (t + 1) * RG_TILE)
        g = _dot(xcb[:, sl], wrg_ref[t])
        a, mult, ig = _lru_gates(g, brga_ref[:, sl], brgx_ref[:, sl], sp[:, sl])
        h = a * h0_ref[:, sl] + (mult * ig) * xc[:, sl]
        hs_ref[:, sl] = h
        ys.append((h * _gelu_tanh(lru_g[:, sl])).astype(BF16))
    y_lru = _dot(jnp.concatenate(ys, axis=1), wbl_ref[...])
    g_lru = _sigmoid(_dot(xb, wmg_ref[:, 0:D_MODEL]) + bmg_ref[:, 0:D_MODEL])
    ylg_ref[...] = g_lru * y_lru
    gatt_ref[...] = _sigmoid(_dot(xb, wmg_ref[:, D_MODEL:2 * D_MODEL]) + bmg_ref[:, D_MODEL:2 * D_MODEL])

    pn = pn_ref[...]
    for g in range(N_GROUPS):
        c = 2 * D_RNN + g * GROUP_WIDTH
        sl = slice(g * GROUP_WIDTH, (g + 1) * GROUP_WIDTH)
        q = _dot(xb, win_ref[:, c:c + GROUP_WIDTH])
        k = _dot(xb, win_ref[:, c + ATT_WIDTH:c + ATT_WIDTH + GROUP_WIDTH])
        v = _dot(xb, win_ref[:, c + 2 * ATT_WIDTH:c + 2 * ATT_WIDTH + GROUP_WIDTH])
        q_ref[:, sl] = _head_rms(q, pn, qg_ref[...]) * (HEAD_DIM ** -0.5)
        k_ref[:, sl] = _head_rms(k, pn, kg_ref[...])
        v_ref[:, sl] = v


def _dec_front_call(x, conv_rows, h0, lw, layer):
    nb = x.shape[0]
    weights = [lw[n] for n in FRONT_WEIGHTS]
    acts = (x,) + tuple(conv_rows) + (h0,)
    widths = (ATT_WIDTH, ATT_WIDTH, ATT_WIDTH, D_MODEL, D_MODEL, D_RNN, D_RNN)
    return pl.pallas_call(
        _dec_front_kernel,
        grid=(1,),
        in_specs=([_const_spec(a.shape) for a in acts] + [_layer_spec(w, layer) for w in weights]
                  + [_const_spec(lw["pnorm"].shape)]),
        out_specs=[pl.BlockSpec((nb, w), lambda i: (0, 0)) for w in widths],
        out_shape=[jax.ShapeDtypeStruct((nb, w), F32) for w in widths],
        compiler_params=pltpu.CompilerParams(
            dimension_semantics=("arbitrary",), vmem_limit_bytes=VMEM_LIMIT_BYTES),
        name="decode_front",
    )(*acts, *weights, lw["pnorm"])


def _decode_attention_init(tabt_ref, bkt_refs, bias_refs, s_refs, ot_ref):
    ot_ref[...] = jnp.zeros(ot_ref.shape, F32)
    for g in range(N_GROUPS):
        s_refs[g][...] = jnp.zeros(s_refs[g].shape, F32)
        bkt = bkt_refs[g][...]
        for part in range(bias_refs[g].shape[0]):
            acc = jnp.full(bkt.shape, -jnp.inf, F32)
            for n in range(NUM_BUCKETS):
                acc = jnp.where(bkt == n, tabt_ref[g, part, :, n:n + 1], acc)
            bias_refs[g][part] = acc


def _decode_attention_step(seq, part, n_heads, qt_ref, kt_ref, vt_ref, c_refs, tabt_ref, bias_refs, s_refs,
                           ot_ref):
    mine = lax.broadcasted_iota(jnp.int32, (HEAD_DIM, qt_ref.shape[1]), 1) == seq

    def column(ref, g, h):
        row0 = pl.multiple_of(g * GROUP_WIDTH + (part * n_heads + h) * HEAD_DIM, HEAD_DIM)
        return jnp.sum(jnp.where(mine, ref[pl.ds(row0, HEAD_DIM), :], 0.0), axis=1, keepdims=True)

    def fold_lanes(t, op):
        out = t[:, 0:LANES]
        for c in range(1, t.shape[1] // LANES):
            out = op(out, t[:, c * LANES:(c + 1) * LANES])
        return out

    pad = [jnp.zeros((SUBLANES - n_heads, 1), F32)] if n_heads < SUBLANES else []
    s0, vcols = [], []
    for g in range(N_GROUPS):
        rows = []
        for h in range(n_heads):
            qh = column(qt_ref, g, h)
            s_refs[g][h:h + 1, :] = jnp.sum(c_refs[g][0, h] * qh, axis=0, keepdims=True)
            rows.append(jnp.sum(qh * column(kt_ref, g, h), axis=0, keepdims=True))
            vcols.append(column(vt_ref, g, h))
        s0.append(jnp.concatenate(rows + pad, axis=0) + tabt_ref[g, part, :, 0:1])

    ss = [s_refs[g][...] + bias_refs[g][part] for g in range(N_GROUPS)]
    m_max = jnp.maximum(jnp.maximum(s0[0], s0[1]), s0[2])
    for s in ss:
        m_max = jnp.maximum(m_max, jnp.max(fold_lanes(s, jnp.maximum), axis=1, keepdims=True))
    p0 = [jnp.exp(s - m_max) for s in s0]
    den = p0[0] + p0[1] + p0[2]
    for g, s in enumerate(ss):
        p = jnp.exp(s - m_max)
        s_refs[g][...] = p
        den = den + jnp.sum(fold_lanes(p, jnp.add), axis=1, keepdims=True)
    inv_den = 1.0 / den

    for h in range(n_heads):
        acc = None
        new = 0.0
        for g in range(N_GROUPS):
            f = fold_lanes(c_refs[g][1, h] * s_refs[g][h:h + 1, :], jnp.add)
            acc = f if acc is None else acc + f
            new = new + p0[g][h:h + 1, :] * vcols[g * n_heads + h]
        num = jnp.sum(acc, axis=1, keepdims=True) + new
        row0 = pl.multiple_of((part * n_heads + h) * HEAD_DIM, HEAD_DIM)
        ot_ref[pl.ds(row0, HEAD_DIM), :] = jnp.where(mine, num * inv_den[h:h + 1, :],
                                                      ot_ref[pl.ds(row0, HEAD_DIM), :])


def _decode_attention_operands(rel_bias, parts):
    n_heads = HEADS_PER_GROUP // parts
    tab = rel_bias.reshape(NUM_BUCKETS, N_GROUPS, parts, n_heads).transpose(1, 2, 3, 0)
    tab = jnp.pad(tab, ((0, 0), (0, 0), (0, SUBLANES - n_heads), (0, 0)))
    bkts = []
    for window, dil in DILATED_GROUPS:
        w = np.arange(window)
        bkt = np.where(w % dil == 0, _t5_bucket(window - w), -1).astype(np.int32)
        bkts.append(jnp.asarray(np.broadcast_to(bkt[None, :], (SUBLANES, window))))
    return tab, bkts


def _dec_back_kernel(x_ref, ylg_ref, gatt_ref, o_ref, wba_ref, wo_ref, nf_ref, wfi_ref, wfo_ref, y_ref):
    y_att = _dot(o_ref[...].astype(BF16), wba_ref[...])
    mix = ylg_ref[...] + gatt_ref[...] * y_att
    x1 = x_ref[...] + _dot(mix.astype(BF16), wo_ref[...])
    y_ref[...] = _ffn(x1, _rms_rows(x1, nf_ref[...]).astype(BF16), wfi_ref, wfo_ref)


def _dec_back_call(x, ylg, gatt, o, lw, layer):
    weights = [lw[n] for n in BACK_WEIGHTS]
    acts = (x, ylg, gatt, o)
    return pl.pallas_call(
        _dec_back_kernel,
        grid=(1,),
        in_specs=[_const_spec(a.shape) for a in acts] + [_layer_spec(w, layer) for w in weights],
        out_specs=pl.BlockSpec(x.shape, lambda i: (0, 0)),
        out_shape=jax.ShapeDtypeStruct(x.shape, F32),
        compiler_params=pltpu.CompilerParams(
            dimension_semantics=("arbitrary",), vmem_limit_bytes=VMEM_LIMIT_BYTES),
        name="decode_back",
    )(*acts, *weights)


def _prepare_weights(norm_mix, w_in, w_conv, b_conv, w_rg_a, b_rg_a, w_rg_x, b_rg_x, lru_lambda,
                     q_gain, k_gain, w_merge, b_merge, w_branch_lru, w_branch_att, w_o, norm_ffn,
                     w_ffn_in, w_ffn_out):
    depth = norm_mix.shape[0]
    row = lambda t: t.reshape(depth, 1, -1).astype(F32)
    per_tile = RG_TILE // LRU_BLOCK
    tiles = D_RNN // RG_TILE

    def block_diag(w):
        w = w.reshape(depth, tiles, per_tile, LRU_BLOCK, LRU_BLOCK)
        eye = jnp.eye(per_tile, dtype=w.dtype)
        return jnp.einsum("ltnij,nm->ltnimj", w, eye).reshape(depth, tiles, RG_TILE, RG_TILE)

    head = jnp.arange(MXU_DIM) // HEAD_DIM
    tile_gain = lambda t: jnp.tile(t.reshape(depth, 1, HEAD_DIM), (1, 1, HEADS_PER_GROUP)).astype(F32)
    return dict(
        norm_mix=row(norm_mix), w_in=w_in.astype(BF16), w_merge=w_merge.astype(BF16),
        b_merge=row(b_merge), w_conv=w_conv.astype(F32), b_conv=row(b_conv),
        w_rg=jnp.concatenate([block_diag(w_rg_a), block_diag(w_rg_x)], axis=3).astype(BF16),
        b_rg_a=row(b_rg_a), b_rg_x=row(b_rg_x), lam=row(lru_lambda),
        q_gain=tile_gain(q_gain), k_gain=tile_gain(k_gain),
        w_branch_lru=w_branch_lru.astype(BF16),
        pnorm=((head[:, None] == head[None, :]).astype(F32) / HEAD_DIM).astype(BF16),
        w_branch_att=w_branch_att.astype(BF16), w_o=w_o.astype(BF16), norm_ffn=row(norm_ffn),
        w_ffn_in=w_ffn_in.astype(BF16), w_ffn_out=w_ffn_out.astype(BF16))


def kernel(x_prompt, x_sample, cache_kv_g0, cache_kv_g1, cache_kv_g2, state_conv, state_h, rel_bias,
           norm_mix, w_in, w_conv, b_conv, w_rg_a, b_rg_a, w_rg_x, b_rg_x, lru_lambda, q_gain, k_gain,
           w_merge, b_merge, w_branch_lru, w_branch_att, w_o, norm_ffn, w_ffn_in, w_ffn_out):
    B, S, _ = x_prompt.shape
    nb = x_sample.shape[0]
    depth = norm_mix.shape[0]
    lw = _prepare_weights(norm_mix, w_in, w_conv, b_conv, w_rg_a, b_rg_a, w_rg_x, b_rg_x, lru_lambda,
                          q_gain, k_gain, w_merge, b_merge, w_branch_lru, w_branch_att, w_o, norm_ffn,
                          w_ffn_in, w_ffn_out)
    rel_bias = rel_bias.astype(F32)
    caches_t = [jnp.transpose(c.astype(F32), (0, 1, 3, 4, 5, 2))
                for c in (cache_kv_g0, cache_kv_g1, cache_kv_g2)]

    yp = x_prompt
    ys = x_sample.reshape(nb, D_MODEL)
    kvp = [[] for _ in range(N_GROUPS)]
    kvs = [[] for _ in range(N_GROUPS)]
    conv_p, h_p, conv_s, h_s = [], [], [], []
    for l in range(depth):
        sc = state_conv[l].astype(F32)
        conv_rows = [sc[:, r, :] for r in range(CONV_WIDTH - 1)]
        qs, ks, vs, ylg_s, gatt_s, lrux_s, hs_s = _dec_front_call(ys, conv_rows, state_h[l].astype(F32), lw, l)

        (q0, q1, q2, k0, k1, k2, v0, v1, v2, ylg, gatt, kv0, kv1, kv2, ctail, hfin) = _front_call(
            yp, lw, l, TM_PROMPT)
        os_, lses = [], []
        for g, (qg, kg, vg) in enumerate(((q0, k0, v0), (q1, k1, v1), (q2, k2, v2))):
            tab = rel_bias[:, g * HEADS_PER_GROUP:(g + 1) * HEADS_PER_GROUP]
            o, lse = _attn_call(qg, kg, vg, tab, DILATED_GROUPS[g][1])
            os_.append(o)
            lses.append(lse)
        yp, o_t = _back_call(yp, ylg, gatt, os_, lses, lw, l, TM_PROMPT, qs.T, ks.T, vs.T, caches_t, rel_bias)
        for g, kv in enumerate((kv0, kv1, kv2)):
            kvp[g].append(kv.reshape(B, kv.shape[1], 2, HEADS_PER_GROUP, HEAD_DIM))
        conv_p.append(ctail[:, SUBLANES - (CONV_WIDTH - 1):, :])
        h_p.append(hfin[:, 0, :])

        ys = _dec_back_call(ys, ylg_s, gatt_s, o_t.T, lw, l)
        for g in range(N_GROUPS):
            sl = slice(g * GROUP_WIDTH, (g + 1) * GROUP_WIDTH)
            kvs[g].append(jnp.stack([ks[:, sl], vs[:, sl]], axis=1).reshape(
                nb, 1, 2, HEADS_PER_GROUP, HEAD_DIM))
        conv_s.append(jnp.stack(conv_rows[1:] + [lrux_s], axis=1))
        h_s.append(hs_s)

    stack = jnp.stack
    return (yp, ys.reshape(nb, 1, D_MODEL),
            stack(kvp[0]), stack(kvp[1]), stack(kvp[2]), stack(conv_p), stack(h_p),
            stack(kvs[0]), stack(kvs[1]), stack(kvs[2]), stack(conv_s), stack(h_s))
```

```python
import functools
import math

import numpy as np
import jax
import jax.numpy as jnp
from jax import lax
from jax.experimental import pallas as pl
from jax.experimental.pallas import tpu as pltpu

D_MODEL = 1024
D_RNN = D_MODEL
N_LRU_BLOCKS = 16
LRU_BLOCK = D_RNN // N_LRU_BLOCKS
CONV_WIDTH = 4
LRU_C = 8.0
HEAD_DIM = 64
HEADS_PER_GROUP = 8
DILATED_GROUPS = ((128, 1), (512, 4), (2048, 16))
N_GROUPS = len(DILATED_GROUPS)
N_ATT_HEADS = N_GROUPS * HEADS_PER_GROUP
ATT_WIDTH = N_ATT_HEADS * HEAD_DIM
GROUP_WIDTH = HEADS_PER_GROUP * HEAD_DIM
BAND_BLOCK = 128
NUM_BUCKETS = 32
MAX_DISTANCE = 2048
D_FF = 2816
EPS = 1e-6

F32 = jnp.float32
BF16 = jnp.bfloat16

MXU_DIM = 256
SUBLANES = 8
LANES = 128
VMEM_LIMIT_BYTES = 56 * 1024 * 1024

TM_PROMPT = 256
ATTN_QBLOCKS = 8
RG_TILE = MXU_DIM
FFN_CHUNKS = ((0, 768), (768, 768), (1536, 768), (2304, 512))


def _dot(a, b):
    return jnp.dot(a, b, preferred_element_type=F32)


def _dot_nt(a, b):
    return lax.dot_general(a, b, (((1,), (1,)), ((), ())), preferred_element_type=F32)


def _sigmoid(x):
    return 0.5 * (jnp.tanh(0.5 * x) + 1.0)


def _gelu_tanh(x):
    c = math.sqrt(2.0 / math.pi)
    return 0.5 * x * (1.0 + jnp.tanh(c * (x + 0.044715 * (x * x * x))))


def _softplus(z):
    return jnp.maximum(z, 0.0) + jnp.log1p(jnp.exp(-jnp.abs(z)))


def _rms_rows(x, g):
    y = x * lax.rsqrt(jnp.mean(x * x, axis=-1, keepdims=True) + EPS)
    return y * g


def _head_rms(t, pn, gain):
    t2 = (t * t).astype(BF16)
    tiles = [_dot(t2[:, c * MXU_DIM:(c + 1) * MXU_DIM], pn) for c in range(t.shape[1] // MXU_DIM)]
    ms = tiles[0] if len(tiles) == 1 else jnp.concatenate(tiles, axis=1)
    return (t * lax.rsqrt(ms + EPS)) * gain


def _lru_gates(g, b_a, b_x, sp):
    w = g.shape[1] // 2
    r = _sigmoid(g[:, :w] + b_a)
    ig = _sigmoid(g[:, w:] + b_x)
    log_a = (-LRU_C * r) * sp
    a = jnp.exp(log_a)
    mult = jnp.sqrt(-jnp.tanh(log_a) * (a * a + 1.0))
    return a, mult, ig


def _scan_rows(a, u, h_in):
    rows, c = a.shape
    groups = rows // SUBLANES
    a3 = a.reshape(groups, SUBLANES, c)
    u3 = u.reshape(groups, SUBLANES, c)
    row = lax.broadcasted_iota(jnp.int32, (groups, SUBLANES, c), 1)
    shift = 1
    while shift < SUBLANES:
        ok = row >= shift
        a_sh = jnp.where(ok, pltpu.roll(a3, shift, axis=1), 1.0)
        u_sh = jnp.where(ok, pltpu.roll(u3, shift, axis=1), 0.0)
        u3 = u3 + a3 * u_sh
        a3 = a3 * a_sh
        shift *= 2
    out = []
    h = h_in
    for g in range(groups):
        hg = a3[g] * h + u3[g]
        out.append(hg)
        h = hg[SUBLANES - 1:SUBLANES, :]
    return jnp.concatenate(out, axis=0), h


def _to_slabs(slab_ref, t, first_slab=0):
    for c in range(t.shape[1] // LANES):
        slab_ref[first_slab + c] = t[:, c * LANES:(c + 1) * LANES]


def _from_slabs(slab_ref, first_slab=0, n_slabs=None):
    n_slabs = slab_ref.shape[0] - first_slab if n_slabs is None else n_slabs
    return jnp.concatenate([slab_ref[first_slab + c] for c in range(n_slabs)], axis=1)


def _gather_residues(slab_ref, dil, first_slab=0, n_slabs=None):
    n_slabs = slab_ref.shape[0] - first_slab if n_slabs is None else n_slabs
    per = slab_ref.shape[1] // dil
    cols = []
    for c in range(first_slab, first_slab + n_slabs):
        cols.append(jnp.concatenate(
            [slab_ref[c, pl.ds(r, per, stride=dil), :] for r in range(dil)], axis=0))
    return jnp.concatenate(cols, axis=1)


def _scatter_residues(slab_ref, t, dil, first_slab=0):
    per = slab_ref.shape[1] // dil
    for c in range(t.shape[1] // LANES):
        for r in range(dil):
            slab_ref[first_slab + c, pl.ds(r, per, stride=dil), :] = (
                t[r * per:(r + 1) * per, c * LANES:(c + 1) * LANES])


def _ffn(x1, xb, wfi_ref, wfo_ref):
    acc = x1
    for start, width in FFN_CHUNKS:
        gate = _dot(xb, wfi_ref[:, start:start + width])
        up = _dot(xb, wfi_ref[:, D_FF + start:D_FF + start + width])
        hid = (gate * _sigmoid(gate)) * up
        acc = acc + _dot(hid.astype(BF16), wfo_ref[start:start + width, :])
    return acc


def _layer_spec(w, layer):
    tail = (0,) * (w.ndim - 1)
    return pl.BlockSpec((None,) + w.shape[1:], lambda *_: (layer,) + tail, pipeline_mode=pl.Buffered(1))


def _const_spec(shape):
    zeros = (0,) * len(shape)
    return pl.BlockSpec(shape, lambda *_: zeros, pipeline_mode=pl.Buffered(1))


FRONT_WEIGHTS = ("norm_mix", "w_in", "w_merge", "b_merge", "w_conv", "b_conv", "w_rg", "b_rg_a", "b_rg_x",
                 "lam", "q_gain", "k_gain", "w_branch_lru")
BACK_WEIGHTS = ("w_branch_att", "w_o", "norm_ffn", "w_ffn_in", "w_ffn_out")


def _front_kernel(x_ref, nm_ref, win_ref, wmg_ref, bmg_ref, wconv_ref, bconv_ref, wrg_ref,
                  brga_ref, brgx_ref, lam_ref, qg_ref, kg_ref, wbl_ref, pn_ref,
                  q0_ref, q1_ref, q2_ref, k0_ref, k1_ref, k2_ref, v0_ref, v1_ref, v2_ref,
                  ylg_ref, gatt_ref, kvp0_ref, kvp1_ref, kvp2_ref, convp_ref, hp_ref,
                  xn_ref, kv_ref, ext_ref, h_ref, xb_ref, xg_ref, xc_ref, xcb_ref, gl_ref, y_ref, *, tm):
    i = pl.program_id(1)
    n_tiles = D_RNN // RG_TILE

    @pl.when(i == 0)
    def _():
        ext_ref[0:SUBLANES, :] = jnp.zeros((SUBLANES, D_RNN), F32)
        h_ref[...] = jnp.zeros((SUBLANES, D_RNN), F32)

    xn = _rms_rows(x_ref[...], nm_ref[...])
    _to_slabs(xn_ref, xn)
    xb_ref[...] = xn.astype(BF16)

    for t in range(n_tiles):
        sl = slice(t * RG_TILE, (t + 1) * RG_TILE)
        lru_x = _dot(xb_ref[...], win_ref[:, sl])
        ext_ref[SUBLANES:SUBLANES + tm, sl] = lru_x
        xc = bconv_ref[:, sl] + wconv_ref[0:1, sl] * ext_ref[SUBLANES - 3:SUBLANES - 3 + tm, sl]
        xc = xc + wconv_ref[1:2, sl] * ext_ref[SUBLANES - 2:SUBLANES - 2 + tm, sl]
        xc = xc + wconv_ref[2:3, sl] * ext_ref[SUBLANES - 1:SUBLANES - 1 + tm, sl]
        xc = xc + wconv_ref[3:4, sl] * lru_x
        xc_ref[:, sl] = xc
        xcb_ref[:, sl] = xc.astype(BF16)
        gl_ref[:, sl] = _gelu_tanh(_dot(xb_ref[...], win_ref[:, D_RNN + t * RG_TILE:D_RNN + (t + 1) * RG_TILE]))
    tail = ext_ref[tm:tm + SUBLANES, :]
    ext_ref[0:SUBLANES, :] = tail
    convp_ref[...] = tail

    sp = _softplus(-lam_ref[...])
    first_row = (lax.broadcasted_iota(jnp.int32, (tm, RG_TILE), 0) == 0) & (i == 0)
    gates = {}

    def lru_gates(t):
        sl = slice(t * RG_TILE, (t + 1) * RG_TILE)
        g = _dot(xcb_ref[:, sl], wrg_ref[t])
        a, mult, ig = _lru_gates(g, brga_ref[:, sl], brgx_ref[:, sl], sp[:, sl])
        mult = jnp.where(first_row, 1.0, mult)
        gates[t] = (a, (mult * ig) * xc_ref[:, sl])

    def lru_scan(t):
        sl = slice(t * RG_TILE, (t + 1) * RG_TILE)
        a, u = gates.pop(t)
        hs, h_last = _scan_rows(a, u, h_ref[0:1, sl])
        h_ref[:, sl] = jnp.broadcast_to(h_last, (SUBLANES, RG_TILE))
        y_ref[:, sl] = (hs * gl_ref[:, sl]).astype(BF16)

    pn = pn_ref[...]
    q_refs = (q0_ref, q1_ref, q2_ref)
    k_refs = (k0_ref, k1_ref, k2_ref)
    v_refs = (v0_ref, v1_ref, v2_ref)
    kvp_refs = (kvp0_ref, kvp1_ref, kvp2_ref)

    def attn_operands(g, half):
        window, dil = DILATED_GROUPS[g]
        per = tm // dil
        keep = min(window, tm)
        if dil == 1:
            src = xb_ref
        else:
            if half == 0:
                xg_ref[...] = _gather_residues(xn_ref, dil).astype(BF16)
            src = xg_ref
        c = 2 * D_RNN + g * GROUP_WIDTH + half * MXU_DIM
        hl = slice(half * MXU_DIM, (half + 1) * MXU_DIM)
        vl = slice(GROUP_WIDTH + half * MXU_DIM, GROUP_WIDTH + (half + 1) * MXU_DIM)
        q = _dot(src[...], win_ref[:, c:c + MXU_DIM])
        k = _dot(src[...], win_ref[:, c + ATT_WIDTH:c + ATT_WIDTH + MXU_DIM])
        v = _dot(src[...], win_ref[:, c + 2 * ATT_WIDTH:c + 2 * ATT_WIDTH + MXU_DIM])
        qn = (_head_rms(q, pn, qg_ref[:, hl]) * (HEAD_DIM ** -0.5)).astype(BF16)
        kn = _head_rms(k, pn, kg_ref[:, hl])
        knb = kn.astype(BF16)
        vb = v.astype(BF16)
        for r in range(dil):
            rows = slice(r * per, (r + 1) * per)
            q_refs[g][r, :, hl] = qn[rows]
            k_refs[g][r, :, hl] = knb[rows]
            v_refs[g][r, :, hl] = vb[rows]
        if dil == 1:
            kvp_refs[g][:, hl] = kn[tm - keep:, :]
            kvp_refs[g][:, vl] = v[tm - keep:, :]
        else:
            _scatter_residues(kv_ref, kn, dil, first_slab=hl.start // LANES)
            _scatter_residues(kv_ref, v, dil, first_slab=vl.start // LANES)
            if half == GROUP_WIDTH // MXU_DIM - 1:
                kvp_refs[g][...] = _from_slabs(kv_ref)

    def att_gate(t):
        sl = slice(t * RG_TILE, (t + 1) * RG_TILE)
        al = slice(D_MODEL + t * RG_TILE, D_MODEL + (t + 1) * RG_TILE)
        gatt_ref[:, sl] = _sigmoid(_dot(xb_ref[...], wmg_ref[:, al]) + bmg_ref[:, al]).astype(gatt_ref.dtype)

    for t in range(n_tiles):
        lru_gates(t)
        if t < N_GROUPS:
            attn_operands(t, 0)
        else:
            att_gate(0)
            att_gate(1)
        lru_scan(t)
        if t < N_GROUPS:
            attn_operands(t, 1)
        else:
            att_gate(2)
            att_gate(3)
    hp_ref[...] = h_ref[...]
    for t in range(n_tiles):
        sl = slice(t * RG_TILE, (t + 1) * RG_TILE)
        g_lru = _sigmoid(_dot(xb_ref[...], wmg_ref[:, sl]) + bmg_ref[:, sl])
        ylg_ref[:, sl] = (g_lru * _dot(y_ref[...], wbl_ref[:, sl])).astype(ylg_ref.dtype)


def _front_call(x, lw, layer, tm):
    B, S, _ = x.shape
    nt = S // tm
    row_spec = lambda width: pl.BlockSpec((None, tm, width), lambda b, i: (b, i, 0))
    weights = [lw[n] for n in FRONT_WEIGHTS]
    in_specs = ([row_spec(D_MODEL)] + [_layer_spec(w, layer) for w in weights]
                + [_const_spec(lw["pnorm"].shape)])

    out_shape, out_specs = [], []
    for _ in range(3):
        for _, dil in DILATED_GROUPS:
            out_shape.append(jax.ShapeDtypeStruct((B, dil, S // dil, GROUP_WIDTH), BF16))
            out_specs.append(pl.BlockSpec((None, dil, tm // dil, GROUP_WIDTH), lambda b, i: (b, 0, i, 0)))
    out_shape.append(jax.ShapeDtypeStruct((B, S, D_MODEL), BF16))
    out_specs.append(row_spec(D_MODEL))
    out_shape.append(jax.ShapeDtypeStruct((B, S, D_MODEL), BF16))
    out_specs.append(row_spec(D_MODEL))
    for window, _ in DILATED_GROUPS:
        keep = min(window, tm)
        first = nt - window // keep
        out_shape.append(jax.ShapeDtypeStruct((B, window, 2 * GROUP_WIDTH), F32))
        out_specs.append(pl.BlockSpec(
            (None, keep, 2 * GROUP_WIDTH),
            functools.partial(lambda b, i, first: (b, jnp.maximum(i - first, 0), 0), first=first)))
    for _ in range(2):
        out_shape.append(jax.ShapeDtypeStruct((B, SUBLANES, D_RNN), F32))
        out_specs.append(pl.BlockSpec((None, SUBLANES, D_RNN), lambda b, i: (b, 0, 0)))

    return pl.pallas_call(
        functools.partial(_front_kernel, tm=tm),
        grid=(B, nt),
        in_specs=in_specs,
        out_specs=out_specs,
        out_shape=out_shape,
        scratch_shapes=[pltpu.VMEM((D_MODEL // LANES, tm, LANES), F32),
                        pltpu.VMEM((2 * GROUP_WIDTH // LANES, tm, LANES), F32),
                        pltpu.VMEM((tm + SUBLANES, D_RNN), F32), pltpu.VMEM((SUBLANES, D_RNN), F32),
                        pltpu.VMEM((tm, D_MODEL), BF16), pltpu.VMEM((tm, D_MODEL), BF16),
                        pltpu.VMEM((tm, D_RNN), F32), pltpu.VMEM((tm, D_RNN), BF16),
                        pltpu.VMEM((tm, D_RNN), F32), pltpu.VMEM((tm, D_RNN), BF16)],
        compiler_params=pltpu.CompilerParams(
            dimension_semantics=("arbitrary", "arbitrary"), vmem_limit_bytes=VMEM_LIMIT_BYTES),
        name="prompt_front",
    )(x, *weights, lw["pnorm"])


def _attn_kernel(tab_ref, bkt_ref, q_ref, kp_ref, kc_ref, vp_ref, vc_ref, o_ref, lse_ref,
                 bias_ref, kbuf_ref, vbuf_ref, *, qblocks):
    b, r, j = pl.program_id(0), pl.program_id(1), pl.program_id(2)
    BB = BAND_BLOCK
    PAIR = 2 * HEAD_DIM

    @pl.when((b == 0) & (r == 0) & (j == 0))
    def _():
        bkt = bkt_ref[...]
        for h in range(HEADS_PER_GROUP):
            acc = jnp.full(bkt.shape, -jnp.inf, F32)
            for n in range(NUM_BUCKETS):
                acc = jnp.where(bkt == n, tab_ref[n, h], acc)
            bias_ref[h // 2, (h % 2) * BB:(h % 2 + 1) * BB, :] = acc

    first_col = lax.broadcasted_iota(jnp.int32, (1, 2 * BB), 1) < BB
    pen = jnp.where(first_col & (j == 0), -jnp.inf, 0.0).astype(F32)
    low_half = lax.broadcasted_iota(jnp.int32, (BB, PAIR), 1) < HEAD_DIM
    ones = jnp.ones((2 * BB, PAIR), BF16)
    for rr in range(q_ref.shape[0]):
        kbuf_ref[0:BB, :] = kp_ref[rr]
        kbuf_ref[BB:, :] = kc_ref[rr]
        vbuf_ref[0:BB, :] = vp_ref[rr]
        vbuf_ref[BB:, :] = vc_ref[rr]
        for t in range(qblocks):
            rows = slice(t * BB, (t + 1) * BB)
            keys = slice(t * BB, (t + 2) * BB)
            for hp in range(HEADS_PER_GROUP // 2):
                sl = slice(hp * PAIR, (hp + 1) * PAIR)
                qp = q_ref[rr, rows, sl]
                zero = jnp.zeros_like(qp)
                lhs = jnp.concatenate([jnp.where(low_half, qp, zero), jnp.where(low_half, zero, qp)], axis=0)
                s = _dot_nt(lhs, kbuf_ref[keys, sl]) + bias_ref[hp]
                if t == 0:
                    s = s + pen
                m = jnp.max(s, axis=-1, keepdims=True)
                p = jnp.exp((s - m).astype(BF16))
                oe = _dot(p, jnp.concatenate([vbuf_ref[keys, sl], ones], axis=1))
                l = oe[:, PAIR:]
                o = oe[:, :PAIR] / l
                lse = m + jnp.log(l)
                for e in range(2):
                    hl = slice(hp * PAIR + e * HEAD_DIM, hp * PAIR + (e + 1) * HEAD_DIM)
                    o_ref[rr, rows, hl] = o[e * BB:(e + 1) * BB, e * HEAD_DIM:(e + 1) * HEAD_DIM]
                    lse_ref[rr, rows, hl] = lse[e * BB:(e + 1) * BB, e * HEAD_DIM:(e + 1) * HEAD_DIM]


def _t5_bucket(dist):
    max_exact = NUM_BUCKETS // 2
    d_f = np.maximum(dist, 1).astype(np.float32)
    large = max_exact + (np.log(d_f / np.float32(max_exact)) / np.float32(math.log(MAX_DISTANCE / max_exact))
                         * np.float32(NUM_BUCKETS - max_exact)).astype(np.int32)
    large = np.minimum(large, NUM_BUCKETS - 1)
    return np.where(dist < max_exact, dist, large).astype(np.int32)


def _band_buckets(dil):
    BB = BAND_BLOCK
    qi = np.arange(BB)[:, None]
    ki = np.arange(2 * BB)[None, :]
    sub = qi + BB - ki
    in_band = (sub >= 0) & (sub <= BB)
    return jnp.asarray(np.where(in_band, _t5_bucket(np.clip(sub, 0, BB) * dil), -1).astype(np.int32))


def _attn_call(q, k, v, tab, dil):
    B, _, n, C = q.shape
    BB = BAND_BLOCK
    qb = min(ATTN_QBLOCKS, n // BB)
    res = ATTN_QBLOCKS // qb
    cur = pl.BlockSpec((None, res, qb * BB, C), lambda b, r, j: (b, r, j, 0))
    prev = pl.BlockSpec((None, res, BB, C), lambda b, r, j: (b, r, jnp.maximum(j * qb - 1, 0), 0))
    return pl.pallas_call(
        functools.partial(_attn_kernel, qblocks=qb),
        grid=(B, dil // res, n // (qb * BB)),
        in_specs=[pl.BlockSpec(memory_space=pltpu.SMEM),
                  pl.BlockSpec((BB, 2 * BB), lambda b, r, j: (0, 0)),
                  cur, prev, cur, prev, cur],
        out_specs=[cur, cur],
        out_shape=[jax.ShapeDtypeStruct(q.shape, F32)] * 2,
        scratch_shapes=[pltpu.VMEM((HEADS_PER_GROUP // 2, 2 * BB, 2 * BB), F32),
                        pltpu.VMEM(((qb + 1) * BB, C), BF16), pltpu.VMEM(((qb + 1) * BB, C), BF16)],
        compiler_params=pltpu.CompilerParams(
            dimension_semantics=("arbitrary", "arbitrary", "arbitrary")),
        name="prompt_attn_d%d" % dil,
    )(tab, _band_buckets(dil), q, k, k, v, v)


def _back_kernel(x_ref, ylg_ref, gatt_ref, o0_ref, o1_ref, o2_ref, l0_ref, l1_ref, l2_ref,
                 wba_ref, wo_ref, nf_ref, wfi_ref, wfo_ref,
                 qt_ref, kt_ref, vt_ref, c0_ref, c1_ref, c2_ref, tabt_ref, bkt0_ref, bkt1_ref, bkt2_ref,
                 y_ref, ot_ref,
                 so1_ref, so2_ref, sl1_ref, sl2_ref, bias0_ref, bias1_ref, bias2_ref, s0_ref, s1_ref, s2_ref,
                 *, tm, dec_parts):
    step = pl.program_id(0) * pl.num_programs(1) + pl.program_id(1)
    bias_refs = (bias0_ref, bias1_ref, bias2_ref)
    s_refs = (s0_ref, s1_ref, s2_ref)

    @pl.when(step == 0)
    def _():
        _decode_attention_init(tabt_ref, (bkt0_ref, bkt1_ref, bkt2_ref), bias_refs, s_refs, ot_ref)

    _decode_attention_step(step // dec_parts, step % dec_parts, HEADS_PER_GROUP // dec_parts,
                           qt_ref, kt_ref, vt_ref, (c0_ref, c1_ref, c2_ref), tabt_ref, bias_refs, s_refs, ot_ref)

    for (_, dil), src, dst in ((DILATED_GROUPS[1], o1_ref, so1_ref), (DILATED_GROUPS[2], o2_ref, so2_ref),
                               (DILATED_GROUPS[1], l1_ref, sl1_ref), (DILATED_GROUPS[2], l2_ref, sl2_ref)):
        _scatter_residues(dst, src[...].reshape(tm, GROUP_WIDTH), dil)
    os_ = (o0_ref[0], _from_slabs(so1_ref), _from_slabs(so2_ref))
    lses = (l0_ref[0], _from_slabs(sl1_ref), _from_slabs(sl2_ref))

    top = jnp.maximum(jnp.maximum(lses[0], lses[1]), lses[2])
    num = 0.0
    den = 0.0
    for o, lse in zip(os_, lses):
        z = jnp.exp(lse - top)
        num = num + z * o
        den = den + z
    o = num / den

    y_att = _dot(o.astype(BF16), wba_ref[...])
    mix = ylg_ref[...] + gatt_ref[...] * y_att
    x1 = x_ref[...] + _dot(mix.astype(BF16), wo_ref[...])
    y_ref[...] = _ffn(x1, _rms_rows(x1, nf_ref[...]).astype(BF16), wfi_ref, wfo_ref)


def _back_call(x, ylg, gatt, os_, lses, lw, layer, tm, qt, kt, vt, caches_t, rel_bias):
    B, S, _ = x.shape
    nt = S // tm
    nb = qt.shape[1]
    dec_parts = (B * nt) // nb
    assert dec_parts * nb == B * nt and HEADS_PER_GROUP % dec_parts == 0
    n_heads = HEADS_PER_GROUP // dec_parts
    tabt, bkts = _decode_attention_operands(rel_bias, dec_parts)

    row_spec = lambda width: pl.BlockSpec((None, tm, width), lambda b, i: (b, i, 0))
    res_specs = [pl.BlockSpec((None, dil, tm // dil, GROUP_WIDTH), lambda b, i: (b, 0, i, 0))
                 for _, dil in DILATED_GROUPS]
    cache_spec = lambda c: pl.BlockSpec(
        (None, None, 2, n_heads) + c.shape[4:],
        lambda b, i: (layer, (b * nt + i) // dec_parts, 0, (b * nt + i) % dec_parts, 0, 0))
    weights = [lw[n] for n in BACK_WEIGHTS]
    in_specs = ([row_spec(D_MODEL)] * 3 + res_specs * 2 + [_layer_spec(w, layer) for w in weights]
                + [_const_spec(qt.shape)] * 3 + [cache_spec(c) for c in caches_t]
                + [_const_spec(tabt.shape)] + [_const_spec(b.shape) for b in bkts])
    return pl.pallas_call(
        functools.partial(_back_kernel, tm=tm, dec_parts=dec_parts),
        grid=(B, nt),
        in_specs=in_specs,
        out_specs=[row_spec(D_MODEL), pl.BlockSpec((GROUP_WIDTH, nb), lambda b, i: (0, 0))],
        out_shape=[jax.ShapeDtypeStruct((B, S, D_MODEL), F32), jax.ShapeDtypeStruct((GROUP_WIDTH, nb), F32)],
        scratch_shapes=[pltpu.VMEM((GROUP_WIDTH // LANES, tm, LANES), F32)] * 4
        + [pltpu.VMEM((dec_parts, SUBLANES, window), F32) for window, _ in DILATED_GROUPS]
        + [pltpu.VMEM((SUBLANES, window), F32) for window, _ in DILATED_GROUPS],
        compiler_params=pltpu.CompilerParams(
            dimension_semantics=("arbitrary", "arbitrary"), vmem_limit_bytes=VMEM_LIMIT_BYTES),
        name="prompt_back",
    )(x, ylg, gatt, *os_, *lses, *weights, qt, kt, vt, *caches_t, tabt, *bkts)


def _dec_front_kernel(x_ref, c0_ref, c1_ref, c2_ref, h0_ref, nm_ref, win_ref, wmg_ref, bmg_ref,
                      wconv_ref, bconv_ref, wrg_ref, brga_ref, brgx_ref, lam_ref, qg_ref, kg_ref,
                      wbl_ref, pn_ref,
                      q_ref, k_ref, v_ref, ylg_ref, gatt_ref, lrux_ref, hs_ref):
    xb = _rms_rows(x_ref[...], nm_ref[...]).astype(BF16)
    lru_x = _dot(xb, win_ref[:, 0:D_RNN])
    lrux_ref[...] = lru_x
    xc = bconv_ref[...] + wconv_ref[0:1, :] * c0_ref[...]
    xc = xc + wconv_ref[1:2, :] * c1_ref[...]
    xc = xc + wconv_ref[2:3, :] * c2_ref[...]
    xc = xc + wconv_ref[3:4, :] * lru_x

    lru_g = _dot(xb, win_ref[:, D_RNN:2 * D_RNN])
    xcb = xc.astype(BF16)
    sp = _softplus(-lam_ref[...])
    ys = []
    for t in range(D_RNN // RG_TILE):
        sl = slice(t * RG_TILE, (t + 1) * RG_TILE)
        g = _dot(xcb[:, sl], wrg_ref[t])
        a, mult, ig = _lru_gates(g, brga_ref[:, sl], brgx_ref[:, sl], sp[:, sl])
        h = a * h0_ref[:, sl] + (mult * ig) * xc[:, sl]
        hs_ref[:, sl] = h
        ys.append((h * _gelu_tanh(lru_g[:, sl])).astype(BF16))
    y_lru = _dot(jnp.concatenate(ys, axis=1), wbl_ref[...])
    g_lru = _sigmoid(_dot(xb, wmg_ref[:, 0:D_MODEL]) + bmg_ref[:, 0:D_MODEL])
    ylg_ref[...] = g_lru * y_lru
    gatt_ref[...] = _sigmoid(_dot(xb, wmg_ref[:, D_MODEL:2 * D_MODEL]) + bmg_ref[:, D_MODEL:2 * D_MODEL])

    pn = pn_ref[...]
    for g in range(N_GROUPS):
        c = 2 * D_RNN + g * GROUP_WIDTH
        sl = slice(g * GROUP_WIDTH, (g + 1) * GROUP_WIDTH)
        q = _dot(xb, win_ref[:, c:c + GROUP_WIDTH])
        k = _dot(xb, win_ref[:, c + ATT_WIDTH:c + ATT_WIDTH + GROUP_WIDTH])
        v = _dot(xb, win_ref[:, c + 2 * ATT_WIDTH:c + 2 * ATT_WIDTH + GROUP_WIDTH])
        q_ref[:, sl] = _head_rms(q, pn, qg_ref[...]) * (HEAD_DIM ** -0.5)
        k_ref[:, sl] = _head_rms(k, pn, kg_ref[...])
        v_ref[:, sl] = v


def _dec_front_call(x, conv_rows, h0, lw, layer):
    nb = x.shape[0]
    weights = [lw[n] for n in FRONT_WEIGHTS]
    acts = (x,) + tuple(conv_rows) + (h0,)
    widths = (ATT_WIDTH, ATT_WIDTH, ATT_WIDTH, D_MODEL, D_MODEL, D_RNN, D_RNN)
    return pl.pallas_call(
        _dec_front_kernel,
        grid=(1,),
        in_specs=([_const_spec(a.shape) for a in acts] + [_layer_spec(w, layer) for w in weights]
                  + [_const_spec(lw["pnorm"].shape)]),
        out_specs=[pl.BlockSpec((nb, w), lambda i: (0, 0)) for w in widths],
        out_shape=[jax.ShapeDtypeStruct((nb, w), F32) for w in widths],
        compiler_params=pltpu.CompilerParams(
            dimension_semantics=("arbitrary",), vmem_limit_bytes=VMEM_LIMIT_BYTES),
        name="decode_front",
    )(*acts, *weights, lw["pnorm"])


def _decode_attention_init(tabt_ref, bkt_refs, bias_refs, s_refs, ot_ref):
    ot_ref[...] = jnp.zeros(ot_ref.shape, F32)
    for g in range(N_GROUPS):
        s_refs[g][...] = jnp.zeros(s_refs[g].shape, F32)
        bkt = bkt_refs[g][...]
        for part in range(bias_refs[g].shape[0]):
            acc = jnp.full(bkt.shape, -jnp.inf, F32)
            for n in range(NUM_BUCKETS):
                acc = jnp.where(bkt == n, tabt_ref[g, part, :, n:n + 1], acc)
            bias_refs[g][part] = acc


def _decode_attention_step(seq, part, n_heads, qt_ref, kt_ref, vt_ref, c_refs, tabt_ref, bias_refs, s_refs,
                           ot_ref):
    mine = lax.broadcasted_iota(jnp.int32, (HEAD_DIM, qt_ref.shape[1]), 1) == seq

    def column(ref, g, h):
        row0 = pl.multiple_of(g * GROUP_WIDTH + (part * n_heads + h) * HEAD_DIM, HEAD_DIM)
        return jnp.sum(jnp.where(mine, ref[pl.ds(row0, HEAD_DIM), :], 0.0), axis=1, keepdims=True)

    def fold_lanes(t, op):
        out = t[:, 0:LANES]
        for c in range(1, t.shape[1] // LANES):
            out = op(out, t[:, c * LANES:(c + 1) * LANES])
        return out

    pad = [jnp.zeros((SUBLANES - n_heads, 1), F32)] if n_heads < SUBLANES else []
    s0, vcols = [], []
    for g in range(N_GROUPS):
        rows = []
        for h in range(n_heads):
            qh = column(qt_ref, g, h)
            s_refs[g][h:h + 1, :] = jnp.sum(c_refs[g][0, h] * qh, axis=0, keepdims=True)
            rows.append(jnp.sum(qh * column(kt_ref, g, h), axis=0, keepdims=True))
            vcols.append(column(vt_ref, g, h))
        s0.append(jnp.concatenate(rows + pad, axis=0) + tabt_ref[g, part, :, 0:1])

    ss = [s_refs[g][...] + bias_refs[g][part] for g in range(N_GROUPS)]
    m_max = jnp.maximum(jnp.maximum(s0[0], s0[1]), s0[2])
    for s in ss:
        m_max = jnp.maximum(m_max, jnp.max(fold_lanes(s, jnp.maximum), axis=1, keepdims=True))
    p0 = [jnp.exp(s - m_max) for s in s0]
    den = p0[0] + p0[1] + p0[2]
    for g, s in enumerate(ss):
        p = jnp.exp(s - m_max)
        s_refs[g][...] = p
        den = den + jnp.sum(fold_lanes(p, jnp.add), axis=1, keepdims=True)
    inv_den = 1.0 / den

    for h in range(n_heads):
        acc = None
        new = 0.0
        for g in range(N_GROUPS):
            f = fold_lanes(c_refs[g][1, h] * s_refs[g][h:h + 1, :], jnp.add)
            acc = f if acc is None else acc + f
            new = new + p0[g][h:h + 1, :] * vcols[g * n_heads + h]
        num = jnp.sum(acc, axis=1, keepdims=True) + new
        row0 = pl.multiple_of((part * n_heads + h) * HEAD_DIM, HEAD_DIM)
        ot_ref[pl.ds(row0, HEAD_DIM), :] = jnp.where(mine, num * inv_den[h:h + 1, :],
                                                      ot_ref[pl.ds(row0, HEAD_DIM), :])


def _decode_attention_operands(rel_bias, parts):
    n_heads = HEADS_PER_GROUP // parts
    tab = rel_bias.reshape(NUM_BUCKETS, N_GROUPS, parts, n_heads).transpose(1, 2, 3, 0)
    tab = jnp.pad(tab, ((0, 0), (0, 0), (0, SUBLANES - n_heads), (0, 0)))
    bkts = []
    for window, dil in DILATED_GROUPS:
        w = np.arange(window)
        bkt = np.where(w % dil == 0, _t5_bucket(window - w), -1).astype(np.int32)
        bkts.append(jnp.asarray(np.broadcast_to(bkt[None, :], (SUBLANES, window))))
    return tab, bkts


def _dec_back_kernel(x_ref, ylg_ref, gatt_ref, o_ref, wba_ref, wo_ref, nf_ref, wfi_ref, wfo_ref, y_ref):
    y_att = _dot(o_ref[...].astype(BF16), wba_ref[...])
    mix = ylg_ref[...] + gatt_ref[...] * y_att
    x1 = x_ref[...] + _dot(mix.astype(BF16), wo_ref[...])
    y_ref[...] = _ffn(x1, _rms_rows(x1, nf_ref[...]).astype(BF16), wfi_ref, wfo_ref)


def _dec_back_call(x, ylg, gatt, o, lw, layer):
    weights = [lw[n] for n in BACK_WEIGHTS]
    acts = (x, ylg, gatt, o)
    return pl.pallas_call(
        _dec_back_kernel,
        grid=(1,),
        in_specs=[_const_spec(a.shape) for a in acts] + [_layer_spec(w, layer) for w in weights],
        out_specs=pl.BlockSpec(x.shape, lambda i: (0, 0)),
        out_shape=jax.ShapeDtypeStruct(x.shape, F32),
        compiler_params=pltpu.CompilerParams(
            dimension_semantics=("arbitrary",), vmem_limit_bytes=VMEM_LIMIT_BYTES),
        name="decode_back",
    )(*acts, *weights)


def _prepare_weights(norm_mix, w_in, w_conv, b_conv, w_rg_a, b_rg_a, w_rg_x, b_rg_x, lru_lambda,
                     q_gain, k_gain, w_merge, b_merge, w_branch_lru, w_branch_att, w_o, norm_ffn,
                     w_ffn_in, w_ffn_out):
    depth = norm_mix.shape[0]
    row = lambda t: t.reshape(depth, 1, -1).astype(F32)
    per_tile = RG_TILE // LRU_BLOCK
    tiles = D_RNN // RG_TILE

    def block_diag(w):
        w = w.reshape(depth, tiles, per_tile, LRU_BLOCK, LRU_BLOCK)
        eye = jnp.eye(per_tile, dtype=w.dtype)
        return jnp.einsum("ltnij,nm->ltnimj", w, eye).reshape(depth, tiles, RG_TILE, RG_TILE)

    head = jnp.arange(MXU_DIM) // HEAD_DIM
    tile_gain = lambda t: jnp.tile(t.reshape(depth, 1, HEAD_DIM), (1, 1, HEADS_PER_GROUP)).astype(F32)
    return dict(
        norm_mix=row(norm_mix), w_in=w_in.astype(BF16), w_merge=w_merge.astype(BF16),
        b_merge=row(b_merge), w_conv=w_conv.astype(F32), b_conv=row(b_conv),
        w_rg=jnp.concatenate([block_diag(w_rg_a), block_diag(w_rg_x)], axis=3).astype(BF16),
        b_rg_a=row(b_rg_a), b_rg_x=row(b_rg_x), lam=row(lru_lambda),
        q_gain=tile_gain(q_gain), k_gain=tile_gain(k_gain),
        w_branch_lru=w_branch_lru.astype(BF16),
        pnorm=((head[:, None] == head[None, :]).astype(F32) / HEAD_DIM).astype(BF16),
        w_branch_att=w_branch_att.astype(BF16), w_o=w_o.astype(BF16), norm_ffn=row(norm_ffn),
        w_ffn_in=w_ffn_in.astype(BF16), w_ffn_out=w_ffn_out.astype(BF16))


def kernel(x_prompt, x_sample, cache_kv_g0, cache_kv_g1, cache_kv_g2, state_conv, state_h, rel_bias,
           norm_mix, w_in, w_conv, b_conv, w_rg_a, b_rg_a, w_rg_x, b_rg_x, lru_lambda, q_gain, k_gain,
           w_merge, b_merge, w_branch_lru, w_branch_att, w_o, norm_ffn, w_ffn_in, w_ffn_out):
    B, S, _ = x_prompt.shape
    nb = x_sample.shape[0]
    depth = norm_mix.shape[0]
    lw = _prepare_weights(norm_mix, w_in, w_conv, b_conv, w_rg_a, b_rg_a, w_rg_x, b_rg_x, lru_lambda,
                          q_gain, k_gain, w_merge, b_merge, w_branch_lru, w_branch_att, w_o, norm_ffn,
                          w_ffn_in, w_ffn_out)
    rel_bias = rel_bias.astype(F32)
    caches_t = [jnp.transpose(c.astype(F32), (0, 1, 3, 4, 5, 2))
                for c in (cache_kv_g0, cache_kv_g1, cache_kv_g2)]

    yp = x_prompt
    ys = x_sample.reshape(nb, D_MODEL)
    kvp = [[] for _ in range(N_GROUPS)]
    kvs = [[] for _ in range(N_GROUPS)]
    conv_p, h_p, conv_s, h_s = [], [], [], []
    for l in range(depth):
        sc = state_conv[l].astype(F32)
        conv_rows = [sc[:, r, :] for r in range(CONV_WIDTH - 1)]
        qs, ks, vs, ylg_s, gatt_s, lrux_s, hs_s = _dec_front_call(ys, conv_rows, state_h[l].astype(F32), lw, l)

        (q0, q1, q2, k0, k1, k2, v0, v1, v2, ylg, gatt, kv0, kv1, kv2, ctail, hfin) = _front_call(
            yp, lw, l, TM_PROMPT)
        os_, lses = [], []
        for g, (qg, kg, vg) in enumerate(((q0, k0, v0), (q1, k1, v1), (q2, k2, v2))):
            tab = rel_bias[:, g * HEADS_PER_GROUP:(g + 1) * HEADS_PER_GROUP]
            o, lse = _attn_call(qg, kg, vg, tab, DILATED_GROUPS[g][1])
            os_.append(o)
            lses.append(lse)
        yp, o_t = _back_call(yp, ylg, gatt, os_, lses, lw, l, TM_PROMPT, qs.T, ks.T, vs.T, caches_t, rel_bias)
        for g, kv in enumerate((kv0, kv1, kv2)):
            kvp[g].append(kv.reshape(B, kv.shape[1], 2, HEADS_PER_GROUP, HEAD_DIM))
        conv_p.append(ctail[:, SUBLANES - (CONV_WIDTH - 1):, :])
        h_p.append(hfin[:, 0, :])

        ys = _dec_back_call(ys, ylg_s, gatt_s, o_t.T, lw, l)
        for g in range(N_GROUPS):
            sl = slice(g * GROUP_WIDTH, (g + 1) * GROUP_WIDTH)
            kvs[g].append(jnp.stack([ks[:, sl], vs[:, sl]], axis=1).reshape(
                nb, 1, 2, HEADS_PER_GROUP, HEAD_DIM))
        conv_s.append(jnp.stack(conv_rows[1:] + [lrux_s], axis=1))
        h_s.append(hs_s)

    stack = jnp.stack
    return (yp, ys.reshape(nb, 1, D_MODEL),
            stack(kvp[0]), stack(kvp[1]), stack(kvp[2]), stack(conv_p), stack(h_p),
            stack(kvs[0]), stack(kvs[1]), stack(kvs[2]), stack(conv_s), stack(h_s))
```

```python
import functools
import math

import numpy as np
import jax
import jax.numpy as jnp
from jax import lax
from jax.experimental import pallas as pl
from jax.experimental.pallas import tpu as pltpu

D_MODEL = 1024
D_RNN = D_MODEL
N_LRU_BLOCKS = 16
LRU_BLOCK = D_RNN // N_LRU_BLOCKS
CONV_WIDTH = 4
LRU_C = 8.0
HEAD_DIM = 64
HEADS_PER_GROUP = 8
DILATED_GROUPS = ((128, 1), (512, 4), (2048, 16))
N_GROUPS = len(DILATED_GROUPS)
N_ATT_HEADS = N_GROUPS * HEADS_PER_GROUP
ATT_WIDTH = N_ATT_HEADS * HEAD_DIM
GROUP_WIDTH = HEADS_PER_GROUP * HEAD_DIM
BAND_BLOCK = 128
NUM_BUCKETS = 32
MAX_DISTANCE = 2048
D_FF = 2816
EPS = 1e-6

F32 = jnp.float32
BF16 = jnp.bfloat16

MXU_DIM = 256
SUBLANES = 8
LANES = 128
VMEM_LIMIT_BYTES = 56 * 1024 * 1024

TM_PROMPT = 256
ATTN_QBLOCKS = 16
RG_TILE = MXU_DIM
FFN_CHUNKS = ((0, 768), (768, 768), (1536, 768), (2304, 512))


def _dot(a, b):
    return jnp.dot(a, b, preferred_element_type=F32)


def _dot_nt(a, b):
    return lax.dot_general(a, b, (((1,), (1,)), ((), ())), preferred_element_type=F32)


def _sigmoid(x):
    return 0.5 * (jnp.tanh(0.5 * x) + 1.0)


def _gelu_tanh(x):
    c = math.sqrt(2.0 / math.pi)
    return 0.5 * x * (1.0 + jnp.tanh(c * (x + 0.044715 * (x * x * x))))


def _softplus(z):
    return jnp.maximum(z, 0.0) + jnp.log1p(jnp.exp(-jnp.abs(z)))


def _rms_rows(x, g):
    y = x * lax.rsqrt(jnp.mean(x * x, axis=-1, keepdims=True) + EPS)
    return y * g


def _head_rms(t, pn, gain):
    t2 = (t * t).astype(BF16)
    tiles = [_dot(t2[:, c * MXU_DIM:(c + 1) * MXU_DIM], pn) for c in range(t.shape[1] // MXU_DIM)]
    ms = tiles[0] if len(tiles) == 1 else jnp.concatenate(tiles, axis=1)
    return (t * lax.rsqrt(ms + EPS)) * gain


def _lru_gates(g, b_a, b_x, sp):
    w = g.shape[1] // 2
    r = _sigmoid(g[:, :w] + b_a)
    ig = _sigmoid(g[:, w:] + b_x)
    log_a = (-LRU_C * r) * sp
    a = jnp.exp(log_a)
    mult = jnp.sqrt(-jnp.tanh(log_a) * (a * a + 1.0))
    return a, mult, ig


def _scan_rows(a, u, h_in):
    rows, c = a.shape
    groups = rows // SUBLANES
    a3 = a.reshape(groups, SUBLANES, c)
    u3 = u.reshape(groups, SUBLANES, c)
    row = lax.broadcasted_iota(jnp.int32, (groups, SUBLANES, c), 1)
    shift = 1
    while shift < SUBLANES:
        ok = row >= shift
        a_sh = jnp.where(ok, pltpu.roll(a3, shift, axis=1), 1.0)
        u_sh = jnp.where(ok, pltpu.roll(u3, shift, axis=1), 0.0)
        u3 = u3 + a3 * u_sh
        a3 = a3 * a_sh
        shift *= 2
    out = []
    h = h_in
    for g in range(groups):
        hg = a3[g] * h + u3[g]
        out.append(hg)
        h = hg[SUBLANES - 1:SUBLANES, :]
    return jnp.concatenate(out, axis=0), h


def _to_slabs(slab_ref, t, first_slab=0):
    for c in range(t.shape[1] // LANES):
        slab_ref[first_slab + c] = t[:, c * LANES:(c + 1) * LANES]


def _from_slabs(slab_ref, first_slab=0, n_slabs=None):
    n_slabs = slab_ref.shape[0] - first_slab if n_slabs is None else n_slabs
    return jnp.concatenate([slab_ref[first_slab + c] for c in range(n_slabs)], axis=1)


def _gather_residues(slab_ref, dil, first_slab=0, n_slabs=None):
    n_slabs = slab_ref.shape[0] - first_slab if n_slabs is None else n_slabs
    per = slab_ref.shape[1] // dil
    cols = []
    for c in range(first_slab, first_slab + n_slabs):
        cols.append(jnp.concatenate(
            [slab_ref[c, pl.ds(r, per, stride=dil), :] for r in range(dil)], axis=0))
    return jnp.concatenate(cols, axis=1)


def _scatter_residues(slab_ref, t, dil, first_slab=0):
    per = slab_ref.shape[1] // dil
    for c in range(t.shape[1] // LANES):
        for r in range(dil):
            slab_ref[first_slab + c, pl.ds(r, per, stride=dil), :] = (
                t[r * per:(r + 1) * per, c * LANES:(c + 1) * LANES])


def _ffn(x1, xb, wfi_ref, wfo_ref):
    acc = x1
    for start, width in FFN_CHUNKS:
        gate = _dot(xb, wfi_ref[:, start:start + width])
        up = _dot(xb, wfi_ref[:, D_FF + start:D_FF + start + width])
        hid = (gate * _sigmoid(gate)) * up
        acc = acc + _dot(hid.astype(BF16), wfo_ref[start:start + width, :])
    return acc


def _layer_spec(w, layer):
    tail = (0,) * (w.ndim - 1)
    return pl.BlockSpec((None,) + w.shape[1:], lambda *_: (layer,) + tail, pipeline_mode=pl.Buffered(1))


def _const_spec(shape):
    zeros = (0,) * len(shape)
    return pl.BlockSpec(shape, lambda *_: zeros, pipeline_mode=pl.Buffered(1))


FRONT_WEIGHTS = ("norm_mix", "w_in", "w_merge", "b_merge", "w_conv", "b_conv", "w_rg", "b_rg_a", "b_rg_x",
                 "lam", "q_gain", "k_gain", "w_branch_lru")
BACK_WEIGHTS = ("w_branch_att", "w_o", "norm_ffn", "w_ffn_in", "w_ffn_out")


def _front_kernel(x_ref, nm_ref, win_ref, wmg_ref, bmg_ref, wconv_ref, bconv_ref, wrg_ref,
                  brga_ref, brgx_ref, lam_ref, qg_ref, kg_ref, wbl_ref, pn_ref,
                  q0_ref, q1_ref, q2_ref, k0_ref, k1_ref, k2_ref, v0_ref, v1_ref, v2_ref,
                  ylg_ref, gatt_ref, kvp0_ref, kvp1_ref, kvp2_ref, convp_ref, hp_ref,
                  xn_ref, kv_ref, ext_ref, h_ref, xb_ref, xg_ref, xc_ref, xcb_ref, gl_ref, y_ref, *, tm):
    i = pl.program_id(1)
    n_tiles = D_RNN // RG_TILE

    @pl.when(i == 0)
    def _():
        ext_ref[0:SUBLANES, :] = jnp.zeros((SUBLANES, D_RNN), F32)
        h_ref[...] = jnp.zeros((SUBLANES, D_RNN), F32)

    xn = _rms_rows(x_ref[...], nm_ref[...])
    _to_slabs(xn_ref, xn)
    xb_ref[...] = xn.astype(BF16)

    for t in range(n_tiles):
        sl = slice(t * RG_TILE, (t + 1) * RG_TILE)
        lru_x = _dot(xb_ref[...], win_ref[:, sl])
        ext_ref[SUBLANES:SUBLANES + tm, sl] = lru_x
        xc = bconv_ref[:, sl] + wconv_ref[0:1, sl] * ext_ref[SUBLANES - 3:SUBLANES - 3 + tm, sl]
        xc = xc + wconv_ref[1:2, sl] * ext_ref[SUBLANES - 2:SUBLANES - 2 + tm, sl]
        xc = xc + wconv_ref[2:3, sl] * ext_ref[SUBLANES - 1:SUBLANES - 1 + tm, sl]
        xc = xc + wconv_ref[3:4, sl] * lru_x
        xc_ref[:, sl] = xc
        xcb_ref[:, sl] = xc.astype(BF16)
        gl_ref[:, sl] = _gelu_tanh(_dot(xb_ref[...], win_ref[:, D_RNN + t * RG_TILE:D_RNN + (t + 1) * RG_TILE]))
    tail = ext_ref[tm:tm + SUBLANES, :]
    ext_ref[0:SUBLANES, :] = tail
    convp_ref[...] = tail

    sp = _softplus(-lam_ref[...])
    first_row = (lax.broadcasted_iota(jnp.int32, (tm, RG_TILE), 0) == 0) & (i == 0)
    gates = {}

    def lru_gates(t):
        sl = slice(t * RG_TILE, (t + 1) * RG_TILE)
        g = _dot(xcb_ref[:, sl], wrg_ref[t])
        a, mult, ig = _lru_gates(g, brga_ref[:, sl], brgx_ref[:, sl], sp[:, sl])
        mult = jnp.where(first_row, 1.0, mult)
        gates[t] = (a, (mult * ig) * xc_ref[:, sl])

    def lru_scan(t):
        sl = slice(t * RG_TILE, (t + 1) * RG_TILE)
        a, u = gates.pop(t)
        hs, h_last = _scan_rows(a, u, h_ref[0:1, sl])
        h_ref[:, sl] = jnp.broadcast_to(h_last, (SUBLANES, RG_TILE))
        y_ref[:, sl] = (hs * gl_ref[:, sl]).astype(BF16)

    pn = pn_ref[...]
    q_refs = (q0_ref, q1_ref, q2_ref)
    k_refs = (k0_ref, k1_ref, k2_ref)
    v_refs = (v0_ref, v1_ref, v2_ref)
    kvp_refs = (kvp0_ref, kvp1_ref, kvp2_ref)

    def attn_operands(g, half):
        window, dil = DILATED_GROUPS[g]
        per = tm // dil
        keep = min(window, tm)
        if dil == 1:
            src = xb_ref
        else:
            if half == 0:
                xg_ref[...] = _gather_residues(xn_ref, dil).astype(BF16)
            src = xg_ref
        c = 2 * D_RNN + g * GROUP_WIDTH + half * MXU_DIM
        hl = slice(half * MXU_DIM, (half + 1) * MXU_DIM)
        vl = slice(GROUP_WIDTH + half * MXU_DIM, GROUP_WIDTH + (half + 1) * MXU_DIM)
        q = _dot(src[...], win_ref[:, c:c + MXU_DIM])
        k = _dot(src[...], win_ref[:, c + ATT_WIDTH:c + ATT_WIDTH + MXU_DIM])
        v = _dot(src[...], win_ref[:, c + 2 * ATT_WIDTH:c + 2 * ATT_WIDTH + MXU_DIM])
        qn = (_head_rms(q, pn, qg_ref[:, hl]) * (HEAD_DIM ** -0.5)).astype(BF16)
        kn = _head_rms(k, pn, kg_ref[:, hl])
        knb = kn.astype(BF16)
        vb = v.astype(BF16)
        for r in range(dil):
            rows = slice(r * per, (r + 1) * per)
            q_refs[g][r, :, hl] = qn[rows]
            k_refs[g][r, :, hl] = knb[rows]
            v_refs[g][r, :, hl] = vb[rows]
        if dil == 1:
            kvp_refs[g][:, hl] = kn[tm - keep:, :]
            kvp_refs[g][:, vl] = v[tm - keep:, :]
        else:
            _scatter_residues(kv_ref, kn, dil, first_slab=hl.start // LANES)
            _scatter_residues(kv_ref, v, dil, first_slab=vl.start // LANES)
            if half == GROUP_WIDTH // MXU_DIM - 1:
                kvp_refs[g][...] = _from_slabs(kv_ref)

    def att_gate(t):
        sl = slice(t * RG_TILE, (t + 1) * RG_TILE)
        al = slice(D_MODEL + t * RG_TILE, D_MODEL + (t + 1) * RG_TILE)
        gatt_ref[:, sl] = _sigmoid(_dot(xb_ref[...], wmg_ref[:, al]) + bmg_ref[:, al]).astype(gatt_ref.dtype)

    for t in range(n_tiles):
        lru_gates(t)
        if t < N_GROUPS:
            attn_operands(t, 0)
        else:
            att_gate(0)
            att_gate(1)
        lru_scan(t)
        if t < N_GROUPS:
            attn_operands(t, 1)
        else:
            att_gate(2)
            att_gate(3)
    hp_ref[...] = h_ref[...]
    for t in range(n_tiles):
        sl = slice(t * RG_TILE, (t + 1) * RG_TILE)
        g_lru = _sigmoid(_dot(xb_ref[...], wmg_ref[:, sl]) + bmg_ref[:, sl])
        ylg_ref[:, sl] = (g_lru * _dot(y_ref[...], wbl_ref[:, sl])).astype(ylg_ref.dtype)


def _front_call(x, lw, layer, tm):
    B, S, _ = x.shape
    nt = S // tm
    row_spec = lambda width: pl.BlockSpec((None, tm, width), lambda b, i: (b, i, 0))
    weights = [lw[n] for n in FRONT_WEIGHTS]
    in_specs = ([row_spec(D_MODEL)] + [_layer_spec(w, layer) for w in weights]
                + [_const_spec(lw["pnorm"].shape)])

    out_shape, out_specs = [], []
    for _ in range(3):
        for _, dil in DILATED_GROUPS:
            out_shape.append(jax.ShapeDtypeStruct((B, dil, S // dil, GROUP_WIDTH), BF16))
            out_specs.append(pl.BlockSpec((None, dil, tm // dil, GROUP_WIDTH), lambda b, i: (b, 0, i, 0)))
    out_shape.append(jax.ShapeDtypeStruct((B, S, D_MODEL), BF16))
    out_specs.append(row_spec(D_MODEL))
    out_shape.append(jax.ShapeDtypeStruct((B, S, D_MODEL), BF16))
    out_specs.append(row_spec(D_MODEL))
    for window, _ in DILATED_GROUPS:
        keep = min(window, tm)
        first = nt - window // keep
        out_shape.append(jax.ShapeDtypeStruct((B, window, 2 * GROUP_WIDTH), F32))
        out_specs.append(pl.BlockSpec(
            (None, keep, 2 * GROUP_WIDTH),
            functools.partial(lambda b, i, first: (b, jnp.maximum(i - first, 0), 0), first=first)))
    for _ in range(2):
        out_shape.append(jax.ShapeDtypeStruct((B, SUBLANES, D_RNN), F32))
        out_specs.append(pl.BlockSpec((None, SUBLANES, D_RNN), lambda b, i: (b, 0, 0)))

    return pl.pallas_call(
        functools.partial(_front_kernel, tm=tm),
        grid=(B, nt),
        in_specs=in_specs,
        out_specs=out_specs,
        out_shape=out_shape,
        scratch_shapes=[pltpu.VMEM((D_MODEL // LANES, tm, LANES), F32),
                        pltpu.VMEM((2 * GROUP_WIDTH // LANES, tm, LANES), F32),
                        pltpu.VMEM((tm + SUBLANES, D_RNN), F32), pltpu.VMEM((SUBLANES, D_RNN), F32),
                        pltpu.VMEM((tm, D_MODEL), BF16), pltpu.VMEM((tm, D_MODEL), BF16),
                        pltpu.VMEM((tm, D_RNN), F32), pltpu.VMEM((tm, D_RNN), BF16),
                        pltpu.VMEM((tm, D_RNN), F32), pltpu.VMEM((tm, D_RNN), BF16)],
        compiler_params=pltpu.CompilerParams(
            dimension_semantics=("arbitrary", "arbitrary"), vmem_limit_bytes=VMEM_LIMIT_BYTES),
        name="prompt_front",
    )(x, *weights, lw["pnorm"])


def _attn_kernel(tab_ref, bkt_ref, q_ref, kp_ref, kc_ref, vp_ref, vc_ref, o_ref, lse_ref,
                 bias_ref, kbuf_ref, vbuf_ref, *, qblocks):
    b, r, j = pl.program_id(0), pl.program_id(1), pl.program_id(2)
    BB = BAND_BLOCK
    PAIR = 2 * HEAD_DIM

    @pl.when((b == 0) & (r == 0) & (j == 0))
    def _():
        bkt = bkt_ref[...]
        for h in range(HEADS_PER_GROUP):
            acc = jnp.full(bkt.shape, -jnp.inf, F32)
            for n in range(NUM_BUCKETS):
                acc = jnp.where(bkt == n, tab_ref[n, h], acc)
            bias_ref[h // 2, (h % 2) * BB:(h % 2 + 1) * BB, :] = acc

    first_col = lax.broadcasted_iota(jnp.int32, (1, 2 * BB), 1) < BB
    pen = jnp.where(first_col & (j == 0), -jnp.inf, 0.0).astype(F32)
    low_half = lax.broadcasted_iota(jnp.int32, (BB, PAIR), 1) < HEAD_DIM
    ones = jnp.ones((2 * BB, PAIR), BF16)
    for rr in range(q_ref.shape[0]):
        kbuf_ref[0:BB, :] = kp_ref[rr]
        kbuf_ref[BB:, :] = kc_ref[rr]
        vbuf_ref[0:BB, :] = vp_ref[rr]
        vbuf_ref[BB:, :] = vc_ref[rr]
        for t in range(qblocks):
            rows = slice(t * BB, (t + 1) * BB)
            keys = slice(t * BB, (t + 2) * BB)
            for hp in range(HEADS_PER_GROUP // 2):
                sl = slice(hp * PAIR, (hp + 1) * PAIR)
                qp = q_ref[rr, rows, sl]
                zero = jnp.zeros_like(qp)
                lhs = jnp.concatenate([jnp.where(low_half, qp, zero), jnp.where(low_half, zero, qp)], axis=0)
                s = _dot_nt(lhs, kbuf_ref[keys, sl]) + bias_ref[hp]
                if t == 0:
                    s = s + pen
                m = jnp.max(s, axis=-1, keepdims=True)
                p = jnp.exp((s - m).astype(BF16))
                oe = _dot(p, jnp.concatenate([vbuf_ref[keys, sl], ones], axis=1))
                l = oe[:, PAIR:]
                o = oe[:, :PAIR] / l
                lse = m + jnp.log(l)
                for e in range(2):
                    hl = slice(hp * PAIR + e * HEAD_DIM, hp * PAIR + (e + 1) * HEAD_DIM)
                    o_ref[rr, rows, hl] = o[e * BB:(e + 1) * BB, e * HEAD_DIM:(e + 1) * HEAD_DIM]
                    lse_ref[rr, rows, hl] = lse[e * BB:(e + 1) * BB, e * HEAD_DIM:(e + 1) * HEAD_DIM]


def _t5_bucket(dist):
    max_exact = NUM_BUCKETS // 2
    d_f = np.maximum(dist, 1).astype(np.float32)
    large = max_exact + (np.log(d_f / np.float32(max_exact)) / np.float32(math.log(MAX_DISTANCE / max_exact))
                         * np.float32(NUM_BUCKETS - max_exact)).astype(np.int32)
    large = np.minimum(large, NUM_BUCKETS - 1)
    return np.where(dist < max_exact, dist, large).astype(np.int32)


def _band_buckets(dil):
    BB = BAND_BLOCK
    qi = np.arange(BB)[:, None]
    ki = np.arange(2 * BB)[None, :]
    sub = qi + BB - ki
    in_band = (sub >= 0) & (sub <= BB)
    return jnp.asarray(np.where(in_band, _t5_bucket(np.clip(sub, 0, BB) * dil), -1).astype(np.int32))


def _attn_call(q, k, v, tab, dil):
    B, _, n, C = q.shape
    BB = BAND_BLOCK
    qb = min(ATTN_QBLOCKS, n // BB)
    res = ATTN_QBLOCKS // qb
    cur = pl.BlockSpec((None, res, qb * BB, C), lambda b, r, j: (b, r, j, 0))
    prev = pl.BlockSpec((None, res, BB, C), lambda b, r, j: (b, r, jnp.maximum(j * qb - 1, 0), 0))
    return pl.pallas_call(
        functools.partial(_attn_kernel, qblocks=qb),
        grid=(B, dil // res, n // (qb * BB)),
        in_specs=[pl.BlockSpec(memory_space=pltpu.SMEM),
                  pl.BlockSpec((BB, 2 * BB), lambda b, r, j: (0, 0)),
                  cur, prev, cur, prev, cur],
        out_specs=[cur, cur],
        out_shape=[jax.ShapeDtypeStruct(q.shape, F32)] * 2,
        scratch_shapes=[pltpu.VMEM((HEADS_PER_GROUP // 2, 2 * BB, 2 * BB), F32),
                        pltpu.VMEM(((qb + 1) * BB, C), BF16), pltpu.VMEM(((qb + 1) * BB, C), BF16)],
        compiler_params=pltpu.CompilerParams(
            dimension_semantics=("arbitrary", "arbitrary", "arbitrary"), vmem_limit_bytes=VMEM_LIMIT_BYTES),
        name="prompt_attn_d%d" % dil,
    )(tab, _band_buckets(dil), q, k, k, v, v)


def _back_kernel(x_ref, ylg_ref, gatt_ref, o0_ref, o1_ref, o2_ref, l0_ref, l1_ref, l2_ref,
                 wba_ref, wo_ref, nf_ref, wfi_ref, wfo_ref,
                 qt_ref, kt_ref, vt_ref, c0_ref, c1_ref, c2_ref, tabt_ref, bkt0_ref, bkt1_ref, bkt2_ref,
                 y_ref, ot_ref,
                 so1_ref, so2_ref, sl1_ref, sl2_ref, bias0_ref, bias1_ref, bias2_ref, s0_ref, s1_ref, s2_ref,
                 *, tm, dec_parts):
    step = pl.program_id(0) * pl.num_programs(1) + pl.program_id(1)
    bias_refs = (bias0_ref, bias1_ref, bias2_ref)
    s_refs = (s0_ref, s1_ref, s2_ref)

    @pl.when(step == 0)
    def _():
        _decode_attention_init(tabt_ref, (bkt0_ref, bkt1_ref, bkt2_ref), bias_refs, s_refs, ot_ref)

    for (_, dil), src, dst in ((DILATED_GROUPS[1], o1_ref, so1_ref), (DILATED_GROUPS[2], o2_ref, so2_ref),
                               (DILATED_GROUPS[1], l1_ref, sl1_ref), (DILATED_GROUPS[2], l2_ref, sl2_ref)):
        _scatter_residues(dst, src[...].reshape(tm, GROUP_WIDTH), dil)
    os_ = (o0_ref[0], _from_slabs(so1_ref), _from_slabs(so2_ref))
    lses = (l0_ref[0], _from_slabs(sl1_ref), _from_slabs(sl2_ref))

    top = jnp.maximum(jnp.maximum(lses[0], lses[1]), lses[2])
    num = 0.0
    den = 0.0
    for o, lse in zip(os_, lses):
        z = jnp.exp(lse - top)
        num = num + z * o
        den = den + z
    o = num / den

    y_att = _dot(o.astype(BF16), wba_ref[...])
    mix = ylg_ref[...] + gatt_ref[...] * y_att
    x1 = x_ref[...] + _dot(mix.astype(BF16), wo_ref[...])
    y_ref[...] = _ffn(x1, _rms_rows(x1, nf_ref[...]).astype(BF16), wfi_ref, wfo_ref)

    _decode_attention_step(step // dec_parts, step % dec_parts, HEADS_PER_GROUP // dec_parts,
                           qt_ref, kt_ref, vt_ref, (c0_ref, c1_ref, c2_ref), tabt_ref, bias_refs, s_refs, ot_ref)


def _back_call(x, ylg, gatt, os_, lses, lw, layer, tm, qt, kt, vt, caches_t, rel_bias):
    B, S, _ = x.shape
    nt = S // tm
    nb = qt.shape[1]
    dec_parts = (B * nt) // nb
    assert dec_parts * nb == B * nt and HEADS_PER_GROUP % dec_parts == 0
    n_heads = HEADS_PER_GROUP // dec_parts
    tabt, bkts = _decode_attention_operands(rel_bias, dec_parts)

    row_spec = lambda width: pl.BlockSpec((None, tm, width), lambda b, i: (b, i, 0))
    res_specs = [pl.BlockSpec((None, dil, tm // dil, GROUP_WIDTH), lambda b, i: (b, 0, i, 0))
                 for _, dil in DILATED_GROUPS]
    cache_spec = lambda c: pl.BlockSpec(
        (None, None, 2, n_heads) + c.shape[4:],
        lambda b, i: (layer, (b * nt + i) // dec_parts, 0, (b * nt + i) % dec_parts, 0, 0))
    weights = [lw[n] for n in BACK_WEIGHTS]
    in_specs = ([row_spec(D_MODEL)] * 3 + res_specs * 2 + [_layer_spec(w, layer) for w in weights]
                + [_const_spec(qt.shape)] * 3 + [cache_spec(c) for c in caches_t]
                + [_const_spec(tabt.shape)] + [_const_spec(b.shape) for b in bkts])
    return pl.pallas_call(
        functools.partial(_back_kernel, tm=tm, dec_parts=dec_parts),
        grid=(B, nt),
        in_specs=in_specs,
        out_specs=[row_spec(D_MODEL), pl.BlockSpec((GROUP_WIDTH, nb), lambda b, i: (0, 0))],
        out_shape=[jax.ShapeDtypeStruct((B, S, D_MODEL), F32), jax.ShapeDtypeStruct((GROUP_WIDTH, nb), F32)],
        scratch_shapes=[pltpu.VMEM((GROUP_WIDTH // LANES, tm, LANES), F32)] * 4
        + [pltpu.VMEM((dec_parts, SUBLANES, window), F32) for window, _ in DILATED_GROUPS]
        + [pltpu.VMEM((SUBLANES, window), F32) for window, _ in DILATED_GROUPS],
        compiler_params=pltpu.CompilerParams(
            dimension_semantics=("arbitrary", "arbitrary"), vmem_limit_bytes=VMEM_LIMIT_BYTES),
        name="prompt_back",
    )(x, ylg, gatt, *os_, *lses, *weights, qt, kt, vt, *caches_t, tabt, *bkts)


def _dec_front_kernel(x_ref, c0_ref, c1_ref, c2_ref, h0_ref, nm_ref, win_ref, wmg_ref, bmg_ref,
                      wconv_ref, bconv_ref, wrg_ref, brga_ref, brgx_ref, lam_ref, qg_ref, kg_ref,
                      wbl_ref, pn_ref,
                      q_ref, k_ref, v_ref, ylg_ref, gatt_ref, lrux_ref, hs_ref):
    xb = _rms_rows(x_ref[...], nm_ref[...]).astype(BF16)
    lru_x = _dot(xb, win_ref[:, 0:D_RNN])
    lrux_ref[...] = lru_x
    xc = bconv_ref[...] + wconv_ref[0:1, :] * c0_ref[...]
    xc = xc + wconv_ref[1:2, :] * c1_ref[...]
    xc = xc + wconv_ref[2:3, :] * c2_ref[...]
    xc = xc + wconv_ref[3:4, :] * lru_x

    lru_g = _dot(xb, win_ref[:, D_RNN:2 * D_RNN])
    xcb = xc.astype(BF16)
    sp = _softplus(-lam_ref[...])
    ys = []
    for t in range(D_RNN // RG_TILE):
        sl = slice(t * RG_TILE, (t + 1) * RG_TILE)
        g = _dot(xcb[:, sl], wrg_ref[t])
        a, mult, ig = _lru_gates(g, brga_ref[:, sl], brgx_ref[:, sl], sp[:, sl])
        h = a * h0_ref[:, sl] + (mult * ig) * xc[:, sl]
        hs_ref[:, sl] = h
        ys.append((h * _gelu_tanh(lru_g[:, sl])).astype(BF16))
    y_lru = _dot(jnp.concatenate(ys, axis=1), wbl_ref[...])
    g_lru = _sigmoid(_dot(xb, wmg_ref[:, 0:D_MODEL]) + bmg_ref[:, 0:D_MODEL])
    ylg_ref[...] = g_lru * y_lru
    gatt_ref[...] = _sigmoid(_dot(xb, wmg_ref[:, D_MODEL:2 * D_MODEL]) + bmg_ref[:, D_MODEL:2 * D_MODEL])

    pn = pn_ref[...]
    for g in range(N_GROUPS):
        c = 2 * D_RNN + g * GROUP_WIDTH
        sl = slice(g * GROUP_WIDTH, (g + 1) * GROUP_WIDTH)
        q = _dot(xb, win_ref[:, c:c + GROUP_WIDTH])
        k = _dot(xb, win_ref[:, c + ATT_WIDTH:c + ATT_WIDTH + GROUP_WIDTH])
        v = _dot(xb, win_ref[:, c + 2 * ATT_WIDTH:c + 2 * ATT_WIDTH + GROUP_WIDTH])
        q_ref[:, sl] = _head_rms(q, pn, qg_ref[...]) * (HEAD_DIM ** -0.5)
        k_ref[:, sl] = _head_rms(k, pn, kg_ref[...])
        v_ref[:, sl] = v


def _dec_front_call(x, conv_rows, h0, lw, layer):
    nb = x.shape[0]
    weights = [lw[n] for n in FRONT_WEIGHTS]
    acts = (x,) + tuple(conv_rows) + (h0,)
    widths = (ATT_WIDTH, ATT_WIDTH, ATT_WIDTH, D_MODEL, D_MODEL, D_RNN, D_RNN)
    return pl.pallas_call(
        _dec_front_kernel,
        grid=(1,),
        in_specs=([_const_spec(a.shape) for a in acts] + [_layer_spec(w, layer) for w in weights]
                  + [_const_spec(lw["pnorm"].shape)]),
        out_specs=[pl.BlockSpec((nb, w), lambda i: (0, 0)) for w in widths],
        out_shape=[jax.ShapeDtypeStruct((nb, w), F32) for w in widths],
        compiler_params=pltpu.CompilerParams(
            dimension_semantics=("arbitrary",), vmem_limit_bytes=VMEM_LIMIT_BYTES),
        name="decode_front",
    )(*acts, *weights, lw["pnorm"])


def _decode_attention_init(tabt_ref, bkt_refs, bias_refs, s_refs, ot_ref):
    ot_ref[...] = jnp.zeros(ot_ref.shape, F32)
    for g in range(N_GROUPS):
        s_refs[g][...] = jnp.zeros(s_refs[g].shape, F32)
        bkt = bkt_refs[g][...]
        for part in range(bias_refs[g].shape[0]):
            acc = jnp.full(bkt.shape, -jnp.inf, F32)
            for n in range(NUM_BUCKETS):
                acc = jnp.where(bkt == n, tabt_ref[g, part, :, n:n + 1], acc)
            bias_refs[g][part] = acc


def _decode_attention_step(seq, part, n_heads, qt_ref, kt_ref, vt_ref, c_refs, tabt_ref, bias_refs, s_refs,
                           ot_ref):
    mine = lax.broadcasted_iota(jnp.int32, (HEAD_DIM, qt_ref.shape[1]), 1) == seq

    def column(ref, g, h):
        row0 = pl.multiple_of(g * GROUP_WIDTH + (part * n_heads + h) * HEAD_DIM, HEAD_DIM)
        return jnp.sum(jnp.where(mine, ref[pl.ds(row0, HEAD_DIM), :], 0.0), axis=1, keepdims=True)

    def fold_lanes(t, op):
        out = t[:, 0:LANES]
        for c in range(1, t.shape[1] // LANES):
            out = op(out, t[:, c * LANES:(c + 1) * LANES])
        return out

    pad = [jnp.zeros((SUBLANES - n_heads, 1), F32)] if n_heads < SUBLANES else []
    s0, vcols = [], []
    for g in range(N_GROUPS):
        rows = []
        for h in range(n_heads):
            qh = column(qt_ref, g, h)
            s_refs[g][h:h + 1, :] = jnp.sum(c_refs[g][0, h] * qh, axis=0, keepdims=True)
            rows.append(jnp.sum(qh * column(kt_ref, g, h), axis=0, keepdims=True))
            vcols.append(column(vt_ref, g, h))
        s0.append(jnp.concatenate(rows + pad, axis=0) + tabt_ref[g, part, :, 0:1])

    ss = [s_refs[g][...] + bias_refs[g][part] for g in range(N_GROUPS)]
    m_max = jnp.maximum(jnp.maximum(s0[0], s0[1]), s0[2])
    for s in ss:
        m_max = jnp.maximum(m_max, jnp.max(fold_lanes(s, jnp.maximum), axis=1, keepdims=True))
    p0 = [jnp.exp(s - m_max) for s in s0]
    den = p0[0] + p0[1] + p0[2]
    for g, s in enumerate(ss):
        p = jnp.exp(s - m_max)
        s_refs[g][...] = p
        den = den + jnp.sum(fold_lanes(p, jnp.add), axis=1, keepdims=True)
    inv_den = 1.0 / den

    for h in range(n_heads):
        acc = None
        new = 0.0
        for g in range(N_GROUPS):
            f = fold_lanes(c_refs[g][1, h] * s_refs[g][h:h + 1, :], jnp.add)
            acc = f if acc is None else acc + f
            new = new + p0[g][h:h + 1, :] * vcols[g * n_heads + h]
        num = jnp.sum(acc, axis=1, keepdims=True) + new
        row0 = pl.multiple_of((part * n_heads + h) * HEAD_DIM, HEAD_DIM)
        ot_ref[pl.ds(row0, HEAD_DIM), :] = jnp.where(mine, num * inv_den[h:h + 1, :],
                                                      ot_ref[pl.ds(row0, HEAD_DIM), :])


def _decode_attention_operands(rel_bias, parts):
    n_heads = HEADS_PER_GROUP // parts
    tab = rel_bias.reshape(NUM_BUCKETS, N_GROUPS, parts, n_heads).transpose(1, 2, 3, 0)
    tab = jnp.pad(tab, ((0, 0), (0, 0), (0, SUBLANES - n_heads), (0, 0)))
    bkts = []
    for window, dil in DILATED_GROUPS:
        w = np.arange(window)
        bkt = np.where(w % dil == 0, _t5_bucket(window - w), -1).astype(np.int32)
        bkts.append(jnp.asarray(np.broadcast_to(bkt[None, :], (SUBLANES, window))))
    return tab, bkts


def _dec_back_kernel(x_ref, ylg_ref, gatt_ref, o_ref, wba_ref, wo_ref, nf_ref, wfi_ref, wfo_ref, y_ref):
    y_att = _dot(o_ref[...].astype(BF16), wba_ref[...])
    mix = ylg_ref[...] + gatt_ref[...] * y_att
    x1 = x_ref[...] + _dot(mix.astype(BF16), wo_ref[...])
    y_ref[...] = _ffn(x1, _rms_rows(x1, nf_ref[...]).astype(BF16), wfi_ref, wfo_ref)


def _dec_back_call(x, ylg, gatt, o, lw, layer):
    weights = [lw[n] for n in BACK_WEIGHTS]
    acts = (x, ylg, gatt, o)
    return pl.pallas_call(
        _dec_back_kernel,
        grid=(1,),
        in_specs=[_const_spec(a.shape) for a in acts] + [_layer_spec(w, layer) for w in weights],
        out_specs=pl.BlockSpec(x.shape, lambda i: (0, 0)),
        out_shape=jax.ShapeDtypeStruct(x.shape, F32),
        compiler_params=pltpu.CompilerParams(
            dimension_semantics=("arbitrary",), vmem_limit_bytes=VMEM_LIMIT_BYTES),
        name="decode_back",
    )(*acts, *weights)


def _prepare_weights(norm_mix, w_in, w_conv, b_conv, w_rg_a, b_rg_a, w_rg_x, b_rg_x, lru_lambda,
                     q_gain, k_gain, w_merge, b_merge, w_branch_lru, w_branch_att, w_o, norm_ffn,
                     w_ffn_in, w_ffn_out):
    depth = norm_mix.shape[0]
    row = lambda t: t.reshape(depth, 1, -1).astype(F32)
    per_tile = RG_TILE // LRU_BLOCK
    tiles = D_RNN // RG_TILE

    def block_diag(w):
        w = w.reshape(depth, tiles, per_tile, LRU_BLOCK, LRU_BLOCK)
        eye = jnp.eye(per_tile, dtype=w.dtype)
        return jnp.einsum("ltnij,nm->ltnimj", w, eye).reshape(depth, tiles, RG_TILE, RG_TILE)

    head = jnp.arange(MXU_DIM) // HEAD_DIM
    tile_gain = lambda t: jnp.tile(t.reshape(depth, 1, HEAD_DIM), (1, 1, HEADS_PER_GROUP)).astype(F32)
    return dict(
        norm_mix=row(norm_mix), w_in=w_in.astype(BF16), w_merge=w_merge.astype(BF16),
        b_merge=row(b_merge), w_conv=w_conv.astype(F32), b_conv=row(b_conv),
        w_rg=jnp.concatenate([block_diag(w_rg_a), block_diag(w_rg_x)], axis=3).astype(BF16),
        b_rg_a=row(b_rg_a), b_rg_x=row(b_rg_x), lam=row(lru_lambda),
        q_gain=tile_gain(q_gain), k_gain=tile_gain(k_gain),
        w_branch_lru=w_branch_lru.astype(BF16),
        pnorm=((head[:, None] == head[None, :]).astype(F32) / HEAD_DIM).astype(BF16),
        w_branch_att=w_branch_att.astype(BF16), w_o=w_o.astype(BF16), norm_ffn=row(norm_ffn),
        w_ffn_in=w_ffn_in.astype(BF16), w_ffn_out=w_ffn_out.astype(BF16))


def kernel(x_prompt, x_sample, cache_kv_g0, cache_kv_g1, cache_kv_g2, state_conv, state_h, rel_bias,
           norm_mix, w_in, w_conv, b_conv, w_rg_a, b_rg_a, w_rg_x, b_rg_x, lru_lambda, q_gain, k_gain,
           w_merge, b_merge, w_branch_lru, w_branch_att, w_o, norm_ffn, w_ffn_in, w_ffn_out):
    B, S, _ = x_prompt.shape
    nb = x_sample.shape[0]
    depth = norm_mix.shape[0]
    lw = _prepare_weights(norm_mix, w_in, w_conv, b_conv, w_rg_a, b_rg_a, w_rg_x, b_rg_x, lru_lambda,
                          q_gain, k_gain, w_merge, b_merge, w_branch_lru, w_branch_att, w_o, norm_ffn,
                          w_ffn_in, w_ffn_out)
    rel_bias = rel_bias.astype(F32)
    caches_t = [jnp.transpose(c.astype(F32), (0, 1, 3, 4, 5, 2))
                for c in (cache_kv_g0, cache_kv_g1, cache_kv_g2)]

    yp = x_prompt
    ys = x_sample.reshape(nb, D_MODEL)
    kvp = [[] for _ in range(N_GROUPS)]
    kvs = [[] for _ in range(N_GROUPS)]
    conv_p, h_p, conv_s, h_s = [], [], [], []
    for l in range(depth):
        sc = state_conv[l].astype(F32)
        conv_rows = [sc[:, r, :] for r in range(CONV_WIDTH - 1)]
        qs, ks, vs, ylg_s, gatt_s, lrux_s, hs_s = _dec_front_call(ys, conv_rows, state_h[l].astype(F32), lw, l)

        (q0, q1, q2, k0, k1, k2, v0, v1, v2, ylg, gatt, kv0, kv1, kv2, ctail, hfin) = _front_call(
            yp, lw, l, TM_PROMPT)
        os_, lses = [], []
        for g, (qg, kg, vg) in enumerate(((q0, k0, v0), (q1, k1, v1), (q2, k2, v2))):
            tab = rel_bias[:, g * HEADS_PER_GROUP:(g + 1) * HEADS_PER_GROUP]
            o, lse = _attn_call(qg, kg, vg, tab, DILATED_GROUPS[g][1])
            os_.append(o)
            lses.append(lse)
        yp, o_t = _back_call(yp, ylg, gatt, os_, lses, lw, l, TM_PROMPT, qs.T, ks.T, vs.T, caches_t, rel_bias)
        for g, kv in enumerate((kv0, kv1, kv2)):
            kvp[g].append(kv.reshape(B, kv.shape[1], 2, HEADS_PER_GROUP, HEAD_DIM))
        conv_p.append(ctail[:, SUBLANES - (CONV_WIDTH - 1):, :])
        h_p.append(hfin[:, 0, :])

        ys = _dec_back_call(ys, ylg_s, gatt_s, o_t.T, lw, l)
        for g in range(N_GROUPS):
            sl = slice(g * GROUP_WIDTH, (g + 1) * GROUP_WIDTH)
            kvs[g].append(jnp.stack([ks[:, sl], vs[:, sl]], axis=1).reshape(
                nb, 1, 2, HEADS_PER_GROUP, HEAD_DIM))
        conv_s.append(jnp.stack(conv_rows[1:] + [lrux_s], axis=1))
        h_s.append(hs_s)

    stack = jnp.stack
    return (yp, ys.reshape(nb, 1, D_MODEL),
            stack(kvp[0]), stack(kvp[1]), stack(kvp[2]), stack(conv_p), stack(h_p),
            stack(kvs[0]), stack(kvs[1]), stack(kvs[2]), stack(conv_s), stack(h_s))
```

```python
import functools
import math

import numpy as np
import jax
import jax.numpy as jnp
from jax import lax
from jax.experimental import pallas as pl
from jax.experimental.pallas import tpu as pltpu

D_MODEL = 1024
D_RNN = D_MODEL
N_LRU_BLOCKS = 16
LRU_BLOCK = D_RNN // N_LRU_BLOCKS
CONV_WIDTH = 4
LRU_C = 8.0
HEAD_DIM = 64
HEADS_PER_GROUP = 8
DILATED_GROUPS = ((128, 1), (512, 4), (2048, 16))
N_GROUPS = len(DILATED_GROUPS)
N_ATT_HEADS = N_GROUPS * HEADS_PER_GROUP
ATT_WIDTH = N_ATT_HEADS * HEAD_DIM
GROUP_WIDTH = HEADS_PER_GROUP * HEAD_DIM
BAND_BLOCK = 128
NUM_BUCKETS = 32
MAX_DISTANCE = 2048
D_FF = 2816
EPS = 1e-6

F32 = jnp.float32
BF16 = jnp.bfloat16

MXU_DIM = 256
SUBLANES = 8
LANES = 128
VMEM_LIMIT_BYTES = 56 * 1024 * 1024

TM_PROMPT = 256
ATTN_QBLOCKS = 16
RG_TILE = MXU_DIM
FFN_CHUNKS = ((0, 768), (768, 768), (1536, 768), (2304, 512))


def _dot(a, b):
    return jnp.dot(a, b, preferred_element_type=F32)


def _dot_nt(a, b):
    return lax.dot_general(a, b, (((1,), (1,)), ((), ())), preferred_element_type=F32)


def _sigmoid(x):
    return 0.5 * (jnp.tanh(0.5 * x) + 1.0)


def _gelu_tanh(x):
    c = math.sqrt(2.0 / math.pi)
    return 0.5 * x * (1.0 + jnp.tanh(c * (x + 0.044715 * (x * x * x))))


def _softplus(z):
    return jnp.maximum(z, 0.0) + jnp.log1p(jnp.exp(-jnp.abs(z)))


def _rms_rows(x, g):
    y = x * lax.rsqrt(jnp.mean(x * x, axis=-1, keepdims=True) + EPS)
    return y * g


def _head_rms(t, pn, gain):
    t2 = (t * t).astype(BF16)
    tiles = [_dot(t2[:, c * MXU_DIM:(c + 1) * MXU_DIM], pn) for c in range(t.shape[1] // MXU_DIM)]
    ms = tiles[0] if len(tiles) == 1 else jnp.concatenate(tiles, axis=1)
    return (t * lax.rsqrt(ms + EPS)) * gain


def _lru_gates(g, b_a, b_x, sp):
    w = g.shape[1] // 2
    r = _sigmoid(g[:, :w] + b_a)
    ig = _sigmoid(g[:, w:] + b_x)
    log_a = (-LRU_C * r) * sp
    a = jnp.exp(log_a)
    mult = jnp.sqrt(-jnp.tanh(log_a) * (a * a + 1.0))
    return a, mult, ig


def _scan_rows(a, u, h_in):
    rows, c = a.shape
    groups = rows // SUBLANES
    a3 = a.reshape(groups, SUBLANES, c)
    u3 = u.reshape(groups, SUBLANES, c)
    row = lax.broadcasted_iota(jnp.int32, (groups, SUBLANES, c), 1)
    shift = 1
    while shift < SUBLANES:
        ok = row >= shift
        a_sh = jnp.where(ok, pltpu.roll(a3, shift, axis=1), 1.0)
        u_sh = jnp.where(ok, pltpu.roll(u3, shift, axis=1), 0.0)
        u3 = u3 + a3 * u_sh
        a3 = a3 * a_sh
        shift *= 2
    out = []
    h = h_in
    for g in range(groups):
        hg = a3[g] * h + u3[g]
        out.append(hg)
        h = hg[SUBLANES - 1:SUBLANES, :]
    return jnp.concatenate(out, axis=0), h


def _to_slabs(slab_ref, t, first_slab=0):
    for c in range(t.shape[1] // LANES):
        slab_ref[first_slab + c] = t[:, c * LANES:(c + 1) * LANES]


def _from_slabs(slab_ref, first_slab=0, n_slabs=None):
    n_slabs = slab_ref.shape[0] - first_slab if n_slabs is None else n_slabs
    return jnp.concatenate([slab_ref[first_slab + c] for c in range(n_slabs)], axis=1)


def _gather_residues(slab_ref, dil, first_slab=0, n_slabs=None):
    n_slabs = slab_ref.shape[0] - first_slab if n_slabs is None else n_slabs
    per = slab_ref.shape[1] // dil
    cols = []
    for c in range(first_slab, first_slab + n_slabs):
        cols.append(jnp.concatenate(
            [slab_ref[c, pl.ds(r, per, stride=dil), :] for r in range(dil)], axis=0))
    return jnp.concatenate(cols, axis=1)


def _scatter_residues(slab_ref, t, dil, first_slab=0):
    per = slab_ref.shape[1] // dil
    for c in range(t.shape[1] // LANES):
        for r in range(dil):
            slab_ref[first_slab + c, pl.ds(r, per, stride=dil), :] = (
                t[r * per:(r + 1) * per, c * LANES:(c + 1) * LANES])


def _ffn(x1, xb, wfi_ref, wfo_ref):
    acc = x1
    for start, width in FFN_CHUNKS:
        gate = _dot(xb, wfi_ref[:, start:start + width])
        up = _dot(xb, wfi_ref[:, D_FF + start:D_FF + start + width])
        hid = (gate * _sigmoid(gate)) * up
        acc = acc + _dot(hid.astype(BF16), wfo_ref[start:start + width, :])
    return acc


def _layer_spec(w, layer):
    tail = (0,) * (w.ndim - 1)
    return pl.BlockSpec((None,) + w.shape[1:], lambda *_: (layer,) + tail, pipeline_mode=pl.Buffered(1))


def _const_spec(shape):
    zeros = (0,) * len(shape)
    return pl.BlockSpec(shape, lambda *_: zeros, pipeline_mode=pl.Buffered(1))


FRONT_WEIGHTS = ("norm_mix", "w_in", "w_merge", "b_merge", "w_conv", "b_conv", "w_rg", "b_rg_a", "b_rg_x",
                 "lam", "q_gain", "k_gain", "w_branch_lru")
BACK_WEIGHTS = ("w_branch_att", "w_o", "norm_ffn", "w_ffn_in", "w_ffn_out")


def _front_kernel(x_ref, nm_ref, win_ref, wmg_ref, bmg_ref, wconv_ref, bconv_ref, wrg_ref,
                  brga_ref, brgx_ref, lam_ref, qg_ref, kg_ref, wbl_ref, pn_ref,
                  q0_ref, q1_ref, q2_ref, k0_ref, k1_ref, k2_ref, v0_ref, v1_ref, v2_ref,
                  ylg_ref, gatt_ref, kvp0_ref, kvp1_ref, kvp2_ref, convp_ref, hp_ref,
                  xn_ref, kv_ref, ext_ref, h_ref, xb_ref, xg_ref, xc_ref, xcb_ref, gl_ref, y_ref, *, tm):
    i = pl.program_id(1)
    n_tiles = D_RNN // RG_TILE

    @pl.when(i == 0)
    def _():
        ext_ref[0:SUBLANES, :] = jnp.zeros((SUBLANES, D_RNN), F32)
        h_ref[...] = jnp.zeros((SUBLANES, D_RNN), F32)

    xn = _rms_rows(x_ref[...], nm_ref[...])
    _to_slabs(xn_ref, xn)
    xb_ref[...] = xn.astype(BF16)

    for t in range(n_tiles):
        sl = slice(t * RG_TILE, (t + 1) * RG_TILE)
        lru_x = _dot(xb_ref[...], win_ref[:, sl])
        ext_ref[SUBLANES:SUBLANES + tm, sl] = lru_x
        xc = bconv_ref[:, sl] + wconv_ref[0:1, sl] * ext_ref[SUBLANES - 3:SUBLANES - 3 + tm, sl]
        xc = xc + wconv_ref[1:2, sl] * ext_ref[SUBLANES - 2:SUBLANES - 2 + tm, sl]
        xc = xc + wconv_ref[2:3, sl] * ext_ref[SUBLANES - 1:SUBLANES - 1 + tm, sl]
        xc = xc + wconv_ref[3:4, sl] * lru_x
        xc_ref[:, sl] = xc
        xcb_ref[:, sl] = xc.astype(BF16)
        gl_ref[:, sl] = _gelu_tanh(_dot(xb_ref[...], win_ref[:, D_RNN + t * RG_TILE:D_RNN + (t + 1) * RG_TILE]))
    tail = ext_ref[tm:tm + SUBLANES, :]
    ext_ref[0:SUBLANES, :] = tail
    convp_ref[...] = tail

    sp = _softplus(-lam_ref[...])
    first_row = (lax.broadcasted_iota(jnp.int32, (tm, RG_TILE), 0) == 0) & (i == 0)
    gates = {}

    def lru_gates(t):
        sl = slice(t * RG_TILE, (t + 1) * RG_TILE)
        g = _dot(xcb_ref[:, sl], wrg_ref[t])
        a, mult, ig = _lru_gates(g, brga_ref[:, sl], brgx_ref[:, sl], sp[:, sl])
        mult = jnp.where(first_row, 1.0, mult)
        gates[t] = (a, (mult * ig) * xc_ref[:, sl])

    def lru_scan(t):
        sl = slice(t * RG_TILE, (t + 1) * RG_TILE)
        a, u = gates.pop(t)
        hs, h_last = _scan_rows(a, u, h_ref[0:1, sl])
        h_ref[:, sl] = jnp.broadcast_to(h_last, (SUBLANES, RG_TILE))
        y_ref[:, sl] = (hs * gl_ref[:, sl]).astype(BF16)

    pn = pn_ref[...]
    q_refs = (q0_ref, q1_ref, q2_ref)
    k_refs = (k0_ref, k1_ref, k2_ref)
    v_refs = (v0_ref, v1_ref, v2_ref)
    kvp_refs = (kvp0_ref, kvp1_ref, kvp2_ref)

    def attn_operands(g, half):
        window, dil = DILATED_GROUPS[g]
        per = tm // dil
        keep = min(window, tm)
        if dil == 1:
            src = xb_ref
        else:
            if half == 0:
                xg_ref[...] = _gather_residues(xn_ref, dil).astype(BF16)
            src = xg_ref
        c = 2 * D_RNN + g * GROUP_WIDTH + half * MXU_DIM
        hl = slice(half * MXU_DIM, (half + 1) * MXU_DIM)
        vl = slice(GROUP_WIDTH + half * MXU_DIM, GROUP_WIDTH + (half + 1) * MXU_DIM)
        q = _dot(src[...], win_ref[:, c:c + MXU_DIM])
        k = _dot(src[...], win_ref[:, c + ATT_WIDTH:c + ATT_WIDTH + MXU_DIM])
        v = _dot(src[...], win_ref[:, c + 2 * ATT_WIDTH:c + 2 * ATT_WIDTH + MXU_DIM])
        qn = _head_rms(q, pn, qg_ref[:, hl]).astype(BF16)
        kn = _head_rms(k, pn, kg_ref[:, hl])
        knb = kn.astype(BF16)
        vb = v.astype(BF16)
        for r in range(dil):
            rows = slice(r * per, (r + 1) * per)
            q_refs[g][r, :, hl] = qn[rows]
            k_refs[g][r, :, hl] = knb[rows]
            v_refs[g][r, :, hl] = vb[rows]
        if dil == 1:
            kvp_refs[g][:, hl] = kn[tm - keep:, :]
            kvp_refs[g][:, vl] = v[tm - keep:, :]
        else:
            _scatter_residues(kv_ref, kn, dil, first_slab=hl.start // LANES)
            _scatter_residues(kv_ref, v, dil, first_slab=vl.start // LANES)
            if half == GROUP_WIDTH // MXU_DIM - 1:
                kvp_refs[g][...] = _from_slabs(kv_ref)

    def att_gate(t):
        sl = slice(t * RG_TILE, (t + 1) * RG_TILE)
        al = slice(D_MODEL + t * RG_TILE, D_MODEL + (t + 1) * RG_TILE)
        gatt_ref[:, sl] = _sigmoid(_dot(xb_ref[...], wmg_ref[:, al]) + bmg_ref[:, al]).astype(gatt_ref.dtype)

    for t in range(n_tiles):
        lru_gates(t)
        if t < N_GROUPS:
            attn_operands(t, 0)
        else:
            att_gate(0)
            att_gate(1)
        lru_scan(t)
        if t < N_GROUPS:
            attn_operands(t, 1)
        else:
            att_gate(2)
            att_gate(3)
    hp_ref[...] = h_ref[...]
    for t in range(n_tiles):
        sl = slice(t * RG_TILE, (t + 1) * RG_TILE)
        g_lru = _sigmoid(_dot(xb_ref[...], wmg_ref[:, sl]) + bmg_ref[:, sl])
        ylg_ref[:, sl] = (g_lru * _dot(y_ref[...], wbl_ref[:, sl])).astype(ylg_ref.dtype)


def _front_call(x, lw, layer, tm):
    B, S, _ = x.shape
    nt = S // tm
    row_spec = lambda width: pl.BlockSpec((None, tm, width), lambda b, i: (b, i, 0))
    weights = [lw[n] for n in FRONT_WEIGHTS]
    in_specs = ([row_spec(D_MODEL)] + [_layer_spec(w, layer) for w in weights]
                + [_const_spec(lw["pnorm"].shape)])

    out_shape, out_specs = [], []
    for _ in range(3):
        for _, dil in DILATED_GROUPS:
            out_shape.append(jax.ShapeDtypeStruct((B, dil, S // dil, GROUP_WIDTH), BF16))
            out_specs.append(pl.BlockSpec((None, dil, tm // dil, GROUP_WIDTH), lambda b, i: (b, 0, i, 0)))
    out_shape.append(jax.ShapeDtypeStruct((B, S, D_MODEL), BF16))
    out_specs.append(row_spec(D_MODEL))
    out_shape.append(jax.ShapeDtypeStruct((B, S, D_MODEL), BF16))
    out_specs.append(row_spec(D_MODEL))
    for window, _ in DILATED_GROUPS:
        keep = min(window, tm)
        first = nt - window // keep
        out_shape.append(jax.ShapeDtypeStruct((B, window, 2 * GROUP_WIDTH), F32))
        out_specs.append(pl.BlockSpec(
            (None, keep, 2 * GROUP_WIDTH),
            functools.partial(lambda b, i, first: (b, jnp.maximum(i - first, 0), 0), first=first)))
    for _ in range(2):
        out_shape.append(jax.ShapeDtypeStruct((B, SUBLANES, D_RNN), F32))
        out_specs.append(pl.BlockSpec((None, SUBLANES, D_RNN), lambda b, i: (b, 0, 0)))

    return pl.pallas_call(
        functools.partial(_front_kernel, tm=tm),
        grid=(B, nt),
        in_specs=in_specs,
        out_specs=out_specs,
        out_shape=out_shape,
        scratch_shapes=[pltpu.VMEM((D_MODEL // LANES, tm, LANES), F32),
                        pltpu.VMEM((2 * GROUP_WIDTH // LANES, tm, LANES), F32),
                        pltpu.VMEM((tm + SUBLANES, D_RNN), F32), pltpu.VMEM((SUBLANES, D_RNN), F32),
                        pltpu.VMEM((tm, D_MODEL), BF16), pltpu.VMEM((tm, D_MODEL), BF16),
                        pltpu.VMEM((tm, D_RNN), F32), pltpu.VMEM((tm, D_RNN), BF16),
                        pltpu.VMEM((tm, D_RNN), F32), pltpu.VMEM((tm, D_RNN), BF16)],
        compiler_params=pltpu.CompilerParams(
            dimension_semantics=("arbitrary", "arbitrary"), vmem_limit_bytes=VMEM_LIMIT_BYTES),
        name="prompt_front",
    )(x, *weights, lw["pnorm"])


def _attn_kernel(tab_ref, bkt_ref, q_ref, kp_ref, kc_ref, vp_ref, vc_ref, o_ref, lse_ref,
                 bias_ref, kbuf_ref, vbuf_ref, *, qblocks):
    b, r, j = pl.program_id(0), pl.program_id(1), pl.program_id(2)
    BB = BAND_BLOCK
    PAIR = 2 * HEAD_DIM

    @pl.when((b == 0) & (r == 0) & (j == 0))
    def _():
        bkt = bkt_ref[...]
        for h in range(HEADS_PER_GROUP):
            acc = jnp.full(bkt.shape, -jnp.inf, F32)
            for n in range(NUM_BUCKETS):
                acc = jnp.where(bkt == n, tab_ref[n, h], acc)
            bias_ref[h // 2, (h % 2) * BB:(h % 2 + 1) * BB, :] = acc

    first_col = lax.broadcasted_iota(jnp.int32, (1, 2 * BB), 1) < BB
    pen = jnp.where(first_col & (j == 0), -jnp.inf, 0.0).astype(F32)
    low_half = lax.broadcasted_iota(jnp.int32, (BB, PAIR), 1) < HEAD_DIM
    ones = jnp.ones((2 * BB, PAIR), BF16)
    for rr in range(q_ref.shape[0]):
        kbuf_ref[0:BB, :] = kp_ref[rr]
        kbuf_ref[BB:, :] = kc_ref[rr]
        vbuf_ref[0:BB, :] = vp_ref[rr]
        vbuf_ref[BB:, :] = vc_ref[rr]
        for t in range(qblocks):
            rows = slice(t * BB, (t + 1) * BB)
            keys = slice(t * BB, (t + 2) * BB)
            for hp in range(HEADS_PER_GROUP // 2):
                sl = slice(hp * PAIR, (hp + 1) * PAIR)
                qp = q_ref[rr, rows, sl]
                zero = jnp.zeros_like(qp)
                lhs = jnp.concatenate([jnp.where(low_half, qp, zero), jnp.where(low_half, zero, qp)], axis=0)
                s = _dot_nt(lhs, kbuf_ref[keys, sl]) + bias_ref[hp]
                if t == 0:
                    s = s + pen
                m = jnp.max(s, axis=-1, keepdims=True)
                p = jnp.exp((s - m).astype(BF16))
                oe = _dot(p, jnp.concatenate([vbuf_ref[keys, sl], ones], axis=1))
                l = oe[:, PAIR:]
                o = oe[:, :PAIR] / l
                lse = m + jnp.log(l)
                for e in range(2):
                    hl = slice(hp * PAIR + e * HEAD_DIM, hp * PAIR + (e + 1) * HEAD_DIM)
                    o_ref[rr, rows, hl] = o[e * BB:(e + 1) * BB, e * HEAD_DIM:(e + 1) * HEAD_DIM]
                    lse_ref[rr, rows, hl] = lse[e * BB:(e + 1) * BB, e * HEAD_DIM:(e + 1) * HEAD_DIM]


def _t5_bucket(dist):
    max_exact = NUM_BUCKETS // 2
    d_f = np.maximum(dist, 1).astype(np.float32)
    large = max_exact + (np.log(d_f / np.float32(max_exact)) / np.float32(math.log(MAX_DISTANCE / max_exact))
                         * np.float32(NUM_BUCKETS - max_exact)).astype(np.int32)
    large = np.minimum(large, NUM_BUCKETS - 1)
    return np.where(dist < max_exact, dist, large).astype(np.int32)


def _band_buckets(dil):
    BB = BAND_BLOCK
    qi = np.arange(BB)[:, None]
    ki = np.arange(2 * BB)[None, :]
    sub = qi + BB - ki
    in_band = (sub >= 0) & (sub <= BB)
    return jnp.asarray(np.where(in_band, _t5_bucket(np.clip(sub, 0, BB) * dil), -1).astype(np.int32))


def _attn_call(q, k, v, tab, dil):
    B, _, n, C = q.shape
    BB = BAND_BLOCK
    qb = min(ATTN_QBLOCKS, n // BB)
    res = ATTN_QBLOCKS // qb
    cur = pl.BlockSpec((None, res, qb * BB, C), lambda b, r, j: (b, r, j, 0))
    prev = pl.BlockSpec((None, res, BB, C), lambda b, r, j: (b, r, jnp.maximum(j * qb - 1, 0), 0))
    return pl.pallas_call(
        functools.partial(_attn_kernel, qblocks=qb),
        grid=(B, dil // res, n // (qb * BB)),
        in_specs=[pl.BlockSpec(memory_space=pltpu.SMEM),
                  pl.BlockSpec((BB, 2 * BB), lambda b, r, j: (0, 0)),
                  cur, prev, cur, prev, cur],
        out_specs=[cur, cur],
        out_shape=[jax.ShapeDtypeStruct(q.shape, F32)] * 2,
        scratch_shapes=[pltpu.VMEM((HEADS_PER_GROUP // 2, 2 * BB, 2 * BB), F32),
                        pltpu.VMEM(((qb + 1) * BB, C), BF16), pltpu.VMEM(((qb + 1) * BB, C), BF16)],
        compiler_params=pltpu.CompilerParams(
            dimension_semantics=("arbitrary", "arbitrary", "arbitrary"), vmem_limit_bytes=VMEM_LIMIT_BYTES),
        name="prompt_attn_d%d" % dil,
    )(tab, _band_buckets(dil), q, k, k, v, v)


def _back_kernel(x_ref, ylg_ref, gatt_ref, o0_ref, o1_ref, o2_ref, l0_ref, l1_ref, l2_ref,
                 wba_ref, wo_ref, nf_ref, wfi_ref, wfo_ref,
                 qt_ref, kt_ref, vt_ref, c0_ref, c1_ref, c2_ref, tabt_ref, bkt0_ref, bkt1_ref, bkt2_ref,
                 y_ref, ot_ref,
                 so1_ref, so2_ref, sl1_ref, sl2_ref, bias0_ref, bias1_ref, bias2_ref, s0_ref, s1_ref, s2_ref,
                 *, tm, dec_parts):
    step = pl.program_id(0) * pl.num_programs(1) + pl.program_id(1)
    bias_refs = (bias0_ref, bias1_ref, bias2_ref)
    s_refs = (s0_ref, s1_ref, s2_ref)

    @pl.when(step == 0)
    def _():
        _decode_attention_init(tabt_ref, (bkt0_ref, bkt1_ref, bkt2_ref), bias_refs, s_refs, ot_ref)

    _decode_attention_step(step // dec_parts, step % dec_parts, HEADS_PER_GROUP // dec_parts,
                           qt_ref, kt_ref, vt_ref, (c0_ref, c1_ref, c2_ref), tabt_ref, bias_refs, s_refs, ot_ref)

    for (_, dil), src, dst in ((DILATED_GROUPS[1], o1_ref, so1_ref), (DILATED_GROUPS[2], o2_ref, so2_ref),
                               (DILATED_GROUPS[1], l1_ref, sl1_ref), (DILATED_GROUPS[2], l2_ref, sl2_ref)):
        _scatter_residues(dst, src[...].reshape(tm, GROUP_WIDTH), dil)
    os_ = (o0_ref[0], _from_slabs(so1_ref), _from_slabs(so2_ref))
    lses = (l0_ref[0], _from_slabs(sl1_ref), _from_slabs(sl2_ref))

    top = jnp.maximum(jnp.maximum(lses[0], lses[1]), lses[2])
    num = 0.0
    den = 0.0
    for o, lse in zip(os_, lses):
        z = jnp.exp(lse - top)
        num = num + z * o
        den = den + z
    o = num / den

    y_att = _dot(o.astype(BF16), wba_ref[...])
    mix = ylg_ref[...] + gatt_ref[...] * y_att
    x1 = x_ref[...] + _dot(mix.astype(BF16), wo_ref[...])
    y_ref[...] = _ffn(x1, _rms_rows(x1, nf_ref[...]).astype(BF16), wfi_ref, wfo_ref)


def _back_call(x, ylg, gatt, os_, lses, lw, layer, tm, qt, kt, vt, caches_t, rel_bias):
    B, S, _ = x.shape
    nt = S // tm
    nb = qt.shape[1]
    dec_parts = (B * nt) // nb
    assert dec_parts * nb == B * nt and HEADS_PER_GROUP % dec_parts == 0
    n_heads = HEADS_PER_GROUP // dec_parts
    tabt, bkts = _decode_attention_operands(rel_bias, dec_parts)

    row_spec = lambda width: pl.BlockSpec((None, tm, width), lambda b, i: (b, i, 0))
    res_specs = [pl.BlockSpec((None, dil, tm // dil, GROUP_WIDTH), lambda b, i: (b, 0, i, 0))
                 for _, dil in DILATED_GROUPS]
    cache_spec = lambda c: pl.BlockSpec(
        (None, None, 2, n_heads) + c.shape[4:],
        lambda b, i: (layer, (b * nt + i) // dec_parts, 0, (b * nt + i) % dec_parts, 0, 0))
    weights = [lw[n] for n in BACK_WEIGHTS]
    in_specs = ([row_spec(D_MODEL)] * 3 + res_specs * 2 + [_layer_spec(w, layer) for w in weights]
                + [_const_spec(qt.shape)] * 3 + [cache_spec(c) for c in caches_t]
                + [_const_spec(tabt.shape)] + [_const_spec(b.shape) for b in bkts])
    return pl.pallas_call(
        functools.partial(_back_kernel, tm=tm, dec_parts=dec_parts),
        grid=(B, nt),
        in_specs=in_specs,
        out_specs=[row_spec(D_MODEL), pl.BlockSpec((GROUP_WIDTH, nb), lambda b, i: (0, 0))],
        out_shape=[jax.ShapeDtypeStruct((B, S, D_MODEL), F32), jax.ShapeDtypeStruct((GROUP_WIDTH, nb), F32)],
        scratch_shapes=[pltpu.VMEM((GROUP_WIDTH // LANES, tm, LANES), F32)] * 4
        + [pltpu.VMEM((dec_parts, SUBLANES, window), F32) for window, _ in DILATED_GROUPS]
        + [pltpu.VMEM((SUBLANES, window), F32) for window, _ in DILATED_GROUPS],
        compiler_params=pltpu.CompilerParams(
            dimension_semantics=("arbitrary", "arbitrary"), vmem_limit_bytes=VMEM_LIMIT_BYTES),
        name="prompt_back",
    )(x, ylg, gatt, *os_, *lses, *weights, qt, kt, vt, *caches_t, tabt, *bkts)


def _dec_front_kernel(x_ref, c0_ref, c1_ref, c2_ref, h0_ref, nm_ref, win_ref, wmg_ref, bmg_ref,
                      wconv_ref, bconv_ref, wrg_ref, brga_ref, brgx_ref, lam_ref, qg_ref, kg_ref,
                      wbl_ref, pn_ref,
                      q_ref, k_ref, v_ref, ylg_ref, gatt_ref, lrux_ref, hs_ref):
    xb = _rms_rows(x_ref[...], nm_ref[...]).astype(BF16)
    lru_x = _dot(xb, win_ref[:, 0:D_RNN])
    lrux_ref[...] = lru_x
    xc = bconv_ref[...] + wconv_ref[0:1, :] * c0_ref[...]
    xc = xc + wconv_ref[1:2, :] * c1_ref[...]
    xc = xc + wconv_ref[2:3, :] * c2_ref[...]
    xc = xc + wconv_ref[3:4, :] * lru_x

    lru_g = _dot(xb, win_ref[:, D_RNN:2 * D_RNN])
    xcb = xc.astype(BF16)
    sp = _softplus(-lam_ref[...])
    ys = []
    for t in range(D_RNN // RG_TILE):
        sl = slice(t * RG_TILE, (t + 1) * RG_TILE)
        g = _dot(xcb[:, sl], wrg_ref[t])
        a, mult, ig = _lru_gates(g, brga_ref[:, sl], brgx_ref[:, sl], sp[:, sl])
        h = a * h0_ref[:, sl] + (mult * ig) * xc[:, sl]
        hs_ref[:, sl] = h
        ys.append((h * _gelu_tanh(lru_g[:, sl])).astype(BF16))
    y_lru = _dot(jnp.concatenate(ys, axis=1), wbl_ref[...])
    g_lru = _sigmoid(_dot(xb, wmg_ref[:, 0:D_MODEL]) + bmg_ref[:, 0:D_MODEL])
    ylg_ref[...] = g_lru * y_lru
    gatt_ref[...] = _sigmoid(_dot(xb, wmg_ref[:, D_MODEL:2 * D_MODEL]) + bmg_ref[:, D_MODEL:2 * D_MODEL])

    pn = pn_ref[...]
    for g in range(N_GROUPS):
        c = 2 * D_RNN + g * GROUP_WIDTH
        sl = slice(g * GROUP_WIDTH, (g + 1) * GROUP_WIDTH)
        q = _dot(xb, win_ref[:, c:c + GROUP_WIDTH])
        k = _dot(xb, win_ref[:, c + ATT_WIDTH:c + ATT_WIDTH + GROUP_WIDTH])
        v = _dot(xb, win_ref[:, c + 2 * ATT_WIDTH:c + 2 * ATT_WIDTH + GROUP_WIDTH])
        q_ref[:, sl] = _head_rms(q, pn, qg_ref[...])
        k_ref[:, sl] = _head_rms(k, pn, kg_ref[...])
        v_ref[:, sl] = v


def _dec_front_call(x, conv_rows, h0, lw, layer):
    nb = x.shape[0]
    weights = [lw[n] for n in FRONT_WEIGHTS]
    acts = (x,) + tuple(conv_rows) + (h0,)
    widths = (ATT_WIDTH, ATT_WIDTH, ATT_WIDTH, D_MODEL, D_MODEL, D_RNN, D_RNN)
    return pl.pallas_call(
        _dec_front_kernel,
        grid=(1,),
        in_specs=([_const_spec(a.shape) for a in acts] + [_layer_spec(w, layer) for w in weights]
                  + [_const_spec(lw["pnorm"].shape)]),
        out_specs=[pl.BlockSpec((nb, w), lambda i: (0, 0)) for w in widths],
        out_shape=[jax.ShapeDtypeStruct((nb, w), F32) for w in widths],
        compiler_params=pltpu.CompilerParams(
            dimension_semantics=("arbitrary",), vmem_limit_bytes=VMEM_LIMIT_BYTES),
        name="decode_front",
    )(*acts, *weights, lw["pnorm"])


def _decode_attention_init(tabt_ref, bkt_refs, bias_refs, s_refs, ot_ref):
    ot_ref[...] = jnp.zeros(ot_ref.shape, F32)
    for g in range(N_GROUPS):
        s_refs[g][...] = jnp.zeros(s_refs[g].shape, F32)
        bkt = bkt_refs[g][...]
        for part in range(bias_refs[g].shape[0]):
            acc = jnp.full(bkt.shape, -jnp.inf, F32)
            for n in range(NUM_BUCKETS):
                acc = jnp.where(bkt == n, tabt_ref[g, part, :, n:n + 1], acc)
            bias_refs[g][part] = acc


def _decode_attention_step(seq, part, n_heads, qt_ref, kt_ref, vt_ref, c_refs, tabt_ref, bias_refs, s_refs,
                           ot_ref):
    mine = lax.broadcasted_iota(jnp.int32, (HEAD_DIM, qt_ref.shape[1]), 1) == seq

    def column(ref, g, h):
        row0 = pl.multiple_of(g * GROUP_WIDTH + (part * n_heads + h) * HEAD_DIM, HEAD_DIM)
        return jnp.sum(jnp.where(mine, ref[pl.ds(row0, HEAD_DIM), :], 0.0), axis=1, keepdims=True)

    def fold_lanes(t, op):
        out = t[:, 0:LANES]
        for c in range(1, t.shape[1] // LANES):
            out = op(out, t[:, c * LANES:(c + 1) * LANES])
        return out

    pad = [jnp.zeros((SUBLANES - n_heads, 1), F32)] if n_heads < SUBLANES else []
    s0, vcols = [], []
    for g in range(N_GROUPS):
        rows = []
        for h in range(n_heads):
            qh = column(qt_ref, g, h)
            s_refs[g][h:h + 1, :] = jnp.sum(c_refs[g][0, h] * qh, axis=0, keepdims=True)
            rows.append(jnp.sum(qh * column(kt_ref, g, h), axis=0, keepdims=True))
            vcols.append(column(vt_ref, g, h))
        s0.append(jnp.concatenate(rows + pad, axis=0) + tabt_ref[g, part, :, 0:1])

    ss = [s_refs[g][...] + bias_refs[g][part] for g in range(N_GROUPS)]
    m_max = jnp.maximum(jnp.maximum(s0[0], s0[1]), s0[2])
    for s in ss:
        m_max = jnp.maximum(m_max, jnp.max(fold_lanes(s, jnp.maximum), axis=1, keepdims=True))
    p0 = [jnp.exp(s - m_max) for s in s0]
    den = p0[0] + p0[1] + p0[2]
    for g, s in enumerate(ss):
        p = jnp.exp(s - m_max)
        s_refs[g][...] = p
        den = den + jnp.sum(fold_lanes(p, jnp.add), axis=1, keepdims=True)
    inv_den = 1.0 / den

    for h in range(n_heads):
        acc = None
        new = 0.0
        for g in range(N_GROUPS):
            f = fold_lanes(c_refs[g][1, h] * s_refs[g][h:h + 1, :], jnp.add)
            acc = f if acc is None else acc + f
            new = new + p0[g][h:h + 1, :] * vcols[g * n_heads + h]
        num = jnp.sum(acc, axis=1, keepdims=True) + new
        row0 = pl.multiple_of((part * n_heads + h) * HEAD_DIM, HEAD_DIM)
        ot_ref[pl.ds(row0, HEAD_DIM), :] = jnp.where(mine, num * inv_den[h:h + 1, :],
                                                      ot_ref[pl.ds(row0, HEAD_DIM), :])


def _decode_attention_operands(rel_bias, parts):
    n_heads = HEADS_PER_GROUP // parts
    tab = rel_bias.reshape(NUM_BUCKETS, N_GROUPS, parts, n_heads).transpose(1, 2, 3, 0)
    tab = jnp.pad(tab, ((0, 0), (0, 0), (0, SUBLANES - n_heads), (0, 0)))
    bkts = []
    for window, dil in DILATED_GROUPS:
        w = np.arange(window)
        bkt = np.where(w % dil == 0, _t5_bucket(window - w), -1).astype(np.int32)
        bkts.append(jnp.asarray(np.broadcast_to(bkt[None, :], (SUBLANES, window))))
    return tab, bkts


def _dec_back_kernel(x_ref, ylg_ref, gatt_ref, o_ref, wba_ref, wo_ref, nf_ref, wfi_ref, wfo_ref, y_ref):
    y_att = _dot(o_ref[...].astype(BF16), wba_ref[...])
    mix = ylg_ref[...] + gatt_ref[...] * y_att
    x1 = x_ref[...] + _dot(mix.astype(BF16), wo_ref[...])
    y_ref[...] = _ffn(x1, _rms_rows(x1, nf_ref[...]).astype(BF16), wfi_ref, wfo_ref)


def _dec_back_call(x, ylg, gatt, o, lw, layer):
    weights = [lw[n] for n in BACK_WEIGHTS]
    acts = (x, ylg, gatt, o)
    return pl.pallas_call(
        _dec_back_kernel,
        grid=(1,),
        in_specs=[_const_spec(a.shape) for a in acts] + [_layer_spec(w, layer) for w in weights],
        out_specs=pl.BlockSpec(x.shape, lambda i: (0, 0)),
        out_shape=jax.ShapeDtypeStruct(x.shape, F32),
        compiler_params=pltpu.CompilerParams(
            dimension_semantics=("arbitrary",), vmem_limit_bytes=VMEM_LIMIT_BYTES),
        name="decode_back",
    )(*acts, *weights)


def _prepare_weights(norm_mix, w_in, w_conv, b_conv, w_rg_a, b_rg_a, w_rg_x, b_rg_x, lru_lambda,
                     q_gain, k_gain, w_merge, b_merge, w_branch_lru, w_branch_att, w_o, norm_ffn,
                     w_ffn_in, w_ffn_out):
    depth = norm_mix.shape[0]
    row = lambda t: t.reshape(depth, 1, -1).astype(F32)
    per_tile = RG_TILE // LRU_BLOCK
    tiles = D_RNN // RG_TILE

    def block_diag(w):
        w = w.reshape(depth, tiles, per_tile, LRU_BLOCK, LRU_BLOCK)
        eye = jnp.eye(per_tile, dtype=w.dtype)
        return jnp.einsum("ltnij,nm->ltnimj", w, eye).reshape(depth, tiles, RG_TILE, RG_TILE)

    head = jnp.arange(MXU_DIM) // HEAD_DIM
    tile_gain = lambda t: jnp.tile(t.reshape(depth, 1, HEAD_DIM), (1, 1, HEADS_PER_GROUP)).astype(F32)
    return dict(
        norm_mix=row(norm_mix), w_in=w_in.astype(BF16), w_merge=w_merge.astype(BF16),
        b_merge=row(b_merge), w_conv=w_conv.astype(F32), b_conv=row(b_conv),
        w_rg=jnp.concatenate([block_diag(w_rg_a), block_diag(w_rg_x)], axis=3).astype(BF16),
        b_rg_a=row(b_rg_a), b_rg_x=row(b_rg_x), lam=row(lru_lambda),
        q_gain=tile_gain(q_gain) * (HEAD_DIM ** -0.5),
        k_gain=tile_gain(k_gain),
        w_branch_lru=w_branch_lru.astype(BF16),
        pnorm=((head[:, None] == head[None, :]).astype(F32) / HEAD_DIM).astype(BF16),
        w_branch_att=w_branch_att.astype(BF16), w_o=w_o.astype(BF16), norm_ffn=row(norm_ffn),
        w_ffn_in=w_ffn_in.astype(BF16), w_ffn_out=w_ffn_out.astype(BF16))


def kernel(x_prompt, x_sample, cache_kv_g0, cache_kv_g1, cache_kv_g2, state_conv, state_h, rel_bias,
           norm_mix, w_in, w_conv, b_conv, w_rg_a, b_rg_a, w_rg_x, b_rg_x, lru_lambda, q_gain, k_gain,
           w_merge, b_merge, w_branch_lru, w_branch_att, w_o, norm_ffn, w_ffn_in, w_ffn_out):
    B, S, _ = x_prompt.shape
    nb = x_sample.shape[0]
    depth = norm_mix.shape[0]
    lw = _prepare_weights(norm_mix, w_in, w_conv, b_conv, w_rg_a, b_rg_a, w_rg_x, b_rg_x, lru_lambda,
                          q_gain, k_gain, w_merge, b_merge, w_branch_lru, w_branch_att, w_o, norm_ffn,
                          w_ffn_in, w_ffn_out)
    rel_bias = rel_bias.astype(F32)
    caches_t = [jnp.transpose(c.astype(F32), (0, 1, 3, 4, 5, 2))
                for c in (cache_kv_g0, cache_kv_g1, cache_kv_g2)]

    yp = x_prompt
    ys = x_sample.reshape(nb, D_MODEL)
    kvp = [[] for _ in range(N_GROUPS)]
    kvs = [[] for _ in range(N_GROUPS)]
    conv_p, h_p, conv_s, h_s = [], [], [], []
    for l in range(depth):
        sc = state_conv[l].astype(F32)
        conv_rows = [sc[:, r, :] for r in range(CONV_WIDTH - 1)]
        qs, ks, vs, ylg_s, gatt_s, lrux_s, hs_s = _dec_front_call(ys, conv_rows, state_h[l].astype(F32), lw, l)

        (q0, q1, q2, k0, k1, k2, v0, v1, v2, ylg, gatt, kv0, kv1, kv2, ctail, hfin) = _front_call(
            yp, lw, l, TM_PROMPT)
        os_, lses = [], []
        for g, (qg, kg, vg) in enumerate(((q0, k0, v0), (q1, k1, v1), (q2, k2, v2))):
            tab = rel_bias[:, g * HEADS_PER_GROUP:(g + 1) * HEADS_PER_GROUP]
            o, lse = _attn_call(qg, kg, vg, tab, DILATED_GROUPS[g][1])
            os_.append(o)
            lses.append(lse)
        yp, o_t = _back_call(yp, ylg, gatt, os_, lses, lw, l, TM_PROMPT, qs.T, ks.T, vs.T, caches_t, rel_bias)
        for g, kv in enumerate((kv0, kv1, kv2)):
            kvp[g].append(kv.reshape(B, kv.shape[1], 2, HEADS_PER_GROUP, HEAD_DIM))
        conv_p.append(ctail[:, SUBLANES - (CONV_WIDTH - 1):, :])
        h_p.append(hfin[:, 0, :])

        ys = _dec_back_call(ys, ylg_s, gatt_s, o_t.T, lw, l)
        for g in range(N_GROUPS):
            sl = slice(g * GROUP_WIDTH, (g + 1) * GROUP_WIDTH)
            kvs[g].append(jnp.stack([ks[:, sl], vs[:, sl]], axis=1).reshape(
                nb, 1, 2, HEADS_PER_GROUP, HEAD_DIM))
        conv_s.append(jnp.stack(conv_rows[1:] + [lrux_s], axis=1))
        h_s.append(hs_s)

    stack = jnp.stack
    return (yp, ys.reshape(nb, 1, D_MODEL),
            stack(kvp[0]), stack(kvp[1]), stack(kvp[2]), stack(conv_p), stack(h_p),
            stack(kvs[0]), stack(kvs[1]), stack(kvs[2]), stack(conv_s), stack(h_s))
```

```python
import functools
import math

import numpy as np
import jax
import jax.numpy as jnp
from jax import lax
from jax.experimental import pallas as pl
from jax.experimental.pallas import tpu as pltpu

D_MODEL = 1024
D_RNN = D_MODEL
N_LRU_BLOCKS = 16
LRU_BLOCK = D_RNN // N_LRU_BLOCKS
CONV_WIDTH = 4
LRU_C = 8.0
HEAD_DIM = 64
HEADS_PER_GROUP = 8
DILATED_GROUPS = ((128, 1), (512, 4), (2048, 16))
N_GROUPS = len(DILATED_GROUPS)
N_ATT_HEADS = N_GROUPS * HEADS_PER_GROUP
ATT_WIDTH = N_ATT_HEADS * HEAD_DIM
GROUP_WIDTH = HEADS_PER_GROUP * HEAD_DIM
BAND_BLOCK = 128
NUM_BUCKETS = 32
MAX_DISTANCE = 2048
D_FF = 2816
EPS = 1e-6

F32 = jnp.float32
BF16 = jnp.bfloat16

MXU_DIM = 256
SUBLANES = 8
LANES = 128
VMEM_LIMIT_BYTES = 56 * 1024 * 1024

TM_PROMPT = 256
ATTN_QBLOCKS = 16
RG_TILE = MXU_DIM
FFN_CHUNKS = ((0, 768), (768, 768), (1536, 768), (2304, 512))


def _dot(a, b):
    return jnp.dot(a, b, preferred_element_type=F32)


def _dot_nt(a, b):
    return lax.dot_general(a, b, (((1,), (1,)), ((), ())), preferred_element_type=F32)


def _sigmoid(x):
    return 0.5 * (jnp.tanh(0.5 * x) + 1.0)


def _gelu_tanh(x):
    c = math.sqrt(2.0 / math.pi)
    return 0.5 * x * (1.0 + jnp.tanh(c * (x + 0.044715 * (x * x * x))))


def _softplus(z):
    return jnp.maximum(z, 0.0) + jnp.log1p(jnp.exp(-jnp.abs(z)))


def _rms_rows(x, g):
    y = x * lax.rsqrt(jnp.mean(x * x, axis=-1, keepdims=True) + EPS)
    return y * g


def _head_rms(t, pn, gain):
    t2 = (t * t).astype(BF16)
    tiles = [_dot(t2[:, c * MXU_DIM:(c + 1) * MXU_DIM], pn) for c in range(t.shape[1] // MXU_DIM)]
    ms = tiles[0] if len(tiles) == 1 else jnp.concatenate(tiles, axis=1)
    return (t * lax.rsqrt(ms + EPS)) * gain


def _lru_gates(g, b_a, b_x, sp):
    w = g.shape[1] // 2
    r = _sigmoid(g[:, :w] + b_a)
    ig = _sigmoid(g[:, w:] + b_x)
    log_a = (-LRU_C * r) * sp
    a = jnp.exp(log_a)
    mult = jnp.sqrt(-jnp.tanh(log_a) * (a * a + 1.0))
    return a, mult, ig


def _scan_rows(a, u, h_in):
    rows, c = a.shape
    groups = rows // SUBLANES
    a3 = a.reshape(groups, SUBLANES, c)
    u3 = u.reshape(groups, SUBLANES, c)
    row = lax.broadcasted_iota(jnp.int32, (groups, SUBLANES, c), 1)
    shift = 1
    while shift < SUBLANES:
        ok = row >= shift
        a_sh = jnp.where(ok, pltpu.roll(a3, shift, axis=1), 1.0)
        u_sh = jnp.where(ok, pltpu.roll(u3, shift, axis=1), 0.0)
        u3 = u3 + a3 * u_sh
        a3 = a3 * a_sh
        shift *= 2
    out = []
    h = h_in
    for g in range(groups):
        hg = a3[g] * h + u3[g]
        out.append(hg)
        h = hg[SUBLANES - 1:SUBLANES, :]
    return jnp.concatenate(out, axis=0), h


def _to_slabs(slab_ref, t):
    for c in range(slab_ref.shape[0]):
        slab_ref[c] = t[:, c * LANES:(c + 1) * LANES]


def _from_slabs(slab_ref):
    return jnp.concatenate([slab_ref[c] for c in range(slab_ref.shape[0])], axis=1)


def _gather_residues(slab_ref, dil):
    per = slab_ref.shape[1] // dil
    cols = []
    for c in range(slab_ref.shape[0]):
        cols.append(jnp.concatenate(
            [slab_ref[c, pl.ds(r, per, stride=dil), :] for r in range(dil)], axis=0))
    return jnp.concatenate(cols, axis=1)


def _scatter_residues(slab_ref, t, dil, first_slab=0):
    per = slab_ref.shape[1] // dil
    for c in range(t.shape[1] // LANES):
        for r in range(dil):
            slab_ref[first_slab + c, pl.ds(r, per, stride=dil), :] = (
                t[r * per:(r + 1) * per, c * LANES:(c + 1) * LANES])


def _ffn(x1, xb, wfi_ref, wfo_ref):
    acc = x1
    for start, width in FFN_CHUNKS:
        gate = _dot(xb, wfi_ref[:, start:start + width])
        up = _dot(xb, wfi_ref[:, D_FF + start:D_FF + start + width])
        hid = (gate * _sigmoid(gate)) * up
        acc = acc + _dot(hid.astype(BF16), wfo_ref[start:start + width, :])
    return acc


def _layer_spec(w, layer):
    tail = (0,) * (w.ndim - 1)
    return pl.BlockSpec((None,) + w.shape[1:], lambda *_: (layer,) + tail, pipeline_mode=pl.Buffered(1))


def _const_spec(shape):
    zeros = (0,) * len(shape)
    return pl.BlockSpec(shape, lambda *_: zeros, pipeline_mode=pl.Buffered(1))


FRONT_WEIGHTS = ("norm_mix", "w_in", "w_merge", "b_merge", "w_conv", "b_conv", "w_rg", "b_rg_a", "b_rg_x",
                 "lam", "q_gain", "k_gain", "w_branch_lru")
BACK_WEIGHTS = ("w_branch_att", "w_o", "norm_ffn", "w_ffn_in", "w_ffn_out")


def _front_kernel(x_ref, nm_ref, win_ref, wmg_ref, bmg_ref, wconv_ref, bconv_ref, wrg_ref,
                  brga_ref, brgx_ref, lam_ref, qg_ref, kg_ref, wbl_ref, pn_ref,
                  q0_ref, q1_ref, q2_ref, k0_ref, k1_ref, k2_ref, v0_ref, v1_ref, v2_ref,
                  ylg_ref, gatt_ref, kvp0_ref, kvp1_ref, kvp2_ref, convp_ref, hp_ref,
                  xn_ref, kv_ref, ext_ref, h_ref, xb_ref, xg_ref, xc_ref, xcb_ref, gl_ref, y_ref, *, tm):
    i = pl.program_id(1)
    n_tiles = D_RNN // RG_TILE

    @pl.when(i == 0)
    def _():
        ext_ref[0:SUBLANES, :] = jnp.zeros((SUBLANES, D_RNN), F32)
        h_ref[...] = jnp.zeros((SUBLANES, D_RNN), F32)

    xn = _rms_rows(x_ref[...], nm_ref[...])
    _to_slabs(xn_ref, xn)
    xb_ref[...] = xn.astype(BF16)

    for t in range(n_tiles):
        sl = slice(t * RG_TILE, (t + 1) * RG_TILE)
        lru_x = _dot(xb_ref[...], win_ref[:, sl])
        ext_ref[SUBLANES:SUBLANES + tm, sl] = lru_x
        xc = bconv_ref[:, sl] + wconv_ref[0:1, sl] * ext_ref[SUBLANES - 3:SUBLANES - 3 + tm, sl]
        xc = xc + wconv_ref[1:2, sl] * ext_ref[SUBLANES - 2:SUBLANES - 2 + tm, sl]
        xc = xc + wconv_ref[2:3, sl] * ext_ref[SUBLANES - 1:SUBLANES - 1 + tm, sl]
        xc = xc + wconv_ref[3:4, sl] * lru_x
        xc_ref[:, sl] = xc
        xcb_ref[:, sl] = xc.astype(BF16)
        gl_ref[:, sl] = _gelu_tanh(_dot(xb_ref[...], win_ref[:, D_RNN + t * RG_TILE:D_RNN + (t + 1) * RG_TILE]))
    tail = ext_ref[tm:tm + SUBLANES, :]
    ext_ref[0:SUBLANES, :] = tail
    convp_ref[...] = tail

    sp = _softplus(-lam_ref[...])
    first_row = (lax.broadcasted_iota(jnp.int32, (tm, RG_TILE), 0) == 0) & (i == 0)
    gates = {}

    def lru_gates(t):
        sl = slice(t * RG_TILE, (t + 1) * RG_TILE)
        g = _dot(xcb_ref[:, sl], wrg_ref[t])
        a, mult, ig = _lru_gates(g, brga_ref[:, sl], brgx_ref[:, sl], sp[:, sl])
        mult = jnp.where(first_row, 1.0, mult)
        gates[t] = (a, (mult * ig) * xc_ref[:, sl])

    def lru_scan(t):
        sl = slice(t * RG_TILE, (t + 1) * RG_TILE)
        a, u = gates.pop(t)
        hs, h_last = _scan_rows(a, u, h_ref[0:1, sl])
        h_ref[:, sl] = jnp.broadcast_to(h_last, (SUBLANES, RG_TILE))
        y_ref[:, sl] = (hs * gl_ref[:, sl]).astype(BF16)

    pn = pn_ref[...]
    q_refs = (q0_ref, q1_ref, q2_ref)
    k_refs = (k0_ref, k1_ref, k2_ref)
    v_refs = (v0_ref, v1_ref, v2_ref)
    kvp_refs = (kvp0_ref, kvp1_ref, kvp2_ref)

    def attn_operands(g, half):
        window, dil = DILATED_GROUPS[g]
        per = tm // dil
        keep = min(window, tm)
        if dil == 1:
            src = xb_ref
        else:
            if half == 0:
                xg_ref[...] = _gather_residues(xn_ref, dil).astype(BF16)
            src = xg_ref
        c = 2 * D_RNN + g * GROUP_WIDTH + half * MXU_DIM
        hl = slice(half * MXU_DIM, (half + 1) * MXU_DIM)
        vl = slice(GROUP_WIDTH + half * MXU_DIM, GROUP_WIDTH + (half + 1) * MXU_DIM)
        q = _dot(src[...], win_ref[:, c:c + MXU_DIM])
        k = _dot(src[...], win_ref[:, c + ATT_WIDTH:c + ATT_WIDTH + MXU_DIM])
        v = _dot(src[...], win_ref[:, c + 2 * ATT_WIDTH:c + 2 * ATT_WIDTH + MXU_DIM])
        qn = _head_rms(q, pn, qg_ref[:, hl]).astype(BF16)
        kn = _head_rms(k, pn, kg_ref[:, hl])
        knb = kn.astype(BF16)
        vb = v.astype(BF16)
        for r in range(dil):
            rows = slice(r * per, (r + 1) * per)
            q_refs[g][r, :, hl] = qn[rows]
            k_refs[g][r, :, hl] = knb[rows]
            v_refs[g][r, :, hl] = vb[rows]
        if dil == 1:
            kvp_refs[g][:, hl] = kn[tm - keep:, :]
            kvp_refs[g][:, vl] = v[tm - keep:, :]
        else:
            _scatter_residues(kv_ref, kn, dil, first_slab=hl.start // LANES)
            _scatter_residues(kv_ref, v, dil, first_slab=vl.start // LANES)
            if half == GROUP_WIDTH // MXU_DIM - 1:
                kvp_refs[g][...] = _from_slabs(kv_ref)

    def att_gate(t):
        sl = slice(t * RG_TILE, (t + 1) * RG_TILE)
        al = slice(D_MODEL + t * RG_TILE, D_MODEL + (t + 1) * RG_TILE)
        gatt_ref[:, sl] = _sigmoid((_dot(xb_ref[...], wmg_ref[:, al]) + bmg_ref[:, al]).astype(gatt_ref.dtype))

    for t in range(n_tiles):
        lru_gates(t)
        if t < N_GROUPS:
            attn_operands(t, 0)
        else:
            att_gate(0)
            att_gate(1)
        lru_scan(t)
        if t < N_GROUPS:
            attn_operands(t, 1)
        else:
            att_gate(2)
            att_gate(3)
    hp_ref[...] = h_ref[...]
    for t in range(n_tiles):
        sl = slice(t * RG_TILE, (t + 1) * RG_TILE)
        g_lru = _sigmoid((_dot(xb_ref[...], wmg_ref[:, sl]) + bmg_ref[:, sl]).astype(ylg_ref.dtype))
        ylg_ref[:, sl] = g_lru * _dot(y_ref[...], wbl_ref[:, sl]).astype(ylg_ref.dtype)


def _front_call(x, lw, layer, tm):
    B, S, _ = x.shape
    nt = S // tm
    row_spec = lambda width: pl.BlockSpec((None, tm, width), lambda b, i: (b, i, 0))
    weights = [lw[n] for n in FRONT_WEIGHTS]
    in_specs = ([row_spec(D_MODEL)] + [_layer_spec(w, layer) for w in weights]
                + [_const_spec(lw["pnorm"].shape)])

    out_shape, out_specs = [], []
    for _ in range(3):
        for _, dil in DILATED_GROUPS:
            out_shape.append(jax.ShapeDtypeStruct((B, dil, S // dil, GROUP_WIDTH), BF16))
            out_specs.append(pl.BlockSpec((None, dil, tm // dil, GROUP_WIDTH), lambda b, i: (b, 0, i, 0)))
    out_shape.append(jax.ShapeDtypeStruct((B, S, D_MODEL), BF16))
    out_specs.append(row_spec(D_MODEL))
    out_shape.append(jax.ShapeDtypeStruct((B, S, D_MODEL), BF16))
    out_specs.append(row_spec(D_MODEL))
    for window, _ in DILATED_GROUPS:
        keep = min(window, tm)
        first = nt - window // keep
        out_shape.append(jax.ShapeDtypeStruct((B, window, 2 * GROUP_WIDTH), F32))
        out_specs.append(pl.BlockSpec(
            (None, keep, 2 * GROUP_WIDTH),
            functools.partial(lambda b, i, first: (b, jnp.maximum(i - first, 0), 0), first=first)))
    for _ in range(2):
        out_shape.append(jax.ShapeDtypeStruct((B, SUBLANES, D_RNN), F32))
        out_specs.append(pl.BlockSpec((None, SUBLANES, D_RNN), lambda b, i: (b, 0, 0)))

    return pl.pallas_call(
        functools.partial(_front_kernel, tm=tm),
        grid=(B, nt),
        in_specs=in_specs,
        out_specs=out_specs,
        out_shape=out_shape,
        scratch_shapes=[pltpu.VMEM((D_MODEL // LANES, tm, LANES), F32),
                        pltpu.VMEM((2 * GROUP_WIDTH // LANES, tm, LANES), F32),
                        pltpu.VMEM((tm + SUBLANES, D_RNN), F32), pltpu.VMEM((SUBLANES, D_RNN), F32),
                        pltpu.VMEM((tm, D_MODEL), BF16), pltpu.VMEM((tm, D_MODEL), BF16),
                        pltpu.VMEM((tm, D_RNN), F32), pltpu.VMEM((tm, D_RNN), BF16),
                        pltpu.VMEM((tm, D_RNN), F32), pltpu.VMEM((tm, D_RNN), BF16)],
        compiler_params=pltpu.CompilerParams(
            dimension_semantics=("arbitrary", "arbitrary"), vmem_limit_bytes=VMEM_LIMIT_BYTES),
        name="prompt_front",
    )(x, *weights, lw["pnorm"])


def _attn_kernel(tab_ref, bkt_ref, q_ref, kp_ref, kc_ref, vp_ref, vc_ref, o_ref, lse_ref,
                 bias_ref, kbuf_ref, vbuf_ref, *, qblocks):
    b, r, j = pl.program_id(0), pl.program_id(1), pl.program_id(2)
    BB = BAND_BLOCK
    PAIR = 2 * HEAD_DIM

    @pl.when((b == 0) & (r == 0) & (j == 0))
    def _():
        bkt = bkt_ref[...]
        for h in range(HEADS_PER_GROUP):
            acc = jnp.full(bkt.shape, -jnp.inf, F32)
            for n in range(NUM_BUCKETS):
                acc = jnp.where(bkt == n, tab_ref[n, h], acc)
            bias_ref[h // 2, (h % 2) * BB:(h % 2 + 1) * BB, :] = acc

    first_col = lax.broadcasted_iota(jnp.int32, (1, 2 * BB), 1) < BB
    pen = jnp.where(first_col & (j == 0), -jnp.inf, 0.0).astype(F32)
    low_half = lax.broadcasted_iota(jnp.int32, (BB, PAIR), 1) < HEAD_DIM
    ones = jnp.ones((2 * BB, PAIR), BF16)
    for rr in range(q_ref.shape[0]):
        kbuf_ref[0:BB, :] = kp_ref[rr]
        kbuf_ref[BB:, :] = kc_ref[rr]
        vbuf_ref[0:BB, :] = vp_ref[rr]
        vbuf_ref[BB:, :] = vc_ref[rr]
        for t in range(qblocks):
            rows = slice(t * BB, (t + 1) * BB)
            keys = slice(t * BB, (t + 2) * BB)
            for hp in range(HEADS_PER_GROUP // 2):
                sl = slice(hp * PAIR, (hp + 1) * PAIR)
                qp = q_ref[rr, rows, sl]
                zero = jnp.zeros_like(qp)
                lhs = jnp.concatenate([jnp.where(low_half, qp, zero), jnp.where(low_half, zero, qp)], axis=0)
                s = _dot_nt(lhs, kbuf_ref[keys, sl]) + bias_ref[hp]
                if t == 0:
                    s = s + pen
                m = jnp.max(s, axis=-1, keepdims=True)
                p = jnp.exp((s - m).astype(BF16))
                oe = _dot(p, jnp.concatenate([vbuf_ref[keys, sl], ones], axis=1))
                l = oe[:, PAIR:]
                o = oe[:, :PAIR] / l
                lse = m + jnp.log(l)
                for e in range(2):
                    hl = slice(hp * PAIR + e * HEAD_DIM, hp * PAIR + (e + 1) * HEAD_DIM)
                    o_ref[rr, rows, hl] = o[e * BB:(e + 1) * BB, e * HEAD_DIM:(e + 1) * HEAD_DIM]
                    lse_ref[rr, rows, hl] = lse[e * BB:(e + 1) * BB, e * HEAD_DIM:(e + 1) * HEAD_DIM]


def _t5_bucket(dist):
    max_exact = NUM_BUCKETS // 2
    d_f = np.maximum(dist, 1).astype(np.float32)
    large = max_exact + (np.log(d_f / np.float32(max_exact)) / np.float32(math.log(MAX_DISTANCE / max_exact))
                         * np.float32(NUM_BUCKETS - max_exact)).astype(np.int32)
    large = np.minimum(large, NUM_BUCKETS - 1)
    return np.where(dist < max_exact, dist, large).astype(np.int32)


def _band_buckets(dil):
    BB = BAND_BLOCK
    qi = np.arange(BB)[:, None]
    ki = np.arange(2 * BB)[None, :]
    sub = qi + BB - ki
    in_band = (sub >= 0) & (sub <= BB)
    return jnp.asarray(np.where(in_band, _t5_bucket(np.clip(sub, 0, BB) * dil), -1).astype(np.int32))


def _attn_call(q, k, v, tab, dil):
    B, _, n, C = q.shape
    BB = BAND_BLOCK
    qb = min(ATTN_QBLOCKS, n // BB)
    res = ATTN_QBLOCKS // qb
    cur = pl.BlockSpec((None, res, qb * BB, C), lambda b, r, j: (b, r, j, 0))
    prev = pl.BlockSpec((None, res, BB, C), lambda b, r, j: (b, r, jnp.maximum(j * qb - 1, 0), 0))
    return pl.pallas_call(
        functools.partial(_attn_kernel, qblocks=qb),
        grid=(B, dil // res, n // (qb * BB)),
        in_specs=[pl.BlockSpec(memory_space=pltpu.SMEM),
                  pl.BlockSpec((BB, 2 * BB), lambda b, r, j: (0, 0)),
                  cur, prev, cur, prev, cur],
        out_specs=[cur, cur],
        out_shape=[jax.ShapeDtypeStruct(q.shape, F32)] * 2,
        scratch_shapes=[pltpu.VMEM((HEADS_PER_GROUP // 2, 2 * BB, 2 * BB), F32),
                        pltpu.VMEM(((qb + 1) * BB, C), BF16), pltpu.VMEM(((qb + 1) * BB, C), BF16)],
        compiler_params=pltpu.CompilerParams(
            dimension_semantics=("arbitrary", "arbitrary", "arbitrary"), vmem_limit_bytes=VMEM_LIMIT_BYTES),
        name="prompt_attn_d%d" % dil,
    )(tab, _band_buckets(dil), q, k, k, v, v)


def _back_kernel(x_ref, ylg_ref, gatt_ref, o0_ref, o1_ref, o2_ref, l0_ref, l1_ref, l2_ref,
                 wba_ref, wo_ref, nf_ref, wfi_ref, wfo_ref,
                 qt_ref, kt_ref, vt_ref, c0_ref, c1_ref, c2_ref, tabt_ref, bkt0_ref, bkt1_ref, bkt2_ref,
                 y_ref, ot_ref,
                 so1_ref, so2_ref, sl1_ref, sl2_ref, bias0_ref, bias1_ref, bias2_ref, s0_ref, s1_ref, s2_ref,
                 *, tm, dec_parts):
    step = pl.program_id(0) * pl.num_programs(1) + pl.program_id(1)
    bias_refs = (bias0_ref, bias1_ref, bias2_ref)
    s_refs = (s0_ref, s1_ref, s2_ref)

    @pl.when(step == 0)
    def _():
        _decode_attention_init(tabt_ref, (bkt0_ref, bkt1_ref, bkt2_ref), bias_refs, s_refs, ot_ref)

    _decode_attention_step(step // dec_parts, step % dec_parts, HEADS_PER_GROUP // dec_parts,
                           qt_ref, kt_ref, vt_ref, (c0_ref, c1_ref, c2_ref), tabt_ref, bias_refs, s_refs, ot_ref)

    for (_, dil), src, dst in ((DILATED_GROUPS[1], o1_ref, so1_ref), (DILATED_GROUPS[2], o2_ref, so2_ref),
                               (DILATED_GROUPS[1], l1_ref, sl1_ref), (DILATED_GROUPS[2], l2_ref, sl2_ref)):
        _scatter_residues(dst, src[...].reshape(tm, GROUP_WIDTH), dil)
    os_ = (o0_ref[0], _from_slabs(so1_ref), _from_slabs(so2_ref))
    lses = (l0_ref[0], _from_slabs(sl1_ref), _from_slabs(sl2_ref))

    top = jnp.maximum(jnp.maximum(lses[0], lses[1]), lses[2])
    num = 0.0
    den = 0.0
    for o, lse in zip(os_, lses):
        z = jnp.exp(lse - top)
        num = num + z * o
        den = den + z
    o = num / den

    y_att = _dot(o.astype(BF16), wba_ref[...])
    mix = ylg_ref[...] + gatt_ref[...] * y_att
    x1 = x_ref[...] + _dot(mix.astype(BF16), wo_ref[...])
    y_ref[...] = _ffn(x1, _rms_rows(x1, nf_ref[...]).astype(BF16), wfi_ref, wfo_ref)


def _back_call(x, ylg, gatt, os_, lses, lw, layer, tm, qt, kt, vt, caches_t, rel_bias):
    B, S, _ = x.shape
    nt = S // tm
    nb = qt.shape[1]
    dec_parts = (B * nt) // nb
    assert dec_parts * nb == B * nt and HEADS_PER_GROUP % dec_parts == 0
    n_heads = HEADS_PER_GROUP // dec_parts
    tabt, bkts = _decode_attention_operands(rel_bias, dec_parts)

    row_spec = lambda width: pl.BlockSpec((None, tm, width), lambda b, i: (b, i, 0))
    res_specs = [pl.BlockSpec((None, dil, tm // dil, GROUP_WIDTH), lambda b, i: (b, 0, i, 0))
                 for _, dil in DILATED_GROUPS]
    cache_spec = lambda c: pl.BlockSpec(
        (None, None, 2, n_heads) + c.shape[4:],
        lambda b, i: (layer, (b * nt + i) // dec_parts, 0, (b * nt + i) % dec_parts, 0, 0))
    weights = [lw[n] for n in BACK_WEIGHTS]
    in_specs = ([row_spec(D_MODEL)] * 3 + res_specs * 2 + [_layer_spec(w, layer) for w in weights]
                + [_const_spec(qt.shape)] * 3 + [cache_spec(c) for c in caches_t]
                + [_const_spec(tabt.shape)] + [_const_spec(b.shape) for b in bkts])
    return pl.pallas_call(
        functools.partial(_back_kernel, tm=tm, dec_parts=dec_parts),
        grid=(B, nt),
        in_specs=in_specs,
        out_specs=[row_spec(D_MODEL), pl.BlockSpec((GROUP_WIDTH, nb), lambda b, i: (0, 0))],
        out_shape=[jax.ShapeDtypeStruct((B, S, D_MODEL), F32), jax.ShapeDtypeStruct((GROUP_WIDTH, nb), F32)],
        scratch_shapes=[pltpu.VMEM((GROUP_WIDTH // LANES, tm, LANES), F32)] * 4
        + [pltpu.VMEM((dec_parts, SUBLANES, window), F32) for window, _ in DILATED_GROUPS]
        + [pltpu.VMEM((SUBLANES, window), F32) for window, _ in DILATED_GROUPS],
        compiler_params=pltpu.CompilerParams(
            dimension_semantics=("arbitrary", "arbitrary"), vmem_limit_bytes=VMEM_LIMIT_BYTES),
        name="prompt_back",
    )(x, ylg, gatt, *os_, *lses, *weights, qt, kt, vt, *caches_t, tabt, *bkts)


def _dec_front_kernel(x_ref, c0_ref, c1_ref, c2_ref, h0_ref, nm_ref, win_ref, wmg_ref, bmg_ref,
                      wconv_ref, bconv_ref, wrg_ref, brga_ref, brgx_ref, lam_ref, qg_ref, kg_ref,
                      wbl_ref, pn_ref,
                      q_ref, k_ref, v_ref, ylg_ref, gatt_ref, lrux_ref, hs_ref):
    xb = _rms_rows(x_ref[...], nm_ref[...]).astype(BF16)
    lru_x = _dot(xb, win_ref[:, 0:D_RNN])
    lrux_ref[...] = lru_x
    xc = bconv_ref[...] + wconv_ref[0:1, :] * c0_ref[...]
    xc = xc + wconv_ref[1:2, :] * c1_ref[...]
    xc = xc + wconv_ref[2:3, :] * c2_ref[...]
    xc = xc + wconv_ref[3:4, :] * lru_x

    lru_g = _dot(xb, win_ref[:, D_RNN:2 * D_RNN])
    xcb = xc.astype(BF16)
    sp = _softplus(-lam_ref[...])
    ys = []
    for t in range(D_RNN // RG_TILE):
        sl = slice(t * RG_TILE, (t + 1) * RG_TILE)
        g = _dot(xcb[:, sl], wrg_ref[t])
        a, mult, ig = _lru_gates(g, brga_ref[:, sl], brgx_ref[:, sl], sp[:, sl])
        h = a * h0_ref[:, sl] + (mult * ig) * xc[:, sl]
        hs_ref[:, sl] = h
        ys.append((h * _gelu_tanh(lru_g[:, sl])).astype(BF16))
    y_lru = _dot(jnp.concatenate(ys, axis=1), wbl_ref[...])
    g_lru = _sigmoid(_dot(xb, wmg_ref[:, 0:D_MODEL]) + bmg_ref[:, 0:D_MODEL])
    ylg_ref[...] = g_lru * y_lru
    gatt_ref[...] = _sigmoid(_dot(xb, wmg_ref[:, D_MODEL:2 * D_MODEL]) + bmg_ref[:, D_MODEL:2 * D_MODEL])

    pn = pn_ref[...]
    for g in range(N_GROUPS):
        c = 2 * D_RNN + g * GROUP_WIDTH
        sl = slice(g * GROUP_WIDTH, (g + 1) * GROUP_WIDTH)
        q = _dot(xb, win_ref[:, c:c + GROUP_WIDTH])
        k = _dot(xb, win_ref[:, c + ATT_WIDTH:c + ATT_WIDTH + GROUP_WIDTH])
        v = _dot(xb, win_ref[:, c + 2 * ATT_WIDTH:c + 2 * ATT_WIDTH + GROUP_WIDTH])
        q_ref[:, sl] = _head_rms(q, pn, qg_ref[...])
        k_ref[:, sl] = _head_rms(k, pn, kg_ref[...])
        v_ref[:, sl] = v


def _dec_front_call(x, conv_rows, h0, lw, layer):
    nb = x.shape[0]
    weights = [lw[n] for n in FRONT_WEIGHTS]
    acts = (x,) + tuple(conv_rows) + (h0,)
    widths = (ATT_WIDTH, ATT_WIDTH, ATT_WIDTH, D_MODEL, D_MODEL, D_RNN, D_RNN)
    return pl.pallas_call(
        _dec_front_kernel,
        grid=(1,),
        in_specs=([_const_spec(a.shape) for a in acts] + [_layer_spec(w, layer) for w in weights]
                  + [_const_spec(lw["pnorm"].shape)]),
        out_specs=[pl.BlockSpec((nb, w), lambda i: (0, 0)) for w in widths],
        out_shape=[jax.ShapeDtypeStruct((nb, w), F32) for w in widths],
        compiler_params=pltpu.CompilerParams(
            dimension_semantics=("arbitrary",), vmem_limit_bytes=VMEM_LIMIT_BYTES),
        name="decode_front",
    )(*acts, *weights, lw["pnorm"])


def _decode_attention_init(tabt_ref, bkt_refs, bias_refs, s_refs, ot_ref):
    ot_ref[...] = jnp.zeros(ot_ref.shape, F32)
    for g in range(N_GROUPS):
        s_refs[g][...] = jnp.zeros(s_refs[g].shape, F32)
        bkt = bkt_refs[g][...]
        for part in range(bias_refs[g].shape[0]):
            acc = jnp.full(bkt.shape, -jnp.inf, F32)
            for n in range(NUM_BUCKETS):
                acc = jnp.where(bkt == n, tabt_ref[g, part, :, n:n + 1], acc)
            bias_refs[g][part] = acc


def _decode_attention_step(seq, part, n_heads, qt_ref, kt_ref, vt_ref, c_refs, tabt_ref, bias_refs, s_refs,
                           ot_ref):
    mine = lax.broadcasted_iota(jnp.int32, (HEAD_DIM, qt_ref.shape[1]), 1) == seq

    def column(ref, g, h):
        row0 = pl.multiple_of(g * GROUP_WIDTH + (part * n_heads + h) * HEAD_DIM, HEAD_DIM)
        return jnp.sum(jnp.where(mine, ref[pl.ds(row0, HEAD_DIM), :], 0.0), axis=1, keepdims=True)

    def fold_lanes(t, op):
        out = t[:, 0:LANES]
        for c in range(1, t.shape[1] // LANES):
            out = op(out, t[:, c * LANES:(c + 1) * LANES])
        return out

    pad = [jnp.zeros((SUBLANES - n_heads, 1), F32)] if n_heads < SUBLANES else []
    s0, vcols = [], []
    for g in range(N_GROUPS):
        rows = []
        for h in range(n_heads):
            qh = column(qt_ref, g, h)
            s_refs[g][h:h + 1, :] = jnp.sum(c_refs[g][0, h] * qh, axis=0, keepdims=True)
            rows.append(jnp.sum(qh * column(kt_ref, g, h), axis=0, keepdims=True))
            vcols.append(column(vt_ref, g, h))
        s0.append(jnp.concatenate(rows + pad, axis=0) + tabt_ref[g, part, :, 0:1])

    ss = [s_refs[g][...] + bias_refs[g][part] for g in range(N_GROUPS)]
    m_max = jnp.maximum(jnp.maximum(s0[0], s0[1]), s0[2])
    for s in ss:
        m_max = jnp.maximum(m_max, jnp.max(fold_lanes(s, jnp.maximum), axis=1, keepdims=True))
    p0 = [jnp.exp(s - m_max) for s in s0]
    den = p0[0] + p0[1] + p0[2]
    for g, s in enumerate(ss):
        p = jnp.exp(s - m_max)
        s_refs[g][...] = p
        den = den + jnp.sum(fold_lanes(p, jnp.add), axis=1, keepdims=True)
    inv_den = 1.0 / den

    for h in range(n_heads):
        acc = None
        new = 0.0
        for g in range(N_GROUPS):
            f = fold_lanes(c_refs[g][1, h] * s_refs[g][h:h + 1, :], jnp.add)
            acc = f if acc is None else acc + f
            new = new + p0[g][h:h + 1, :] * vcols[g * n_heads + h]
        num = jnp.sum(acc, axis=1, keepdims=True) + new
        row0 = pl.multiple_of((part * n_heads + h) * HEAD_DIM, HEAD_DIM)
        ot_ref[pl.ds(row0, HEAD_DIM), :] = jnp.where(mine, num * inv_den[h:h + 1, :],
                                                      ot_ref[pl.ds(row0, HEAD_DIM), :])


def _decode_attention_operands(rel_bias, parts):
    n_heads = HEADS_PER_GROUP // parts
    tab = rel_bias.reshape(NUM_BUCKETS, N_GROUPS, parts, n_heads).transpose(1, 2, 3, 0)
    tab = jnp.pad(tab, ((0, 0), (0, 0), (0, SUBLANES - n_heads), (0, 0)))
    bkts = []
    for window, dil in DILATED_GROUPS:
        w = np.arange(window)
        bkt = np.where(w % dil == 0, _t5_bucket(window - w), -1).astype(np.int32)
        bkts.append(jnp.asarray(np.broadcast_to(bkt[None, :], (SUBLANES, window))))
    return tab, bkts


def _dec_back_kernel(x_ref, ylg_ref, gatt_ref, o_ref, wba_ref, wo_ref, nf_ref, wfi_ref, wfo_ref, y_ref):
    y_att = _dot(o_ref[...].astype(BF16), wba_ref[...])
    mix = ylg_ref[...] + gatt_ref[...] * y_att
    x1 = x_ref[...] + _dot(mix.astype(BF16), wo_ref[...])
    y_ref[...] = _ffn(x1, _rms_rows(x1, nf_ref[...]).astype(BF16), wfi_ref, wfo_ref)


def _dec_back_call(x, ylg, gatt, o, lw, layer):
    weights = [lw[n] for n in BACK_WEIGHTS]
    acts = (x, ylg, gatt, o)
    return pl.pallas_call(
        _dec_back_kernel,
        grid=(1,),
        in_specs=[_const_spec(a.shape) for a in acts] + [_layer_spec(w, layer) for w in weights],
        out_specs=pl.BlockSpec(x.shape, lambda i: (0, 0)),
        out_shape=jax.ShapeDtypeStruct(x.shape, F32),
        compiler_params=pltpu.CompilerParams(
            dimension_semantics=("arbitrary",), vmem_limit_bytes=VMEM_LIMIT_BYTES),
        name="decode_back",
    )(*acts, *weights)


def _prepare_weights(norm_mix, w_in, w_conv, b_conv, w_rg_a, b_rg_a, w_rg_x, b_rg_x, lru_lambda,
                     q_gain, k_gain, w_merge, b_merge, w_branch_lru, w_branch_att, w_o, norm_ffn,
                     w_ffn_in, w_ffn_out):
    depth = norm_mix.shape[0]
    row = lambda t: t.reshape(depth, 1, -1).astype(F32)
    per_tile = RG_TILE // LRU_BLOCK
    tiles = D_RNN // RG_TILE

    def block_diag(w):
        w = w.reshape(depth, tiles, per_tile, LRU_BLOCK, LRU_BLOCK)
        eye = jnp.eye(per_tile, dtype=w.dtype)
        return jnp.einsum("ltnij,nm->ltnimj", w, eye).reshape(depth, tiles, RG_TILE, RG_TILE)

    head = jnp.arange(MXU_DIM) // HEAD_DIM
    tile_gain = lambda t: jnp.tile(t.reshape(depth, 1, HEAD_DIM), (1, 1, HEADS_PER_GROUP)).astype(F32)
    return dict(
        norm_mix=row(norm_mix), w_in=w_in.astype(BF16), w_merge=w_merge.astype(BF16),
        b_merge=row(b_merge), w_conv=w_conv.astype(F32), b_conv=row(b_conv),
        w_rg=jnp.concatenate([block_diag(w_rg_a), block_diag(w_rg_x)], axis=3).astype(BF16),
        b_rg_a=row(b_rg_a), b_rg_x=row(b_rg_x), lam=row(lru_lambda),
        q_gain=tile_gain(q_gain) * (HEAD_DIM ** -0.5),
        k_gain=tile_gain(k_gain),
        w_branch_lru=w_branch_lru.astype(BF16),
        pnorm=((head[:, None] == head[None, :]).astype(F32) / HEAD_DIM).astype(BF16),
        w_branch_att=w_branch_att.astype(BF16), w_o=w_o.astype(BF16), norm_ffn=row(norm_ffn),
        w_ffn_in=w_ffn_in.astype(BF16), w_ffn_out=w_ffn_out.astype(BF16))


def kernel(x_prompt, x_sample, cache_kv_g0, cache_kv_g1, cache_kv_g2, state_conv, state_h, rel_bias,
           norm_mix, w_in, w_conv, b_conv, w_rg_a, b_rg_a, w_rg_x, b_rg_x, lru_lambda, q_gain, k_gain,
           w_merge, b_merge, w_branch_lru, w_branch_att, w_o, norm_ffn, w_ffn_in, w_ffn_out):
    B, S, _ = x_prompt.shape
    nb = x_sample.shape[0]
    depth = norm_mix.shape[0]
    lw = _prepare_weights(norm_mix, w_in, w_conv, b_conv, w_rg_a, b_rg_a, w_rg_x, b_rg_x, lru_lambda,
                          q_gain, k_gain, w_merge, b_merge, w_branch_lru, w_branch_att, w_o, norm_ffn,
                          w_ffn_in, w_ffn_out)
    rel_bias = rel_bias.astype(F32)
    caches_t = [jnp.transpose(c.astype(F32), (0, 1, 3, 4, 5, 2))
                for c in (cache_kv_g0, cache_kv_g1, cache_kv_g2)]

    yp = x_prompt
    ys = x_sample.reshape(nb, D_MODEL)
    kvp = [[] for _ in range(N_GROUPS)]
    kvs = [[] for _ in range(N_GROUPS)]
    conv_p, h_p, conv_s, h_s = [], [], [], []
    for l in range(depth):
        sc = state_conv[l].astype(F32)
        conv_rows = [sc[:, r, :] for r in range(CONV_WIDTH - 1)]
        qs, ks, vs, ylg_s, gatt_s, lrux_s, hs_s = _dec_front_call(ys, conv_rows, state_h[l].astype(F32), lw, l)

        (q0, q1, q2, k0, k1, k2, v0, v1, v2, ylg, gatt, kv0, kv1, kv2, ctail, hfin) = _front_call(
            yp, lw, l, TM_PROMPT)
        os_, lses = [], []
        for g, (qg, kg, vg) in enumerate(((q0, k0, v0), (q1, k1, v1), (q2, k2, v2))):
            tab = rel_bias[:, g * HEADS_PER_GROUP:(g + 1) * HEADS_PER_GROUP]
            o, lse = _attn_call(qg, kg, vg, tab, DILATED_GROUPS[g][1])
            os_.append(o)
            lses.append(lse)
        yp, o_t = _back_call(yp, ylg, gatt, os_, lses, lw, l, TM_PROMPT, qs.T, ks.T, vs.T, caches_t, rel_bias)
        for g, kv in enumerate((kv0, kv1, kv2)):
            kvp[g].append(kv.reshape(B, kv.shape[1], 2, HEADS_PER_GROUP, HEAD_DIM))
        conv_p.append(ctail[:, SUBLANES - (CONV_WIDTH - 1):, :])
        h_p.append(hfin[:, 0, :])

        ys = _dec_back_call(ys, ylg_s, gatt_s, o_t.T, lw, l)
        for g in range(N_GROUPS):
            sl = slice(g * GROUP_WIDTH, (g + 1) * GROUP_WIDTH)
            kvs[g].append(jnp.stack([ks[:, sl], vs[:, sl]], axis=1).reshape(
                nb, 1, 2, HEADS_PER_GROUP, HEAD_DIM))
        conv_s.append(jnp.stack(conv_rows[1:] + [lrux_s], axis=1))
        h_s.append(hs_s)

    stack = jnp.stack
    return (yp, ys.reshape(nb, 1, D_MODEL),
            stack(kvp[0]), stack(kvp[1]), stack(kvp[2]), stack(conv_p), stack(h_p),
            stack(kvs[0]), stack(kvs[1]), stack(kvs[2]), stack(conv_s), stack(h_s))
```

```python
import functools
import math

import numpy as np
import jax
import jax.numpy as jnp
from jax import lax
from jax.experimental import pallas as pl
from jax.experimental.pallas import tpu as pltpu

D_MODEL = 1024
D_RNN = D_MODEL
N_LRU_BLOCKS = 16
LRU_BLOCK = D_RNN // N_LRU_BLOCKS
CONV_WIDTH = 4
LRU_C = 8.0
HEAD_DIM = 64
HEADS_PER_GROUP = 8
DILATED_GROUPS = ((128, 1), (512, 4), (2048, 16))
N_GROUPS = len(DILATED_GROUPS)
N_ATT_HEADS = N_GROUPS * HEADS_PER_GROUP
ATT_WIDTH = N_ATT_HEADS * HEAD_DIM
GROUP_WIDTH = HEADS_PER_GROUP * HEAD_DIM
BAND_BLOCK = 128
NUM_BUCKETS = 32
MAX_DISTANCE = 2048
D_FF = 2816
EPS = 1e-6

F32 = jnp.float32
BF16 = jnp.bfloat16

MXU_DIM = 256
SUBLANES = 8
LANES = 128
VMEM_LIMIT_BYTES = 56 * 1024 * 1024

TM_PROMPT = 256
ATTN_QBLOCKS = 16
RG_TILE = MXU_DIM
FFN_CHUNKS = ((0, 768), (768, 768), (1536, 768), (2304, 512))


def _dot(a, b):
    return jnp.dot(a, b, preferred_element_type=F32)


def _dot_nt(a, b):
    return lax.dot_general(a, b, (((1,), (1,)), ((), ())), preferred_element_type=F32)


def _sigmoid(x):
    return 0.5 * (jnp.tanh(0.5 * x) + 1.0)


def _gelu_tanh(x):
    c = math.sqrt(2.0 / math.pi)
    return 0.5 * x * (1.0 + jnp.tanh(c * (x + 0.044715 * (x * x * x))))


def _softplus(z):
    return jnp.maximum(z, 0.0) + jnp.log1p(jnp.exp(-jnp.abs(z)))


def _rms_rows(x, g):
    y = x * lax.rsqrt(jnp.mean(x * x, axis=-1, keepdims=True) + EPS)
    return y * g


def _head_rms(t, pn, gain):
    t2 = (t * t).astype(BF16)
    tiles = [_dot(t2[:, c * MXU_DIM:(c + 1) * MXU_DIM], pn) for c in range(t.shape[1] // MXU_DIM)]
    ms = tiles[0] if len(tiles) == 1 else jnp.concatenate(tiles, axis=1)
    return (t * lax.rsqrt(ms + EPS)) * gain


def _lru_gates(g, b_a, b_x, sp):
    w = g.shape[1] // 2
    r = _sigmoid(g[:, :w] + b_a)
    ig = _sigmoid(g[:, w:] + b_x)
    log_a = (-LRU_C * r) * sp
    a = jnp.exp(log_a)
    mult = jnp.sqrt(-jnp.tanh(log_a) * (a * a + 1.0))
    return a, mult, ig


def _scan_rows(a, u, h_in):
    rows, c = a.shape
    groups = rows // SUBLANES
    a3 = a.reshape(groups, SUBLANES, c)
    u3 = u.reshape(groups, SUBLANES, c)
    row = lax.broadcasted_iota(jnp.int32, (groups, SUBLANES, c), 1)
    shift = 1
    while shift < SUBLANES:
        ok = row >= shift
        a_sh = jnp.where(ok, pltpu.roll(a3, shift, axis=1), 1.0)
        u_sh = jnp.where(ok, pltpu.roll(u3, shift, axis=1), 0.0)
        u3 = u3 + a3 * u_sh
        a3 = a3 * a_sh
        shift *= 2
    out = []
    h = h_in
    for g in range(groups):
        hg = a3[g] * h + u3[g]
        out.append(hg)
        h = hg[SUBLANES - 1:SUBLANES, :]
    return jnp.concatenate(out, axis=0), h


def _to_slabs(slab_ref, t):
    for c in range(slab_ref.shape[0]):
        slab_ref[c] = t[:, c * LANES:(c + 1) * LANES]


def _from_slabs(slab_ref):
    return jnp.concatenate([slab_ref[c] for c in range(slab_ref.shape[0])], axis=1)


def _gather_residues(slab_ref, dil):
    per = slab_ref.shape[1] // dil
    cols = []
    for c in range(slab_ref.shape[0]):
        cols.append(jnp.concatenate(
            [slab_ref[c, pl.ds(r, per, stride=dil), :] for r in range(dil)], axis=0))
    return jnp.concatenate(cols, axis=1)


def _scatter_residues(slab_ref, t, dil, first_slab=0):
    per = slab_ref.shape[1] // dil
    for c in range(t.shape[1] // LANES):
        for r in range(dil):
            slab_ref[first_slab + c, pl.ds(r, per, stride=dil), :] = (
                t[r * per:(r + 1) * per, c * LANES:(c + 1) * LANES])


def _ffn(x1, xb, wfi_ref, wfo_ref):
    acc = x1
    for start, width in FFN_CHUNKS:
        gate = _dot(xb, wfi_ref[:, start:start + width])
        up = _dot(xb, wfi_ref[:, D_FF + start:D_FF + start + width])
        hid = (gate * _sigmoid(gate)) * up
        acc = acc + _dot(hid.astype(BF16), wfo_ref[start:start + width, :])
    return acc


def _layer_spec(w, layer):
    tail = (0,) * (w.ndim - 1)
    return pl.BlockSpec((None,) + w.shape[1:], lambda *_: (layer,) + tail, pipeline_mode=pl.Buffered(1))


def _const_spec(shape):
    zeros = (0,) * len(shape)
    return pl.BlockSpec(shape, lambda *_: zeros, pipeline_mode=pl.Buffered(1))


FRONT_WEIGHTS = ("norm_mix", "w_in", "w_merge", "b_merge", "w_conv", "b_conv", "w_rg", "b_rg_a", "b_rg_x",
                 "lam", "q_gain", "k_gain", "w_branch_lru")
BACK_WEIGHTS = ("w_branch_att", "w_o", "norm_ffn", "w_ffn_in", "w_ffn_out")


def _front_kernel(x_ref, nm_ref, win_ref, wmg_ref, bmg_ref, wconv_ref, bconv_ref, wrg_ref,
                  brga_ref, brgx_ref, lam_ref, qg_ref, kg_ref, wbl_ref, pn_ref,
                  q0_ref, q1_ref, q2_ref, k0_ref, k1_ref, k2_ref, v0_ref, v1_ref, v2_ref,
                  ylg_ref, gatt_ref, kvp0_ref, kvp1_ref, kvp2_ref, convp_ref, hp_ref,
                  xn_ref, kv_ref, ext_ref, h_ref, xb_ref, xg_ref, xc_ref, xcb_ref, gl_ref, y_ref, *, tm):
    i = pl.program_id(1)
    n_tiles = D_RNN // RG_TILE

    @pl.when(i == 0)
    def _():
        ext_ref[0:SUBLANES, :] = jnp.zeros((SUBLANES, D_RNN), F32)
        h_ref[...] = jnp.zeros((SUBLANES, D_RNN), F32)

    xn = _rms_rows(x_ref[...], nm_ref[...])
    _to_slabs(xn_ref, xn)
    xb_ref[...] = xn.astype(BF16)

    for t in range(n_tiles):
        sl = slice(t * RG_TILE, (t + 1) * RG_TILE)
        lru_x = _dot(xb_ref[...], win_ref[:, sl])
        ext_ref[SUBLANES:SUBLANES + tm, sl] = lru_x
        xc = bconv_ref[:, sl] + wconv_ref[0:1, sl] * ext_ref[SUBLANES - 3:SUBLANES - 3 + tm, sl]
        xc = xc + wconv_ref[1:2, sl] * ext_ref[SUBLANES - 2:SUBLANES - 2 + tm, sl]
        xc = xc + wconv_ref[2:3, sl] * ext_ref[SUBLANES - 1:SUBLANES - 1 + tm, sl]
        xc = xc + wconv_ref[3:4, sl] * lru_x
        xc_ref[:, sl] = xc
        xcb_ref[:, sl] = xc.astype(BF16)
        gl_ref[:, sl] = _gelu_tanh(_dot(xb_ref[...], win_ref[:, D_RNN + t * RG_TILE:D_RNN + (t + 1) * RG_TILE]))
    tail = ext_ref[tm:tm + SUBLANES, :]
    ext_ref[0:SUBLANES, :] = tail
    convp_ref[...] = tail

    sp = _softplus(-lam_ref[...])
    first_row = (lax.broadcasted_iota(jnp.int32, (tm, RG_TILE), 0) == 0) & (i == 0)
    gates = {}

    def lru_gates(t):
        sl = slice(t * RG_TILE, (t + 1) * RG_TILE)
        g = _dot(xcb_ref[:, sl], wrg_ref[t])
        a, mult, ig = _lru_gates(g, brga_ref[:, sl], brgx_ref[:, sl], sp[:, sl])
        mult = jnp.where(first_row, 1.0, mult)
        gates[t] = (a, (mult * ig) * xc_ref[:, sl])

    def lru_scan(t):
        sl = slice(t * RG_TILE, (t + 1) * RG_TILE)
        a, u = gates.pop(t)
        hs, h_last = _scan_rows(a, u, h_ref[0:1, sl])
        h_ref[:, sl] = jnp.broadcast_to(h_last, (SUBLANES, RG_TILE))
        y_ref[:, sl] = (hs * gl_ref[:, sl]).astype(BF16)

    pn = pn_ref[...]
    q_refs = (q0_ref, q1_ref, q2_ref)
    k_refs = (k0_ref, k1_ref, k2_ref)
    v_refs = (v0_ref, v1_ref, v2_ref)
    kvp_refs = (kvp0_ref, kvp1_ref, kvp2_ref)

    def attn_operands(g, half):
        window, dil = DILATED_GROUPS[g]
        per = tm // dil
        keep = min(window, tm)
        if dil == 1:
            src = xb_ref
        else:
            if half == 0:
                xg_ref[...] = _gather_residues(xn_ref, dil).astype(BF16)
            src = xg_ref
        c = 2 * D_RNN + g * GROUP_WIDTH + half * MXU_DIM
        hl = slice(half * MXU_DIM, (half + 1) * MXU_DIM)
        vl = slice(GROUP_WIDTH + half * MXU_DIM, GROUP_WIDTH + (half + 1) * MXU_DIM)
        q = _dot(src[...], win_ref[:, c:c + MXU_DIM])
        k = _dot(src[...], win_ref[:, c + ATT_WIDTH:c + ATT_WIDTH + MXU_DIM])
        v = _dot(src[...], win_ref[:, c + 2 * ATT_WIDTH:c + 2 * ATT_WIDTH + MXU_DIM])
        qn = _head_rms(q, pn, qg_ref[:, hl]).astype(BF16)
        kn = _head_rms(k, pn, kg_ref[:, hl])
        knb = kn.astype(BF16)
        vb = v.astype(BF16)
        for r in range(dil):
            rows = slice(r * per, (r + 1) * per)
            q_refs[g][r, :, hl] = qn[rows]
            k_refs[g][r, :, hl] = knb[rows]
            v_refs[g][r, :, hl] = vb[rows]
        if dil == 1:
            kvp_refs[g][:, hl] = kn[tm - keep:, :]
            kvp_refs[g][:, vl] = v[tm - keep:, :]
        else:
            _scatter_residues(kv_ref, kn, dil, first_slab=hl.start // LANES)
            _scatter_residues(kv_ref, v, dil, first_slab=vl.start // LANES)
            if half == GROUP_WIDTH // MXU_DIM - 1:
                kvp_refs[g][...] = _from_slabs(kv_ref)

    def att_gate(t):
        sl = slice(t * RG_TILE, (t + 1) * RG_TILE)
        al = slice(D_MODEL + t * RG_TILE, D_MODEL + (t + 1) * RG_TILE)
        gatt_ref[:, sl] = _sigmoid(_dot(xb_ref[...], wmg_ref[:, al]) + bmg_ref[:, al]).astype(gatt_ref.dtype)

    for t in range(n_tiles):
        lru_gates(t)
        if t < N_GROUPS:
            attn_operands(t, 0)
        else:
            att_gate(0)
            att_gate(1)
        lru_scan(t)
        if t < N_GROUPS:
            attn_operands(t, 1)
        else:
            att_gate(2)
            att_gate(3)
    hp_ref[...] = h_ref[...]
    for t in range(n_tiles):
        sl = slice(t * RG_TILE, (t + 1) * RG_TILE)
        g_lru = _sigmoid(_dot(xb_ref[...], wmg_ref[:, sl]) + bmg_ref[:, sl])
        ylg_ref[:, sl] = (g_lru * _dot(y_ref[...], wbl_ref[:, sl])).astype(ylg_ref.dtype)


def _front_call(x, lw, layer, tm):
    B, S, _ = x.shape
    nt = S // tm
    row_spec = lambda width: pl.BlockSpec((None, tm, width), lambda b, i: (b, i, 0))
    weights = [lw[n] for n in FRONT_WEIGHTS]
    in_specs = ([row_spec(D_MODEL)] + [_layer_spec(w, layer) for w in weights]
                + [_const_spec(lw["pnorm"].shape)])

    out_shape, out_specs = [], []
    for _ in range(3):
        for _, dil in DILATED_GROUPS:
            out_shape.append(jax.ShapeDtypeStruct((B, dil, S // dil, GROUP_WIDTH), BF16))
            out_specs.append(pl.BlockSpec((None, dil, tm // dil, GROUP_WIDTH), lambda b, i: (b, 0, i, 0)))
    out_shape.append(jax.ShapeDtypeStruct((B, S, D_MODEL), BF16))
    out_specs.append(row_spec(D_MODEL))
    out_shape.append(jax.ShapeDtypeStruct((B, S, D_MODEL), BF16))
    out_specs.append(row_spec(D_MODEL))
    for window, _ in DILATED_GROUPS:
        keep = min(window, tm)
        first = nt - window // keep
        out_shape.append(jax.ShapeDtypeStruct((B, window, 2 * GROUP_WIDTH), F32))
        out_specs.append(pl.BlockSpec(
            (None, keep, 2 * GROUP_WIDTH),
            functools.partial(lambda b, i, first: (b, jnp.maximum(i - first, 0), 0), first=first)))
    for _ in range(2):
        out_shape.append(jax.ShapeDtypeStruct((B, SUBLANES, D_RNN), F32))
        out_specs.append(pl.BlockSpec((None, SUBLANES, D_RNN), lambda b, i: (b, 0, 0)))

    return pl.pallas_call(
        functools.partial(_front_kernel, tm=tm),
        grid=(B, nt),
        in_specs=in_specs,
        out_specs=out_specs,
        out_shape=out_shape,
        scratch_shapes=[pltpu.VMEM((D_MODEL // LANES, tm, LANES), F32),
                        pltpu.VMEM((2 * GROUP_WIDTH // LANES, tm, LANES), F32),
                        pltpu.VMEM((tm + SUBLANES, D_RNN), F32), pltpu.VMEM((SUBLANES, D_RNN), F32),
                        pltpu.VMEM((tm, D_MODEL), BF16), pltpu.VMEM((tm, D_MODEL), BF16),
                        pltpu.VMEM((tm, D_RNN), F32), pltpu.VMEM((tm, D_RNN), BF16),
                        pltpu.VMEM((tm, D_RNN), F32), pltpu.VMEM((tm, D_RNN), BF16)],
        compiler_params=pltpu.CompilerParams(
            dimension_semantics=("arbitrary", "arbitrary"), vmem_limit_bytes=VMEM_LIMIT_BYTES),
        name="prompt_front",
    )(x, *weights, lw["pnorm"])


def _attn_kernel(tab_ref, bkt_ref, q_ref, kp_ref, kc_ref, vp_ref, vc_ref, o_ref, lse_ref,
                 bias_ref, kbuf_ref, vbuf_ref, *, qblocks):
    b, r, j = pl.program_id(0), pl.program_id(1), pl.program_id(2)
    BB = BAND_BLOCK
    PAIR = 2 * HEAD_DIM

    @pl.when((b == 0) & (r == 0) & (j == 0))
    def _():
        bkt = bkt_ref[...]
        for h in range(HEADS_PER_GROUP):
            acc = jnp.full(bkt.shape, -jnp.inf, F32)
            for n in range(NUM_BUCKETS):
                acc = jnp.where(bkt == n, tab_ref[n, h], acc)
            bias_ref[h // 2, (h % 2) * BB:(h % 2 + 1) * BB, :] = acc

    first_col = lax.broadcasted_iota(jnp.int32, (1, 2 * BB), 1) < BB
    pen = jnp.where(first_col & (j == 0), -jnp.inf, 0.0).astype(F32)
    low_half = lax.broadcasted_iota(jnp.int32, (BB, PAIR), 1) < HEAD_DIM
    ones = jnp.ones((2 * BB, PAIR), BF16)
    for rr in range(q_ref.shape[0]):
        kbuf_ref[0:BB, :] = kp_ref[rr]
        kbuf_ref[BB:, :] = kc_ref[rr]
        vbuf_ref[0:BB, :] = vp_ref[rr]
        vbuf_ref[BB:, :] = vc_ref[rr]
        for t in range(qblocks):
            rows = slice(t * BB, (t + 1) * BB)
            keys = slice(t * BB, (t + 2) * BB)
            for hp in range(HEADS_PER_GROUP // 2):
                sl = slice(hp * PAIR, (hp + 1) * PAIR)
                qp = q_ref[rr, rows, sl]
                zero = jnp.zeros_like(qp)
                lhs = jnp.concatenate([jnp.where(low_half, qp, zero), jnp.where(low_half, zero, qp)], axis=0)
                s = _dot_nt(lhs, kbuf_ref[keys, sl]) + bias_ref[hp]
                if t == 0:
                    s = s + pen
                m = jnp.max(s, axis=-1, keepdims=True)
                p = jnp.exp((s - m).astype(BF16))
                oe = _dot(p, jnp.concatenate([vbuf_ref[keys, sl], ones], axis=1))
                l = oe[:, PAIR:]
                o = oe[:, :PAIR] / l
                lse = m + jnp.log(l)
                for e in range(2):
                    hl = slice(hp * PAIR + e * HEAD_DIM, hp * PAIR + (e + 1) * HEAD_DIM)
                    o_ref[rr, rows, hl] = o[e * BB:(e + 1) * BB, e * HEAD_DIM:(e + 1) * HEAD_DIM]
                    lse_ref[rr, rows, hl] = lse[e * BB:(e + 1) * BB, e * HEAD_DIM:(e + 1) * HEAD_DIM]


def _t5_bucket(dist):
    max_exact = NUM_BUCKETS // 2
    d_f = np.maximum(dist, 1).astype(np.float32)
    large = max_exact + (np.log(d_f / np.float32(max_exact)) / np.float32(math.log(MAX_DISTANCE / max_exact))
                         * np.float32(NUM_BUCKETS - max_exact)).astype(np.int32)
    large = np.minimum(large, NUM_BUCKETS - 1)
    return np.where(dist < max_exact, dist, large).astype(np.int32)


def _band_buckets(dil):
    BB = BAND_BLOCK
    qi = np.arange(BB)[:, None]
    ki = np.arange(2 * BB)[None, :]
    sub = qi + BB - ki
    in_band = (sub >= 0) & (sub <= BB)
    return jnp.asarray(np.where(in_band, _t5_bucket(np.clip(sub, 0, BB) * dil), -1).astype(np.int32))


def _attn_call(q, k, v, tab, dil):
    B, _, n, C = q.shape
    BB = BAND_BLOCK
    qb = min(ATTN_QBLOCKS, n // BB)
    res = ATTN_QBLOCKS // qb
    cur = pl.BlockSpec((None, res, qb * BB, C), lambda b, r, j: (b, r, j, 0))
    prev = pl.BlockSpec((None, res, BB, C), lambda b, r, j: (b, r, jnp.maximum(j * qb - 1, 0), 0))
    return pl.pallas_call(
        functools.partial(_attn_kernel, qblocks=qb),
        grid=(B, dil // res, n // (qb * BB)),
        in_specs=[pl.BlockSpec(memory_space=pltpu.SMEM),
                  pl.BlockSpec((BB, 2 * BB), lambda b, r, j: (0, 0)),
                  cur, prev, cur, prev, cur],
        out_specs=[cur, cur],
        out_shape=[jax.ShapeDtypeStruct(q.shape, F32)] * 2,
        scratch_shapes=[pltpu.VMEM((HEADS_PER_GROUP // 2, 2 * BB, 2 * BB), F32),
                        pltpu.VMEM(((qb + 1) * BB, C), BF16), pltpu.VMEM(((qb + 1) * BB, C), BF16)],
        compiler_params=pltpu.CompilerParams(
            dimension_semantics=("arbitrary", "arbitrary", "arbitrary"), vmem_limit_bytes=VMEM_LIMIT_BYTES),
        name="prompt_attn_d%d" % dil,
    )(tab, _band_buckets(dil), q, k, k, v, v)


def _back_kernel(x_ref, ylg_ref, gatt_ref, o0_ref, o1_ref, o2_ref, l0_ref, l1_ref, l2_ref,
                 wba_ref, wo_ref, nf_ref, wfi_ref, wfo_ref,
                 qt_ref, kt_ref, vt_ref, c0_ref, c1_ref, c2_ref, tabt_ref, bkt0_ref, bkt1_ref, bkt2_ref,
                 y_ref, ot_ref,
                 so1_ref, so2_ref, sl1_ref, sl2_ref, bias0_ref, bias1_ref, bias2_ref, s0_ref, s1_ref, s2_ref,
                 *, tm, dec_parts):
    step = pl.program_id(0) * pl.num_programs(1) + pl.program_id(1)
    bias_refs = (bias0_ref, bias1_ref, bias2_ref)
    s_refs = (s0_ref, s1_ref, s2_ref)

    @pl.when(step == 0)
    def _():
        _decode_attention_init(tabt_ref, (bkt0_ref, bkt1_ref, bkt2_ref), bias_refs, s_refs, ot_ref)

    _decode_attention_step(step // dec_parts, step % dec_parts, HEADS_PER_GROUP // dec_parts,
                           qt_ref, kt_ref, vt_ref, (c0_ref, c1_ref, c2_ref), tabt_ref, bias_refs, s_refs, ot_ref)

    for (_, dil), src, dst in ((DILATED_GROUPS[1], o1_ref, so1_ref), (DILATED_GROUPS[2], o2_ref, so2_ref),
                               (DILATED_GROUPS[1], l1_ref, sl1_ref), (DILATED_GROUPS[2], l2_ref, sl2_ref)):
        _scatter_residues(dst, src[...].reshape(tm, GROUP_WIDTH), dil)
    os_ = (o0_ref[0], _from_slabs(so1_ref), _from_slabs(so2_ref))
    lses = (l0_ref[0], _from_slabs(sl1_ref), _from_slabs(sl2_ref))

    top = jnp.maximum(jnp.maximum(lses[0], lses[1]), lses[2])
    num = 0.0
    den = 0.0
    for o, lse in zip(os_, lses):
        z = jnp.exp(lse - top)
        num = num + z * o
        den = den + z
    o = num / den

    y_att = _dot(o.astype(BF16), wba_ref[...])
    mix = ylg_ref[...] + gatt_ref[...] * y_att
    x1 = x_ref[...] + _dot(mix.astype(BF16), wo_ref[...])
    y_ref[...] = _ffn(x1, _rms_rows(x1, nf_ref[...]).astype(BF16), wfi_ref, wfo_ref)


def _back_call(x, ylg, gatt, os_, lses, lw, layer, tm, qt, kt, vt, caches_t, rel_bias):
    B, S, _ = x.shape
    nt = S // tm
    nb = qt.shape[1]
    dec_parts = (B * nt) // nb
    assert dec_parts * nb == B * nt and HEADS_PER_GROUP % dec_parts == 0
    n_heads = HEADS_PER_GROUP // dec_parts
    tabt, bkts = _decode_attention_operands(rel_bias, dec_parts)

    row_spec = lambda width: pl.BlockSpec((None, tm, width), lambda b, i: (b, i, 0))
    res_specs = [pl.BlockSpec((None, dil, tm // dil, GROUP_WIDTH), lambda b, i: (b, 0, i, 0))
                 for _, dil in DILATED_GROUPS]
    cache_spec = lambda c: pl.BlockSpec(
        (None, None, 2, n_heads) + c.shape[4:],
        lambda b, i: (layer, (b * nt + i) // dec_parts, 0, (b * nt + i) % dec_parts, 0, 0))
    weights = [lw[n] for n in BACK_WEIGHTS]
    in_specs = ([row_spec(D_MODEL)] * 3 + res_specs * 2 + [_layer_spec(w, layer) for w in weights]
                + [_const_spec(qt.shape)] * 3 + [cache_spec(c) for c in caches_t]
                + [_const_spec(tabt.shape)] + [_const_spec(b.shape) for b in bkts])
    return pl.pallas_call(
        functools.partial(_back_kernel, tm=tm, dec_parts=dec_parts),
        grid=(B, nt),
        in_specs=in_specs,
        out_specs=[row_spec(D_MODEL), pl.BlockSpec((GROUP_WIDTH, nb), lambda b, i: (0, 0))],
        out_shape=[jax.ShapeDtypeStruct((B, S, D_MODEL), F32), jax.ShapeDtypeStruct((GROUP_WIDTH, nb), F32)],
        scratch_shapes=[pltpu.VMEM((GROUP_WIDTH // LANES, tm, LANES), F32)] * 4
        + [pltpu.VMEM((dec_parts, SUBLANES, window), F32) for window, _ in DILATED_GROUPS]
        + [pltpu.VMEM((SUBLANES, window), F32) for window, _ in DILATED_GROUPS],
        compiler_params=pltpu.CompilerParams(
            dimension_semantics=("arbitrary", "arbitrary"), vmem_limit_bytes=VMEM_LIMIT_BYTES),
        name="prompt_back",
    )(x, ylg, gatt, *os_, *lses, *weights, qt, kt, vt, *caches_t, tabt, *bkts)


def _dec_front_kernel(x_ref, c0_ref, c1_ref, c2_ref, h0_ref, nm_ref, win_ref, wmg_ref, bmg_ref,
                      wconv_ref, bconv_ref, wrg_ref, brga_ref, brgx_ref, lam_ref, qg_ref, kg_ref,
                      wbl_ref, pn_ref,
                      q_ref, k_ref, v_ref, ylg_ref, gatt_ref, lrux_ref, hs_ref):
    xb = _rms_rows(x_ref[...], nm_ref[...]).astype(BF16)
    lru_x = _dot(xb, win_ref[:, 0:D_RNN])
    lrux_ref[...] = lru_x
    xc = bconv_ref[...] + wconv_ref[0:1, :] * c0_ref[...]
    xc = xc + wconv_ref[1:2, :] * c1_ref[...]
    xc = xc + wconv_ref[2:3, :] * c2_ref[...]
    xc = xc + wconv_ref[3:4, :] * lru_x

    lru_g = _dot(xb, win_ref[:, D_RNN:2 * D_RNN])
    xcb = xc.astype(BF16)
    sp = _softplus(-lam_ref[...])
    ys = []
    for t in range(D_RNN // RG_TILE):
        sl = slice(t * RG_TILE, (t + 1) * RG_TILE)
        g = _dot(xcb[:, sl], wrg_ref[t])
        a, mult, ig = _lru_gates(g, brga_ref[:, sl], brgx_ref[:, sl], sp[:, sl])
        h = a * h0_ref[:, sl] + (mult * ig) * xc[:, sl]
        hs_ref[:, sl] = h
        ys.append((h * _gelu_tanh(lru_g[:, sl])).astype(BF16))
    y_lru = _dot(jnp.concatenate(ys, axis=1), wbl_ref[...])
    g_lru = _sigmoid(_dot(xb, wmg_ref[:, 0:D_MODEL]) + bmg_ref[:, 0:D_MODEL])
    ylg_ref[...] = g_lru * y_lru
    gatt_ref[...] = _sigmoid(_dot(xb, wmg_ref[:, D_MODEL:2 * D_MODEL]) + bmg_ref[:, D_MODEL:2 * D_MODEL])

    pn = pn_ref[...]
    for g in range(N_GROUPS):
        c = 2 * D_RNN + g * GROUP_WIDTH
        sl = slice(g * GROUP_WIDTH, (g + 1) * GROUP_WIDTH)
        q = _dot(xb, win_ref[:, c:c + GROUP_WIDTH])
        k = _dot(xb, win_ref[:, c + ATT_WIDTH:c + ATT_WIDTH + GROUP_WIDTH])
        v = _dot(xb, win_ref[:, c + 2 * ATT_WIDTH:c + 2 * ATT_WIDTH + GROUP_WIDTH])
        q_ref[:, sl] = _head_rms(q, pn, qg_ref[...])
        k_ref[:, sl] = _head_rms(k, pn, kg_ref[...])
        v_ref[:, sl] = v


def _dec_front_call(x, conv_rows, h0, lw, layer):
    nb = x.shape[0]
    weights = [lw[n] for n in FRONT_WEIGHTS]
    acts = (x,) + tuple(conv_rows) + (h0,)
    widths = (ATT_WIDTH, ATT_WIDTH, ATT_WIDTH, D_MODEL, D_MODEL, D_RNN, D_RNN)
    return pl.pallas_call(
        _dec_front_kernel,
        grid=(1,),
        in_specs=([_const_spec(a.shape) for a in acts] + [_layer_spec(w, layer) for w in weights]
                  + [_const_spec(lw["pnorm"].shape)]),
        out_specs=[pl.BlockSpec((nb, w), lambda i: (0, 0)) for w in widths],
        out_shape=[jax.ShapeDtypeStruct((nb, w), F32) for w in widths],
        compiler_params=pltpu.CompilerParams(
            dimension_semantics=("arbitrary",), vmem_limit_bytes=VMEM_LIMIT_BYTES),
        name="decode_front",
    )(*acts, *weights, lw["pnorm"])


def _decode_attention_init(tabt_ref, bkt_refs, bias_refs, s_refs, ot_ref):
    ot_ref[...] = jnp.zeros(ot_ref.shape, F32)
    for g in range(N_GROUPS):
        s_refs[g][...] = jnp.zeros(s_refs[g].shape, F32)
        bkt = bkt_refs[g][...]
        for part in range(bias_refs[g].shape[0]):
            acc = jnp.full(bkt.shape, -jnp.inf, F32)
            for n in range(NUM_BUCKETS):
                acc = jnp.where(bkt == n, tabt_ref[g, part, :, n:n + 1], acc)
            bias_refs[g][part] = acc


def _decode_attention_step(seq, part, n_heads, qt_ref, kt_ref, vt_ref, c_refs, tabt_ref, bias_refs, s_refs,
                           ot_ref):
    mine = lax.broadcasted_iota(jnp.int32, (HEAD_DIM, qt_ref.shape[1]), 1) == seq

    def column(ref, g, h):
        row0 = pl.multiple_of(g * GROUP_WIDTH + (part * n_heads + h) * HEAD_DIM, HEAD_DIM)
        return jnp.sum(jnp.where(mine, ref[pl.ds(row0, HEAD_DIM), :], 0.0), axis=1, keepdims=True)

    def fold_lanes(t, op):
        out = t[:, 0:LANES]
        for c in range(1, t.shape[1] // LANES):
            out = op(out, t[:, c * LANES:(c + 1) * LANES])
        return out

    pad = [jnp.zeros((SUBLANES - n_heads, 1), F32)] if n_heads < SUBLANES else []
    s0, vcols = [], []
    for g in range(N_GROUPS):
        rows = []
        for h in range(n_heads):
            qh = column(qt_ref, g, h)
            s_refs[g][h:h + 1, :] = jnp.sum(c_refs[g][0, h] * qh, axis=0, keepdims=True)
            rows.append(jnp.sum(qh * column(kt_ref, g, h), axis=0, keepdims=True))
            vcols.append(column(vt_ref, g, h))
        s0.append(jnp.concatenate(rows + pad, axis=0) + tabt_ref[g, part, :, 0:1])

    ss = [s_refs[g][...] + bias_refs[g][part] for g in range(N_GROUPS)]
    m_max = jnp.maximum(jnp.maximum(s0[0], s0[1]), s0[2])
    for s in ss:
        m_max = jnp.maximum(m_max, jnp.max(fold_lanes(s, jnp.maximum), axis=1, keepdims=True))
    p0 = [jnp.exp(s - m_max) for s in s0]
    den = p0[0] + p0[1] + p0[2]
    for g, s in enumerate(ss):
        p = jnp.exp(s - m_max)
        s_refs[g][...] = p
        den = den + jnp.sum(fold_lanes(p, jnp.add), axis=1, keepdims=True)
    inv_den = 1.0 / den

    for h in range(n_heads):
        acc = None
        new = 0.0
        for g in range(N_GROUPS):
            f = fold_lanes(c_refs[g][1, h] * s_refs[g][h:h + 1, :], jnp.add)
            acc = f if acc is None else acc + f
            new = new + p0[g][h:h + 1, :] * vcols[g * n_heads + h]
        num = jnp.sum(acc, axis=1, keepdims=True) + new
        row0 = pl.multiple_of((part * n_heads + h) * HEAD_DIM, HEAD_DIM)
        ot_ref[pl.ds(row0, HEAD_DIM), :] = jnp.where(mine, num * inv_den[h:h + 1, :],
                                                      ot_ref[pl.ds(row0, HEAD_DIM), :])


def _decode_attention_operands(rel_bias, parts):
    n_heads = HEADS_PER_GROUP // parts
    tab = rel_bias.reshape(NUM_BUCKETS, N_GROUPS, parts, n_heads).transpose(1, 2, 3, 0)
    tab = jnp.pad(tab, ((0, 0), (0, 0), (0, SUBLANES - n_heads), (0, 0)))
    bkts = []
    for window, dil in DILATED_GROUPS:
        w = np.arange(window)
        bkt = np.where(w % dil == 0, _t5_bucket(window - w), -1).astype(np.int32)
        bkts.append(jnp.asarray(np.broadcast_to(bkt[None, :], (SUBLANES, window))))
    return tab, bkts


def _dec_back_kernel(x_ref, ylg_ref, gatt_ref, o_ref, wba_ref, wo_ref, nf_ref, wfi_ref, wfo_ref, y_ref):
    y_att = _dot(o_ref[...].astype(BF16), wba_ref[...])
    mix = ylg_ref[...] + gatt_ref[...] * y_att
    x1 = x_ref[...] + _dot(mix.astype(BF16), wo_ref[...])
    y_ref[...] = _ffn(x1, _rms_rows(x1, nf_ref[...]).astype(BF16), wfi_ref, wfo_ref)


def _dec_back_call(x, ylg, gatt, o, lw, layer):
    weights = [lw[n] for n in BACK_WEIGHTS]
    acts = (x, ylg, gatt, o)
    return pl.pallas_call(
        _dec_back_kernel,
        grid=(1,),
        in_specs=[_const_spec(a.shape) for a in acts] + [_layer_spec(w, layer) for w in weights],
        out_specs=pl.BlockSpec(x.shape, lambda i: (0, 0)),
        out_shape=jax.ShapeDtypeStruct(x.shape, F32),
        compiler_params=pltpu.CompilerParams(
            dimension_semantics=("arbitrary",), vmem_limit_bytes=VMEM_LIMIT_BYTES),
        name="decode_back",
    )(*acts, *weights)


def _prepare_weights(norm_mix, w_in, w_conv, b_conv, w_rg_a, b_rg_a, w_rg_x, b_rg_x, lru_lambda,
                     q_gain, k_gain, w_merge, b_merge, w_branch_lru, w_branch_att, w_o, norm_ffn,
                     w_ffn_in, w_ffn_out):
    depth = norm_mix.shape[0]
    row = lambda t: t.reshape(depth, 1, -1).astype(F32)
    per_tile = RG_TILE // LRU_BLOCK
    tiles = D_RNN // RG_TILE

    def block_diag(w):
        w = w.reshape(depth, tiles, per_tile, LRU_BLOCK, LRU_BLOCK)
        eye = jnp.eye(per_tile, dtype=w.dtype)
        return jnp.einsum("ltnij,nm->ltnimj", w, eye).reshape(depth, tiles, RG_TILE, RG_TILE)

    head = jnp.arange(MXU_DIM) // HEAD_DIM
    tile_gain = lambda t: jnp.tile(t.reshape(depth, 1, HEAD_DIM), (1, 1, HEADS_PER_GROUP)).astype(F32)
    return dict(
        norm_mix=row(norm_mix), w_in=w_in.astype(BF16), w_merge=w_merge.astype(BF16),
        b_merge=row(b_merge), w_conv=w_conv.astype(F32), b_conv=row(b_conv),
        w_rg=jnp.concatenate([block_diag(w_rg_a), block_diag(w_rg_x)], axis=3).astype(BF16),
        b_rg_a=row(b_rg_a), b_rg_x=row(b_rg_x), lam=row(lru_lambda),
        q_gain=tile_gain(q_gain) * (HEAD_DIM ** -0.5),
        k_gain=tile_gain(k_gain),
        w_branch_lru=w_branch_lru.astype(BF16),
        pnorm=((head[:, None] == head[None, :]).astype(F32) / HEAD_DIM).astype(BF16),
        w_branch_att=w_branch_att.astype(BF16), w_o=w_o.astype(BF16), norm_ffn=row(norm_ffn),
        w_ffn_in=w_ffn_in.astype(BF16), w_ffn_out=w_ffn_out.astype(BF16))


def kernel(x_prompt, x_sample, cache_kv_g0, cache_kv_g1, cache_kv_g2, state_conv, state_h, rel_bias,
           norm_mix, w_in, w_conv, b_conv, w_rg_a, b_rg_a, w_rg_x, b_rg_x, lru_lambda, q_gain, k_gain,
           w_merge, b_merge, w_branch_lru, w_branch_att, w_o, norm_ffn, w_ffn_in, w_ffn_out):
    B, S, _ = x_prompt.shape
    nb = x_sample.shape[0]
    depth = norm_mix.shape[0]
    lw = _prepare_weights(norm_mix, w_in, w_conv, b_conv, w_rg_a, b_rg_a, w_rg_x, b_rg_x, lru_lambda,
                          q_gain, k_gain, w_merge, b_merge, w_branch_lru, w_branch_att, w_o, norm_ffn,
                          w_ffn_in, w_ffn_out)
    rel_bias = rel_bias.astype(F32)
    caches_t = [jnp.transpose(c.astype(F32), (0, 1, 3, 4, 5, 2))
                for c in (cache_kv_g0, cache_kv_g1, cache_kv_g2)]

    yp = x_prompt
    ys = x_sample.reshape(nb, D_MODEL)
    kvp = [[] for _ in range(N_GROUPS)]
    kvs = [[] for _ in range(N_GROUPS)]
    conv_p, h_p, conv_s, h_s = [], [], [], []
    for l in range(depth):
        sc = state_conv[l].astype(F32)
        conv_rows = [sc[:, r, :] for r in range(CONV_WIDTH - 1)]
        qs, ks, vs, ylg_s, gatt_s, lrux_s, hs_s = _dec_front_call(ys, conv_rows, state_h[l].astype(F32), lw, l)

        (q0, q1, q2, k0, k1, k2, v0, v1, v2, ylg, gatt, kv0, kv1, kv2, ctail, hfin) = _front_call(
            yp, lw, l, TM_PROMPT)
        os_, lses = [], []
        for g, (qg, kg, vg) in enumerate(((q0, k0, v0), (q1, k1, v1), (q2, k2, v2))):
            tab = rel_bias[:, g * HEADS_PER_GROUP:(g + 1) * HEADS_PER_GROUP]
            o, lse = _attn_call(qg, kg, vg, tab, DILATED_GROUPS[g][1])
            os_.append(o)
            lses.append(lse)
        yp, o_t = _back_call(yp, ylg, gatt, os_, lses, lw, l, TM_PROMPT, qs.T, ks.T, vs.T, caches_t, rel_bias)
        for g, kv in enumerate((kv0, kv1, kv2)):
            kvp[g].append(kv.reshape(B, kv.shape[1], 2, HEADS_PER_GROUP, HEAD_DIM))
        conv_p.append(ctail[:, SUBLANES - (CONV_WIDTH - 1):, :])
        h_p.append(hfin[:, 0, :])

        ys = _dec_back_call(ys, ylg_s, gatt_s, o_t.T, lw, l)
        for g in range(N_GROUPS):
            sl = slice(g * GROUP_WIDTH, (g + 1) * GROUP_WIDTH)
            kvs[g].append(jnp.stack([ks[:, sl], vs[:, sl]], axis=1).reshape(
                nb, 1, 2, HEADS_PER_GROUP, HEAD_DIM))
        conv_s.append(jnp.stack(conv_rows[1:] + [lrux_s], axis=1))
        h_s.append(hs_s)

    stack = jnp.stack
    return (yp, ys.reshape(nb, 1, D_MODEL),
            stack(kvp[0]), stack(kvp[1]), stack(kvp[2]), stack(conv_p), stack(h_p),
            stack(kvs[0]), stack(kvs[1]), stack(kvs[2]), stack(conv_s), stack(h_s))
```

```python
import functools
import math

import numpy as np
import jax
import jax.numpy as jnp
from jax import lax
from jax.experimental import pallas as pl
from jax.experimental.pallas import tpu as pltpu

D_MODEL = 1024
D_RNN = D_MODEL
N_LRU_BLOCKS = 16
LRU_BLOCK = D_RNN // N_LRU_BLOCKS
CONV_WIDTH = 4
LRU_C = 8.0
HEAD_DIM = 64
HEADS_PER_GROUP = 8
DILATED_GROUPS = ((128, 1), (512, 4), (2048, 16))
N_GROUPS = len(DILATED_GROUPS)
N_ATT_HEADS = N_GROUPS * HEADS_PER_GROUP
ATT_WIDTH = N_ATT_HEADS * HEAD_DIM
GROUP_WIDTH = HEADS_PER_GROUP * HEAD_DIM
BAND_BLOCK = 128
NUM_BUCKETS = 32
MAX_DISTANCE = 2048
D_FF = 2816
EPS = 1e-6

F32 = jnp.float32
BF16 = jnp.bfloat16

MXU_DIM = 256
SUBLANES = 8
LANES = 128
VMEM_LIMIT_BYTES = 56 * 1024 * 1024

TM_PROMPT = 256
ATTN_QBLOCKS = 16
RG_TILE = MXU_DIM
FFN_CHUNKS = ((0, 768), (768, 768), (1536, 768), (2304, 512))


def _dot(a, b):
    return jnp.dot(a, b, preferred_element_type=F32)


def _dot_nt(a, b):
    return lax.dot_general(a, b, (((1,), (1,)), ((), ())), preferred_element_type=F32)


def _sigmoid(x):
    return 0.5 * (jnp.tanh(0.5 * x) + 1.0)


def _gelu_tanh(x):
    c = math.sqrt(2.0 / math.pi)
    return 0.5 * x * (1.0 + jnp.tanh(c * (x + 0.044715 * (x * x * x))))


def _softplus(z):
    return jnp.maximum(z, 0.0) + jnp.log1p(jnp.exp(-jnp.abs(z)))


def _rms_rows(x, g):
    y = x * lax.rsqrt(jnp.mean(x * x, axis=-1, keepdims=True) + EPS)
    return y * g


def _head_rms(t, pn, gain):
    t2 = (t * t).astype(BF16)
    tiles = [_dot(t2[:, c * MXU_DIM:(c + 1) * MXU_DIM], pn) for c in range(t.shape[1] // MXU_DIM)]
    ms = tiles[0] if len(tiles) == 1 else jnp.concatenate(tiles, axis=1)
    return (t * lax.rsqrt(ms + EPS)) * gain


def _lru_gates(g, b_a, b_x, sp):
    w = g.shape[1] // 2
    r = _sigmoid(g[:, :w] + b_a)
    ig = _sigmoid(g[:, w:] + b_x)
    log_a = (-LRU_C * r) * sp
    a = jnp.exp(log_a)
    mult = jnp.sqrt(-jnp.tanh(log_a) * (a * a + 1.0))
    return a, mult, ig


def _scan_rows(a, u, h_in):
    rows, c = a.shape
    groups = rows // SUBLANES
    a3 = a.reshape(groups, SUBLANES, c)
    u3 = u.reshape(groups, SUBLANES, c)
    row = lax.broadcasted_iota(jnp.int32, (groups, SUBLANES, c), 1)
    shift = 1
    while shift < SUBLANES:
        ok = row >= shift
        a_sh = jnp.where(ok, pltpu.roll(a3, shift, axis=1), 1.0)
        u_sh = jnp.where(ok, pltpu.roll(u3, shift, axis=1), 0.0)
        u3 = u3 + a3 * u_sh
        a3 = a3 * a_sh
        shift *= 2
    out = []
    h = h_in
    for g in range(groups):
        hg = a3[g] * h + u3[g]
        out.append(hg)
        h = hg[SUBLANES - 1:SUBLANES, :]
    return jnp.concatenate(out, axis=0), h


def _to_slabs(slab_ref, t):
    for c in range(slab_ref.shape[0]):
        slab_ref[c] = t[:, c * LANES:(c + 1) * LANES]


def _from_slabs(slab_ref):
    return jnp.concatenate([slab_ref[c] for c in range(slab_ref.shape[0])], axis=1)


def _gather_residues(slab_ref, dil):
    per = slab_ref.shape[1] // dil
    cols = []
    for c in range(slab_ref.shape[0]):
        cols.append(jnp.concatenate(
            [slab_ref[c, pl.ds(r, per, stride=dil), :] for r in range(dil)], axis=0))
    return jnp.concatenate(cols, axis=1)


def _scatter_residues(slab_ref, t, dil, first_slab=0):
    per = slab_ref.shape[1] // dil
    for c in range(t.shape[1] // LANES):
        for r in range(dil):
            slab_ref[first_slab + c, pl.ds(r, per, stride=dil), :] = (
                t[r * per:(r + 1) * per, c * LANES:(c + 1) * LANES])


def _ffn(x1, xb, wfi_ref, wfo_ref):
    acc = x1
    for start, width in FFN_CHUNKS:
        gate = _dot(xb, wfi_ref[:, start:start + width])
        up = _dot(xb, wfi_ref[:, D_FF + start:D_FF + start + width])
        hid = (gate * _sigmoid(gate)) * up
        acc = acc + _dot(hid.astype(BF16), wfo_ref[start:start + width, :])
    return acc


def _layer_spec(w, layer):
    tail = (0,) * (w.ndim - 1)
    return pl.BlockSpec((None,) + w.shape[1:], lambda *_: (layer,) + tail, pipeline_mode=pl.Buffered(1))


def _const_spec(shape):
    zeros = (0,) * len(shape)
    return pl.BlockSpec(shape, lambda *_: zeros, pipeline_mode=pl.Buffered(1))


FRONT_WEIGHTS = ("norm_mix", "w_in", "w_merge", "b_merge", "w_conv", "b_conv", "w_rg", "b_rg_a", "b_rg_x",
                 "lam", "q_gain", "k_gain", "w_branch_lru")
BACK_WEIGHTS = ("w_branch_att", "w_o", "norm_ffn", "w_ffn_in", "w_ffn_out")


def _front_kernel(x_ref, nm_ref, win_ref, wmg_ref, bmg_ref, wconv_ref, bconv_ref, wrg_ref,
                  brga_ref, brgx_ref, lam_ref, qg_ref, kg_ref, wbl_ref, pn_ref,
                  q0_ref, q1_ref, q2_ref, k0_ref, k1_ref, k2_ref, v0_ref, v1_ref, v2_ref,
                  ylg_ref, gatt_ref, kvp0_ref, kvp1_ref, kvp2_ref, convp_ref, hp_ref,
                  xn_ref, kv_ref, ext_ref, h_ref, xb_ref, xg_ref, xc_ref, xcb_ref, gl_ref, y_ref, *, tm):
    i = pl.program_id(1)
    n_tiles = D_RNN // RG_TILE

    @pl.when(i == 0)
    def _():
        ext_ref[0:SUBLANES, :] = jnp.zeros((SUBLANES, D_RNN), F32)
        h_ref[...] = jnp.zeros((SUBLANES, D_RNN), F32)

    xn = _rms_rows(x_ref[...], nm_ref[...])
    _to_slabs(xn_ref, xn)
    xb_ref[...] = xn.astype(BF16)

    for t in range(n_tiles):
        sl = slice(t * RG_TILE, (t + 1) * RG_TILE)
        lru_x = _dot(xb_ref[...], win_ref[:, sl])
        ext_ref[SUBLANES:SUBLANES + tm, sl] = lru_x
        xc = bconv_ref[:, sl] + wconv_ref[0:1, sl] * ext_ref[SUBLANES - 3:SUBLANES - 3 + tm, sl]
        xc = xc + wconv_ref[1:2, sl] * ext_ref[SUBLANES - 2:SUBLANES - 2 + tm, sl]
        xc = xc + wconv_ref[2:3, sl] * ext_ref[SUBLANES - 1:SUBLANES - 1 + tm, sl]
        xc = xc + wconv_ref[3:4, sl] * lru_x
        xc_ref[:, sl] = xc
        xcb_ref[:, sl] = xc.astype(BF16)
        gl_ref[:, sl] = _gelu_tanh(_dot(xb_ref[...], win_ref[:, D_RNN + t * RG_TILE:D_RNN + (t + 1) * RG_TILE]))
    tail = ext_ref[tm:tm + SUBLANES, :]
    ext_ref[0:SUBLANES, :] = tail
    convp_ref[...] = tail

    sp = _softplus(-lam_ref[...])
    first_row = (lax.broadcasted_iota(jnp.int32, (tm, RG_TILE), 0) == 0) & (i == 0)
    gates = {}

    def lru_gates(t):
        sl = slice(t * RG_TILE, (t + 1) * RG_TILE)
        g = _dot(xcb_ref[:, sl], wrg_ref[t])
        a, mult, ig = _lru_gates(g, brga_ref[:, sl], brgx_ref[:, sl], sp[:, sl])
        mult = jnp.where(first_row, 1.0, mult)
        gates[t] = (a, (mult * ig) * xc_ref[:, sl])

    def lru_scan(t):
        sl = slice(t * RG_TILE, (t + 1) * RG_TILE)
        a, u = gates.pop(t)
        hs, h_last = _scan_rows(a, u, h_ref[0:1, sl])
        h_ref[:, sl] = jnp.broadcast_to(h_last, (SUBLANES, RG_TILE))
        y_ref[:, sl] = (hs * gl_ref[:, sl]).astype(BF16)

    pn = pn_ref[...]
    q_refs = (q0_ref, q1_ref, q2_ref)
    k_refs = (k0_ref, k1_ref, k2_ref)
    v_refs = (v0_ref, v1_ref, v2_ref)
    kvp_refs = (kvp0_ref, kvp1_ref, kvp2_ref)

    def attn_operands(g, half):
        window, dil = DILATED_GROUPS[g]
        per = tm // dil
        keep = min(window, tm)
        if dil == 1:
            src = xb_ref
        else:
            if half == 0:
                xg_ref[...] = _gather_residues(xn_ref, dil).astype(BF16)
            src = xg_ref
        c = 2 * D_RNN + g * GROUP_WIDTH + half * MXU_DIM
        hl = slice(half * MXU_DIM, (half + 1) * MXU_DIM)
        vl = slice(GROUP_WIDTH + half * MXU_DIM, GROUP_WIDTH + (half + 1) * MXU_DIM)
        q = _dot(src[...], win_ref[:, c:c + MXU_DIM])
        k = _dot(src[...], win_ref[:, c + ATT_WIDTH:c + ATT_WIDTH + MXU_DIM])
        v = _dot(src[...], win_ref[:, c + 2 * ATT_WIDTH:c + 2 * ATT_WIDTH + MXU_DIM])
        qn = _head_rms(q, pn, qg_ref[:, hl]).astype(BF16)
        kn = _head_rms(k, pn, kg_ref[:, hl])
        knb = kn.astype(BF16)
        vb = v.astype(BF16)
        for r in range(dil):
            rows = slice(r * per, (r + 1) * per)
            q_refs[g][r, :, hl] = qn[rows]
            k_refs[g][r, :, hl] = knb[rows]
            v_refs[g][r, :, hl] = vb[rows]
        if dil == 1:
            kvp_refs[g][:, hl] = kn[tm - keep:, :]
            kvp_refs[g][:, vl] = v[tm - keep:, :]
        else:
            _scatter_residues(kv_ref, kn, dil, first_slab=hl.start // LANES)
            _scatter_residues(kv_ref, v, dil, first_slab=vl.start // LANES)
            if half == GROUP_WIDTH // MXU_DIM - 1:
                kvp_refs[g][...] = _from_slabs(kv_ref)

    def att_gate(t):
        sl = slice(t * RG_TILE, (t + 1) * RG_TILE)
        al = slice(D_MODEL + t * RG_TILE, D_MODEL + (t + 1) * RG_TILE)
        gatt_ref[:, sl] = _sigmoid(_dot(xb_ref[...], wmg_ref[:, al]) + bmg_ref[:, al]).astype(gatt_ref.dtype)

    for t in range(n_tiles):
        lru_gates(t)
        if t < N_GROUPS:
            attn_operands(t, 0)
        else:
            att_gate(0)
            att_gate(1)
        lru_scan(t)
        if t < N_GROUPS:
            attn_operands(t, 1)
        else:
            att_gate(2)
            att_gate(3)
    hp_ref[...] = h_ref[...]
    for t in range(n_tiles):
        sl = slice(t * RG_TILE, (t + 1) * RG_TILE)
        g_lru = _sigmoid(_dot(xb_ref[...], wmg_ref[:, sl]) + bmg_ref[:, sl])
        ylg_ref[:, sl] = (g_lru * _dot(y_ref[...], wbl_ref[:, sl])).astype(ylg_ref.dtype)


def _front_call(x, lw, layer, tm):
    B, S, _ = x.shape
    nt = S // tm
    row_spec = lambda width: pl.BlockSpec((None, tm, width), lambda b, i: (b, i, 0))
    weights = [lw[n] for n in FRONT_WEIGHTS]
    in_specs = ([row_spec(D_MODEL)] + [_layer_spec(w, layer) for w in weights]
                + [_const_spec(lw["pnorm"].shape)])

    out_shape, out_specs = [], []
    for _ in range(3):
        for _, dil in DILATED_GROUPS:
            out_shape.append(jax.ShapeDtypeStruct((B, dil, S // dil, GROUP_WIDTH), BF16))
            out_specs.append(pl.BlockSpec((None, dil, tm // dil, GROUP_WIDTH), lambda b, i: (b, 0, i, 0)))
    out_shape.append(jax.ShapeDtypeStruct((B, S, D_MODEL), BF16))
    out_specs.append(row_spec(D_MODEL))
    out_shape.append(jax.ShapeDtypeStruct((B, S, D_MODEL), BF16))
    out_specs.append(row_spec(D_MODEL))
    for window, _ in DILATED_GROUPS:
        keep = min(window, tm)
        first = nt - window // keep
        out_shape.append(jax.ShapeDtypeStruct((B, window, 2 * GROUP_WIDTH), F32))
        out_specs.append(pl.BlockSpec(
            (None, keep, 2 * GROUP_WIDTH),
            functools.partial(lambda b, i, first: (b, jnp.maximum(i - first, 0), 0), first=first)))
    for _ in range(2):
        out_shape.append(jax.ShapeDtypeStruct((B, SUBLANES, D_RNN), F32))
        out_specs.append(pl.BlockSpec((None, SUBLANES, D_RNN), lambda b, i: (b, 0, 0)))

    return pl.pallas_call(
        functools.partial(_front_kernel, tm=tm),
        grid=(B, nt),
        in_specs=in_specs,
        out_specs=out_specs,
        out_shape=out_shape,
        scratch_shapes=[pltpu.VMEM((D_MODEL // LANES, tm, LANES), F32),
                        pltpu.VMEM((2 * GROUP_WIDTH // LANES, tm, LANES), F32),
                        pltpu.VMEM((tm + SUBLANES, D_RNN), F32), pltpu.VMEM((SUBLANES, D_RNN), F32),
                        pltpu.VMEM((tm, D_MODEL), BF16), pltpu.VMEM((tm, D_MODEL), BF16),
                        pltpu.VMEM((tm, D_RNN), F32), pltpu.VMEM((tm, D_RNN), BF16),
                        pltpu.VMEM((tm, D_RNN), F32), pltpu.VMEM((tm, D_RNN), BF16)],
        compiler_params=pltpu.CompilerParams(
            dimension_semantics=("arbitrary", "arbitrary"), vmem_limit_bytes=VMEM_LIMIT_BYTES),
        name="prompt_front",
    )(x, *weights, lw["pnorm"])


def _attn_kernel(tab_ref, bkt_ref, q_ref, kp_ref, kc_ref, vp_ref, vc_ref, o_ref, lse_ref,
                 bias_ref, kbuf_ref, vbuf_ref, *, qblocks):
    b, r, j = pl.program_id(0), pl.program_id(1), pl.program_id(2)
    BB = BAND_BLOCK
    PAIR = 2 * HEAD_DIM

    @pl.when((b == 0) & (r == 0) & (j == 0))
    def _():
        bkt = bkt_ref[...]
        for h in range(HEADS_PER_GROUP):
            acc = jnp.full(bkt.shape, -jnp.inf, F32)
            for n in range(NUM_BUCKETS):
                acc = jnp.where(bkt == n, tab_ref[n, h], acc)
            bias_ref[h // 2, (h % 2) * BB:(h % 2 + 1) * BB, :] = acc

    first_col = lax.broadcasted_iota(jnp.int32, (1, 2 * BB), 1) < BB
    pen = jnp.where(first_col & (j == 0), -jnp.inf, 0.0).astype(F32)
    low_half = lax.broadcasted_iota(jnp.int32, (BB, PAIR), 1) < HEAD_DIM
    ones = jnp.ones((2 * BB, PAIR), BF16)
    for rr in range(q_ref.shape[0]):
        kbuf_ref[0:BB, :] = kp_ref[rr]
        kbuf_ref[BB:, :] = kc_ref[rr]
        vbuf_ref[0:BB, :] = vp_ref[rr]
        vbuf_ref[BB:, :] = vc_ref[rr]
        for t in range(qblocks):
            rows = slice(t * BB, (t + 1) * BB)
            keys = slice(t * BB, (t + 2) * BB)
            for hp in range(HEADS_PER_GROUP // 2):
                sl = slice(hp * PAIR, (hp + 1) * PAIR)
                qp = q_ref[rr, rows, sl]
                zero = jnp.zeros_like(qp)
                lhs = jnp.concatenate([jnp.where(low_half, qp, zero), jnp.where(low_half, zero, qp)], axis=0)
                s = _dot_nt(lhs, kbuf_ref[keys, sl]) + bias_ref[hp]
                if t == 0:
                    s = s + pen
                m = jnp.max(s, axis=-1, keepdims=True)
                p = jnp.exp((s - m).astype(BF16))
                oe = _dot(p, jnp.concatenate([vbuf_ref[keys, sl], ones], axis=1))
                l = oe[:, PAIR:]
                o = oe[:, :PAIR] / l
                lse = m + jnp.log(l)
                for e in range(2):
                    hl = slice(hp * PAIR + e * HEAD_DIM, hp * PAIR + (e + 1) * HEAD_DIM)
                    o_ref[rr, rows, hl] = o[e * BB:(e + 1) * BB, e * HEAD_DIM:(e + 1) * HEAD_DIM]
                    lse_ref[rr, rows, hl] = lse[e * BB:(e + 1) * BB, e * HEAD_DIM:(e + 1) * HEAD_DIM]


def _t5_bucket(dist):
    max_exact = NUM_BUCKETS // 2
    d_f = np.maximum(dist, 1).astype(np.float32)
    large = max_exact + (np.log(d_f / np.float32(max_exact)) / np.float32(math.log(MAX_DISTANCE / max_exact))
                         * np.float32(NUM_BUCKETS - max_exact)).astype(np.int32)
    large = np.minimum(large, NUM_BUCKETS - 1)
    return np.where(dist < max_exact, dist, large).astype(np.int32)


def _band_buckets(dil):
    BB = BAND_BLOCK
    qi = np.arange(BB)[:, None]
    ki = np.arange(2 * BB)[None, :]
    sub = qi + BB - ki
    in_band = (sub >= 0) & (sub <= BB)
    return jnp.asarray(np.where(in_band, _t5_bucket(np.clip(sub, 0, BB) * dil), -1).astype(np.int32))


def _attn_call(q, k, v, tab, dil):
    B, _, n, C = q.shape
    BB = BAND_BLOCK
    qb = min(ATTN_QBLOCKS, n // BB)
    res = ATTN_QBLOCKS // qb
    cur = pl.BlockSpec((None, res, qb * BB, C), lambda b, r, j: (b, r, j, 0))
    prev = pl.BlockSpec((None, res, BB, C), lambda b, r, j: (b, r, jnp.maximum(j * qb - 1, 0), 0))
    return pl.pallas_call(
        functools.partial(_attn_kernel, qblocks=qb),
        grid=(B, dil // res, n // (qb * BB)),
        in_specs=[pl.BlockSpec(memory_space=pltpu.SMEM),
                  pl.BlockSpec((BB, 2 * BB), lambda b, r, j: (0, 0)),
                  cur, prev, cur, prev, cur],
        out_specs=[cur, cur],
        out_shape=[jax.ShapeDtypeStruct(q.shape, F32)] * 2,
        scratch_shapes=[pltpu.VMEM((HEADS_PER_GROUP // 2, 2 * BB, 2 * BB), F32),
                        pltpu.VMEM(((qb + 1) * BB, C), BF16), pltpu.VMEM(((qb + 1) * BB, C), BF16)],
        compiler_params=pltpu.CompilerParams(
            dimension_semantics=("arbitrary", "arbitrary", "arbitrary"), vmem_limit_bytes=VMEM_LIMIT_BYTES),
        name="prompt_attn_d%d" % dil,
    )(tab, _band_buckets(dil), q, k, k, v, v)


def _back_kernel(x_ref, ylg_ref, gatt_ref, o0_ref, o1_ref, o2_ref, l0_ref, l1_ref, l2_ref,
                 wba_ref, wo_ref, nf_ref, wfi_ref, wfo_ref,
                 qt_ref, kt_ref, vt_ref, c0_ref, c1_ref, c2_ref, tabt_ref, bkt0_ref, bkt1_ref, bkt2_ref,
                 y_ref, ot_ref,
                 so1_ref, so2_ref, sl1_ref, sl2_ref, bias0_ref, bias1_ref, bias2_ref, s0_ref, s1_ref, s2_ref,
                 *, tm, dec_parts):
    step = pl.program_id(0) * pl.num_programs(1) + pl.program_id(1)
    bias_refs = (bias0_ref, bias1_ref, bias2_ref)
    s_refs = (s0_ref, s1_ref, s2_ref)

    @pl.when(step == 0)
    def _():
        _decode_attention_init(tabt_ref, (bkt0_ref, bkt1_ref, bkt2_ref), bias_refs, s_refs, ot_ref)

    _decode_attention_step(step // dec_parts, step % dec_parts, HEADS_PER_GROUP // dec_parts,
                           qt_ref, kt_ref, vt_ref, (c0_ref, c1_ref, c2_ref), tabt_ref, bias_refs, s_refs, ot_ref)

    for (_, dil), src, dst in ((DILATED_GROUPS[1], o1_ref, so1_ref), (DILATED_GROUPS[2], o2_ref, so2_ref),
                               (DILATED_GROUPS[1], l1_ref, sl1_ref), (DILATED_GROUPS[2], l2_ref, sl2_ref)):
        _scatter_residues(dst, src[...].reshape(tm, GROUP_WIDTH), dil)
    os_ = (o0_ref[0], _from_slabs(so1_ref), _from_slabs(so2_ref))
    lses = (l0_ref[0], _from_slabs(sl1_ref), _from_slabs(sl2_ref))

    top = jnp.maximum(jnp.maximum(lses[0], lses[1]), lses[2])
    num = 0.0
    den = 0.0
    for o, lse in zip(os_, lses):
        z = jnp.exp(lse - top)
        num = num + z * o
        den = den + z
    o = num / den

    y_att = _dot(o.astype(BF16), wba_ref[...])
    mix = ylg_ref[...] + gatt_ref[...] * y_att
    x1 = x_ref[...] + _dot(mix.astype(BF16), wo_ref[...])
    y_ref[...] = _ffn(x1, _rms_rows(x1, nf_ref[...]).astype(BF16), wfi_ref, wfo_ref)


def _back_call(x, ylg, gatt, os_, lses, lw, layer, tm, qt, kt, vt, caches_t, rel_bias):
    B, S, _ = x.shape
    nt = S // tm
    nb = qt.shape[1]
    dec_parts = (B * nt) // nb
    assert dec_parts * nb == B * nt and HEADS_PER_GROUP % dec_parts == 0
    n_heads = HEADS_PER_GROUP // dec_parts
    tabt, bkts = _decode_attention_operands(rel_bias, dec_parts)

    row_spec = lambda width: pl.BlockSpec((None, tm, width), lambda b, i: (b, i, 0))
    res_specs = [pl.BlockSpec((None, dil, tm // dil, GROUP_WIDTH), lambda b, i: (b, 0, i, 0))
                 for _, dil in DILATED_GROUPS]
    cache_spec = lambda c: pl.BlockSpec(
        (None, None, 2, n_heads) + c.shape[4:],
        lambda b, i: (layer, (b * nt + i) // dec_parts, 0, (b * nt + i) % dec_parts, 0, 0))
    weights = [lw[n] for n in BACK_WEIGHTS]
    in_specs = ([row_spec(D_MODEL)] * 3 + res_specs * 2 + [_layer_spec(w, layer) for w in weights]
                + [_const_spec(qt.shape)] * 3 + [cache_spec(c) for c in caches_t]
                + [_const_spec(tabt.shape)] + [_const_spec(b.shape) for b in bkts])
    return pl.pallas_call(
        functools.partial(_back_kernel, tm=tm, dec_parts=dec_parts),
        grid=(B, nt),
        in_specs=in_specs,
        out_specs=[row_spec(D_MODEL), pl.BlockSpec((GROUP_WIDTH, nb), lambda b, i: (0, 0))],
        out_shape=[jax.ShapeDtypeStruct((B, S, D_MODEL), F32), jax.ShapeDtypeStruct((GROUP_WIDTH, nb), F32)],
        scratch_shapes=[pltpu.VMEM((GROUP_WIDTH // LANES, tm, LANES), F32)] * 4
        + [pltpu.VMEM((dec_parts, SUBLANES, window), F32) for window, _ in DILATED_GROUPS]
        + [pltpu.VMEM((SUBLANES, window), F32) for window, _ in DILATED_GROUPS],
        compiler_params=pltpu.CompilerParams(
            dimension_semantics=("arbitrary", "arbitrary"), vmem_limit_bytes=VMEM_LIMIT_BYTES),
        name="prompt_back",
    )(x, ylg, gatt, *os_, *lses, *weights, qt, kt, vt, *caches_t, tabt, *bkts)


def _dec_front_kernel(x_ref, c0_ref, c1_ref, c2_ref, h0_ref, nm_ref, win_ref, wmg_ref, bmg_ref,
                      wconv_ref, bconv_ref, wrg_ref, brga_ref, brgx_ref, lam_ref, qg_ref, kg_ref,
                      wbl_ref, pn_ref,
                      qt_ref, kt_ref, vt_ref, k_ref, v_ref, ylg_ref, gatt_ref, lrux_ref, hs_ref):
    xb = _rms_rows(x_ref[...], nm_ref[...]).astype(BF16)
    lru_x = _dot(xb, win_ref[:, 0:D_RNN])
    lrux_ref[...] = lru_x
    xc = bconv_ref[...] + wconv_ref[0:1, :] * c0_ref[...]
    xc = xc + wconv_ref[1:2, :] * c1_ref[...]
    xc = xc + wconv_ref[2:3, :] * c2_ref[...]
    xc = xc + wconv_ref[3:4, :] * lru_x

    lru_g = _dot(xb, win_ref[:, D_RNN:2 * D_RNN])
    xcb = xc.astype(BF16)
    sp = _softplus(-lam_ref[...])
    ys = []
    for t in range(D_RNN // RG_TILE):
        sl = slice(t * RG_TILE, (t + 1) * RG_TILE)
        g = _dot(xcb[:, sl], wrg_ref[t])
        a, mult, ig = _lru_gates(g, brga_ref[:, sl], brgx_ref[:, sl], sp[:, sl])
        h = a * h0_ref[:, sl] + (mult * ig) * xc[:, sl]
        hs_ref[:, sl] = h
        ys.append((h * _gelu_tanh(lru_g[:, sl])).astype(BF16))
    y_lru = _dot(jnp.concatenate(ys, axis=1), wbl_ref[...])
    g_lru = _sigmoid(_dot(xb, wmg_ref[:, 0:D_MODEL]) + bmg_ref[:, 0:D_MODEL])
    ylg_ref[...] = g_lru * y_lru
    gatt_ref[...] = _sigmoid(_dot(xb, wmg_ref[:, D_MODEL:2 * D_MODEL]) + bmg_ref[:, D_MODEL:2 * D_MODEL])

    pn = pn_ref[...]
    for g in range(N_GROUPS):
        c = 2 * D_RNN + g * GROUP_WIDTH
        sl = slice(g * GROUP_WIDTH, (g + 1) * GROUP_WIDTH)
        q = _dot(xb, win_ref[:, c:c + GROUP_WIDTH])
        k = _dot(xb, win_ref[:, c + ATT_WIDTH:c + ATT_WIDTH + GROUP_WIDTH])
        v = _dot(xb, win_ref[:, c + 2 * ATT_WIDTH:c + 2 * ATT_WIDTH + GROUP_WIDTH])
        qn = _head_rms(q, pn, qg_ref[...])
        kn = _head_rms(k, pn, kg_ref[...])
        k_ref[:, sl] = kn
        v_ref[:, sl] = v
        qt_ref[sl, :] = qn.T
        kt_ref[sl, :] = kn.T
        vt_ref[sl, :] = v.T


def _dec_front_call(x, conv_rows, h0, lw, layer):
    nb = x.shape[0]
    weights = [lw[n] for n in FRONT_WEIGHTS]
    acts = (x,) + tuple(conv_rows) + (h0,)
    shapes = [(ATT_WIDTH, nb)] * 3 + [(nb, w) for w in (ATT_WIDTH, ATT_WIDTH, D_MODEL, D_MODEL, D_RNN, D_RNN)]
    return pl.pallas_call(
        _dec_front_kernel,
        grid=(1,),
        in_specs=([_const_spec(a.shape) for a in acts] + [_layer_spec(w, layer) for w in weights]
                  + [_const_spec(lw["pnorm"].shape)]),
        out_specs=[pl.BlockSpec(s, lambda i: (0, 0)) for s in shapes],
        out_shape=[jax.ShapeDtypeStruct(s, F32) for s in shapes],
        compiler_params=pltpu.CompilerParams(
            dimension_semantics=("arbitrary",), vmem_limit_bytes=VMEM_LIMIT_BYTES),
        name="decode_front",
    )(*acts, *weights, lw["pnorm"])


def _decode_attention_init(tabt_ref, bkt_refs, bias_refs, s_refs, ot_ref):
    ot_ref[...] = jnp.zeros(ot_ref.shape, F32)
    for g in range(N_GROUPS):
        s_refs[g][...] = jnp.zeros(s_refs[g].shape, F32)
        bkt = bkt_refs[g][...]
        for part in range(bias_refs[g].shape[0]):
            acc = jnp.full(bkt.shape, -jnp.inf, F32)
            for n in range(NUM_BUCKETS):
                acc = jnp.where(bkt == n, tabt_ref[g, part, :, n:n + 1], acc)
            bias_refs[g][part] = acc


def _decode_attention_step(seq, part, n_heads, qt_ref, kt_ref, vt_ref, c_refs, tabt_ref, bias_refs, s_refs,
                           ot_ref):
    mine = lax.broadcasted_iota(jnp.int32, (HEAD_DIM, qt_ref.shape[1]), 1) == seq

    def column(ref, g, h):
        row0 = pl.multiple_of(g * GROUP_WIDTH + (part * n_heads + h) * HEAD_DIM, HEAD_DIM)
        return jnp.sum(jnp.where(mine, ref[pl.ds(row0, HEAD_DIM), :], 0.0), axis=1, keepdims=True)

    def fold_lanes(t, op):
        out = t[:, 0:LANES]
        for c in range(1, t.shape[1] // LANES):
            out = op(out, t[:, c * LANES:(c + 1) * LANES])
        return out

    pad = [jnp.zeros((SUBLANES - n_heads, 1), F32)] if n_heads < SUBLANES else []
    s0, vcols = [], []
    for g in range(N_GROUPS):
        rows = []
        for h in range(n_heads):
            qh = column(qt_ref, g, h)
            s_refs[g][h:h + 1, :] = jnp.sum(c_refs[g][0, h] * qh, axis=0, keepdims=True)
            rows.append(jnp.sum(qh * column(kt_ref, g, h), axis=0, keepdims=True))
            vcols.append(column(vt_ref, g, h))
        s0.append(jnp.concatenate(rows + pad, axis=0) + tabt_ref[g, part, :, 0:1])

    ss = [s_refs[g][...] + bias_refs[g][part] for g in range(N_GROUPS)]
    m_max = jnp.maximum(jnp.maximum(s0[0], s0[1]), s0[2])
    for s in ss:
        m_max = jnp.maximum(m_max, jnp.max(fold_lanes(s, jnp.maximum), axis=1, keepdims=True))
    p0 = [jnp.exp(s - m_max) for s in s0]
    den = p0[0] + p0[1] + p0[2]
    for g, s in enumerate(ss):
        p = jnp.exp(s - m_max)
        s_refs[g][...] = p
        den = den + jnp.sum(fold_lanes(p, jnp.add), axis=1, keepdims=True)
    inv_den = 1.0 / den

    for h in range(n_heads):
        acc = None
        new = 0.0
        for g in range(N_GROUPS):
            f = fold_lanes(c_refs[g][1, h] * s_refs[g][h:h + 1, :], jnp.add)
            acc = f if acc is None else acc + f
            new = new + p0[g][h:h + 1, :] * vcols[g * n_heads + h]
        num = jnp.sum(acc, axis=1, keepdims=True) + new
        row0 = pl.multiple_of((part * n_heads + h) * HEAD_DIM, HEAD_DIM)
        ot_ref[pl.ds(row0, HEAD_DIM), :] = jnp.where(mine, num * inv_den[h:h + 1, :],
                                                      ot_ref[pl.ds(row0, HEAD_DIM), :])


def _decode_attention_operands(rel_bias, parts):
    n_heads = HEADS_PER_GROUP // parts
    tab = rel_bias.reshape(NUM_BUCKETS, N_GROUPS, parts, n_heads).transpose(1, 2, 3, 0)
    tab = jnp.pad(tab, ((0, 0), (0, 0), (0, SUBLANES - n_heads), (0, 0)))
    bkts = []
    for window, dil in DILATED_GROUPS:
        w = np.arange(window)
        bkt = np.where(w % dil == 0, _t5_bucket(window - w), -1).astype(np.int32)
        bkts.append(jnp.asarray(np.broadcast_to(bkt[None, :], (SUBLANES, window))))
    return tab, bkts


def _dec_back_kernel(x_ref, ylg_ref, gatt_ref, ot_ref, wba_ref, wo_ref, nf_ref, wfi_ref, wfo_ref, y_ref):
    y_att = _dot(ot_ref[...].T.astype(BF16), wba_ref[...])
    mix = ylg_ref[...] + gatt_ref[...] * y_att
    x1 = x_ref[...] + _dot(mix.astype(BF16), wo_ref[...])
    y_ref[...] = _ffn(x1, _rms_rows(x1, nf_ref[...]).astype(BF16), wfi_ref, wfo_ref)


def _dec_back_call(x, ylg, gatt, o, lw, layer):
    weights = [lw[n] for n in BACK_WEIGHTS]
    acts = (x, ylg, gatt, o)
    return pl.pallas_call(
        _dec_back_kernel,
        grid=(1,),
        in_specs=[_const_spec(a.shape) for a in acts] + [_layer_spec(w, layer) for w in weights],
        out_specs=pl.BlockSpec(x.shape, lambda i: (0, 0)),
        out_shape=jax.ShapeDtypeStruct(x.shape, F32),
        compiler_params=pltpu.CompilerParams(
            dimension_semantics=("arbitrary",), vmem_limit_bytes=VMEM_LIMIT_BYTES),
        name="decode_back",
    )(*acts, *weights)


def _prepare_weights(norm_mix, w_in, w_conv, b_conv, w_rg_a, b_rg_a, w_rg_x, b_rg_x, lru_lambda,
                     q_gain, k_gain, w_merge, b_merge, w_branch_lru, w_branch_att, w_o, norm_ffn,
                     w_ffn_in, w_ffn_out):
    depth = norm_mix.shape[0]
    row = lambda t: t.reshape(depth, 1, -1).astype(F32)
    per_tile = RG_TILE // LRU_BLOCK
    tiles = D_RNN // RG_TILE

    def block_diag(w):
        w = w.reshape(depth, tiles, per_tile, LRU_BLOCK, LRU_BLOCK)
        eye = jnp.eye(per_tile, dtype=w.dtype)
        return jnp.einsum("ltnij,nm->ltnimj", w, eye).reshape(depth, tiles, RG_TILE, RG_TILE)

    head = jnp.arange(MXU_DIM) // HEAD_DIM
    tile_gain = lambda t: jnp.tile(t.reshape(depth, 1, HEAD_DIM), (1, 1, HEADS_PER_GROUP)).astype(F32)
    return dict(
        norm_mix=row(norm_mix), w_in=w_in.astype(BF16), w_merge=w_merge.astype(BF16),
        b_merge=row(b_merge), w_conv=w_conv.astype(F32), b_conv=row(b_conv),
        w_rg=jnp.concatenate([block_diag(w_rg_a), block_diag(w_rg_x)], axis=3).astype(BF16),
        b_rg_a=row(b_rg_a), b_rg_x=row(b_rg_x), lam=row(lru_lambda),
        q_gain=tile_gain(q_gain) * (HEAD_DIM ** -0.5),
        k_gain=tile_gain(k_gain),
        w_branch_lru=w_branch_lru.astype(BF16),
        pnorm=((head[:, None] == head[None, :]).astype(F32) / HEAD_DIM).astype(BF16),
        w_branch_att=w_branch_att.astype(BF16), w_o=w_o.astype(BF16), norm_ffn=row(norm_ffn),
        w_ffn_in=w_ffn_in.astype(BF16), w_ffn_out=w_ffn_out.astype(BF16))


def kernel(x_prompt, x_sample, cache_kv_g0, cache_kv_g1, cache_kv_g2, state_conv, state_h, rel_bias,
           norm_mix, w_in, w_conv, b_conv, w_rg_a, b_rg_a, w_rg_x, b_rg_x, lru_lambda, q_gain, k_gain,
           w_merge, b_merge, w_branch_lru, w_branch_att, w_o, norm_ffn, w_ffn_in, w_ffn_out):
    B, S, _ = x_prompt.shape
    nb = x_sample.shape[0]
    depth = norm_mix.shape[0]
    lw = _prepare_weights(norm_mix, w_in, w_conv, b_conv, w_rg_a, b_rg_a, w_rg_x, b_rg_x, lru_lambda,
                          q_gain, k_gain, w_merge, b_merge, w_branch_lru, w_branch_att, w_o, norm_ffn,
                          w_ffn_in, w_ffn_out)
    rel_bias = rel_bias.astype(F32)
    caches_t = [jnp.transpose(c.astype(F32), (0, 1, 3, 4, 5, 2))
                for c in (cache_kv_g0, cache_kv_g1, cache_kv_g2)]

    yp = x_prompt
    ys = x_sample.reshape(nb, D_MODEL)
    kvp = [[] for _ in range(N_GROUPS)]
    kvs = [[] for _ in range(N_GROUPS)]
    conv_p, h_p, conv_s, h_s = [], [], [], []
    for l in range(depth):
        sc = state_conv[l].astype(F32)
        conv_rows = [sc[:, r, :] for r in range(CONV_WIDTH - 1)]
        qt, kt, vt, ks, vs, ylg_s, gatt_s, lrux_s, hs_s = _dec_front_call(
            ys, conv_rows, state_h[l].astype(F32), lw, l)

        (q0, q1, q2, k0, k1, k2, v0, v1, v2, ylg, gatt, kv0, kv1, kv2, ctail, hfin) = _front_call(
            yp, lw, l, TM_PROMPT)
        os_, lses = [], []
        for g, (qg, kg, vg) in enumerate(((q0, k0, v0), (q1, k1, v1), (q2, k2, v2))):
            tab = rel_bias[:, g * HEADS_PER_GROUP:(g + 1) * HEADS_PER_GROUP]
            o, lse = _attn_call(qg, kg, vg, tab, DILATED_GROUPS[g][1])
            os_.append(o)
            lses.append(lse)
        yp, o_t = _back_call(yp, ylg, gatt, os_, lses, lw, l, TM_PROMPT, qt, kt, vt, caches_t, rel_bias)
        for g, kv in enumerate((kv0, kv1, kv2)):
            kvp[g].append(kv.reshape(B, kv.shape[1], 2, HEADS_PER_GROUP, HEAD_DIM))
        conv_p.append(ctail[:, SUBLANES - (CONV_WIDTH - 1):, :])
        h_p.append(hfin[:, 0, :])

        ys = _dec_back_call(ys, ylg_s, gatt_s, o_t, lw, l)
        for g in range(N_GROUPS):
            sl = slice(g * GROUP_WIDTH, (g + 1) * GROUP_WIDTH)
            kvs[g].append(jnp.stack([ks[:, sl], vs[:, sl]], axis=1).reshape(
                nb, 1, 2, HEADS_PER_GROUP, HEAD_DIM))
        conv_s.append(jnp.stack(conv_rows[1:] + [lrux_s], axis=1))
        h_s.append(hs_s)

    stack = jnp.stack
    return (yp, ys.reshape(nb, 1, D_MODEL),
            stack(kvp[0]), stack(kvp[1]), stack(kvp[2]), stack(conv_p), stack(h_p),
            stack(kvs[0]), stack(kvs[1]), stack(kvs[2]), stack(conv_s), stack(h_s))
```

```python
import functools
import math

import numpy as np
import jax
import jax.numpy as jnp
from jax import lax
from jax.experimental import pallas as pl
from jax.experimental.pallas import tpu as pltpu

D_MODEL = 1024
D_RNN = D_MODEL
N_LRU_BLOCKS = 16
LRU_BLOCK = D_RNN // N_LRU_BLOCKS
CONV_WIDTH = 4
LRU_C = 8.0
HEAD_DIM = 64
HEADS_PER_GROUP = 8
DILATED_GROUPS = ((128, 1), (512, 4), (2048, 16))
N_GROUPS = len(DILATED_GROUPS)
N_ATT_HEADS = N_GROUPS * HEADS_PER_GROUP
ATT_WIDTH = N_ATT_HEADS * HEAD_DIM
GROUP_WIDTH = HEADS_PER_GROUP * HEAD_DIM
BAND_BLOCK = 128
NUM_BUCKETS = 32
MAX_DISTANCE = 2048
D_FF = 2816
EPS = 1e-6

F32 = jnp.float32
BF16 = jnp.bfloat16

MXU_DIM = 256
SUBLANES = 8
LANES = 128
VMEM_LIMIT_BYTES = 56 * 1024 * 1024

TM_PROMPT = 256
ATTN_QBLOCKS = 16
RG_TILE = MXU_DIM
FFN_CHUNKS = ((0, 768), (768, 768), (1536, 768), (2304, 512))


def _dot(a, b):
    return jnp.dot(a, b, preferred_element_type=F32)


def _dot_nt(a, b):
    return lax.dot_general(a, b, (((1,), (1,)), ((), ())), preferred_element_type=F32)


def _sigmoid(x):
    return 0.5 * (jnp.tanh(0.5 * x) + 1.0)


def _gelu_tanh(x):
    c = math.sqrt(2.0 / math.pi)
    return 0.5 * x * (1.0 + jnp.tanh(c * (x + 0.044715 * (x * x * x))))


def _softplus(z):
    return jnp.maximum(z, 0.0) + jnp.log1p(jnp.exp(-jnp.abs(z)))


def _rms_rows(x, g):
    y = x * lax.rsqrt(jnp.mean(x * x, axis=-1, keepdims=True) + EPS)
    return y * g


def _head_rms(t, pn, gain):
    t2 = (t * t).astype(BF16)
    tiles = [_dot(t2[:, c * MXU_DIM:(c + 1) * MXU_DIM], pn) for c in range(t.shape[1] // MXU_DIM)]
    ms = tiles[0] if len(tiles) == 1 else jnp.concatenate(tiles, axis=1)
    return (t * lax.rsqrt(ms + EPS)) * gain


def _lru_gates(g, b_a, b_x, sp):
    w = g.shape[1] // 2
    r = _sigmoid(g[:, :w] + b_a)
    ig = _sigmoid(g[:, w:] + b_x)
    log_a = (-LRU_C * r) * sp
    a = jnp.exp(log_a)
    mult = jnp.sqrt(-jnp.tanh(log_a) * (a * a + 1.0))
    return a, mult, ig


def _scan_rows(a, u, h_in):
    rows, c = a.shape
    groups = rows // SUBLANES
    a3 = a.reshape(groups, SUBLANES, c)
    u3 = u.reshape(groups, SUBLANES, c)
    row = lax.broadcasted_iota(jnp.int32, (groups, SUBLANES, c), 1)
    shift = 1
    while shift < SUBLANES:
        ok = row >= shift
        a_sh = jnp.where(ok, pltpu.roll(a3, shift, axis=1), 1.0)
        u_sh = jnp.where(ok, pltpu.roll(u3, shift, axis=1), 0.0)
        u3 = u3 + a3 * u_sh
        a3 = a3 * a_sh
        shift *= 2
    out = []
    h = h_in
    for g in range(groups):
        hg = a3[g] * h + u3[g]
        out.append(hg)
        h = hg[SUBLANES - 1:SUBLANES, :]
    return jnp.concatenate(out, axis=0), h


def _to_slabs(slab_ref, t):
    for c in range(slab_ref.shape[0]):
        slab_ref[c] = t[:, c * LANES:(c + 1) * LANES]


def _from_slabs(slab_ref):
    return jnp.concatenate([slab_ref[c] for c in range(slab_ref.shape[0])], axis=1)


def _gather_residues(slab_ref, dil):
    per = slab_ref.shape[1] // dil
    cols = []
    for c in range(slab_ref.shape[0]):
        cols.append(jnp.concatenate(
            [slab_ref[c, pl.ds(r, per, stride=dil), :] for r in range(dil)], axis=0))
    return jnp.concatenate(cols, axis=1)


def _scatter_residues(slab_ref, t, dil, first_slab=0):
    per = slab_ref.shape[1] // dil
    for c in range(t.shape[1] // LANES):
        for r in range(dil):
            slab_ref[first_slab + c, pl.ds(r, per, stride=dil), :] = (
                t[r * per:(r + 1) * per, c * LANES:(c + 1) * LANES])


def _gather_residues_2level(slab_ref, tmp_ref, dil):
    side = 4
    assert dil == side * side
    per = slab_ref.shape[1] // dil
    block = per * side
    cols = []
    for c in range(slab_ref.shape[0]):
        for r0 in range(side):
            tmp_ref[c, r0 * block:(r0 + 1) * block, :] = slab_ref[c, pl.ds(r0, block, stride=side), :]
        cols.append(jnp.concatenate(
            [tmp_ref[c, pl.ds((r % side) * block + r // side, per, stride=side), :] for r in range(dil)], axis=0))
    return jnp.concatenate(cols, axis=1)


def _scatter_residues_2level(slab_ref, tmp_ref, t, dil, first_slab=0):
    side = 4
    assert dil == side * side
    per = slab_ref.shape[1] // dil
    block = per * side
    for c in range(t.shape[1] // LANES):
        for r in range(dil):
            r1, r0 = divmod(r, side)
            tmp_ref[c, pl.ds(r0 * block + r1, per, stride=side), :] = (
                t[r * per:(r + 1) * per, c * LANES:(c + 1) * LANES])
        for r0 in range(side):
            slab_ref[first_slab + c, pl.ds(r0, block, stride=side), :] = tmp_ref[c, r0 * block:(r0 + 1) * block, :]


def _ffn(x1, xb, wfi_ref, wfo_ref):
    acc = x1
    for start, width in FFN_CHUNKS:
        gate = _dot(xb, wfi_ref[:, start:start + width])
        up = _dot(xb, wfi_ref[:, D_FF + start:D_FF + start + width])
        hid = (gate * _sigmoid(gate)) * up
        acc = acc + _dot(hid.astype(BF16), wfo_ref[start:start + width, :])
    return acc


def _layer_spec(w, layer):
    tail = (0,) * (w.ndim - 1)
    return pl.BlockSpec((None,) + w.shape[1:], lambda *_: (layer,) + tail, pipeline_mode=pl.Buffered(1))


def _const_spec(shape):
    zeros = (0,) * len(shape)
    return pl.BlockSpec(shape, lambda *_: zeros, pipeline_mode=pl.Buffered(1))


FRONT_WEIGHTS = ("norm_mix", "w_in", "w_merge", "b_merge", "w_conv", "b_conv", "w_rg", "b_rg_a", "b_rg_x",
                 "lam", "q_gain", "k_gain", "w_branch_lru")
BACK_WEIGHTS = ("w_branch_att", "w_o", "norm_ffn", "w_ffn_in", "w_ffn_out")


def _front_kernel(x_ref, nm_ref, win_ref, wmg_ref, bmg_ref, wconv_ref, bconv_ref, wrg_ref,
                  brga_ref, brgx_ref, lam_ref, qg_ref, kg_ref, wbl_ref, pn_ref,
                  q0_ref, q1_ref, q2_ref, k0_ref, k1_ref, k2_ref, v0_ref, v1_ref, v2_ref,
                  ylg_ref, gatt_ref, kvp0_ref, kvp1_ref, kvp2_ref, convp_ref, hp_ref,
                  xn_ref, kv_ref, tmp_ref, ext_ref, h_ref, xb_ref, xg_ref, xc_ref, xcb_ref, gl_ref, y_ref, *, tm):
    i = pl.program_id(1)
    n_tiles = D_RNN // RG_TILE

    @pl.when(i == 0)
    def _():
        ext_ref[0:SUBLANES, :] = jnp.zeros((SUBLANES, D_RNN), F32)
        h_ref[...] = jnp.zeros((SUBLANES, D_RNN), F32)

    xn = _rms_rows(x_ref[...], nm_ref[...])
    _to_slabs(xn_ref, xn)
    xb_ref[...] = xn.astype(BF16)

    for t in range(n_tiles):
        sl = slice(t * RG_TILE, (t + 1) * RG_TILE)
        lru_x = _dot(xb_ref[...], win_ref[:, sl])
        ext_ref[SUBLANES:SUBLANES + tm, sl] = lru_x
        xc = bconv_ref[:, sl] + wconv_ref[0:1, sl] * ext_ref[SUBLANES - 3:SUBLANES - 3 + tm, sl]
        xc = xc + wconv_ref[1:2, sl] * ext_ref[SUBLANES - 2:SUBLANES - 2 + tm, sl]
        xc = xc + wconv_ref[2:3, sl] * ext_ref[SUBLANES - 1:SUBLANES - 1 + tm, sl]
        xc = xc + wconv_ref[3:4, sl] * lru_x
        xc_ref[:, sl] = xc
        xcb_ref[:, sl] = xc.astype(BF16)
        gl_ref[:, sl] = _gelu_tanh(_dot(xb_ref[...], win_ref[:, D_RNN + t * RG_TILE:D_RNN + (t + 1) * RG_TILE]))
    tail = ext_ref[tm:tm + SUBLANES, :]
    ext_ref[0:SUBLANES, :] = tail
    convp_ref[...] = tail

    sp = _softplus(-lam_ref[...])
    first_row = (lax.broadcasted_iota(jnp.int32, (tm, RG_TILE), 0) == 0) & (i == 0)
    gates = {}

    def lru_gates(t):
        sl = slice(t * RG_TILE, (t + 1) * RG_TILE)
        g = _dot(xcb_ref[:, sl], wrg_ref[t])
        a, mult, ig = _lru_gates(g, brga_ref[:, sl], brgx_ref[:, sl], sp[:, sl])
        mult = jnp.where(first_row, 1.0, mult)
        gates[t] = (a, (mult * ig) * xc_ref[:, sl])

    def lru_scan(t):
        sl = slice(t * RG_TILE, (t + 1) * RG_TILE)
        a, u = gates.pop(t)
        hs, h_last = _scan_rows(a, u, h_ref[0:1, sl])
        h_ref[:, sl] = jnp.broadcast_to(h_last, (SUBLANES, RG_TILE))
        y_ref[:, sl] = (hs * gl_ref[:, sl]).astype(BF16)

    pn = pn_ref[...]
    q_refs = (q0_ref, q1_ref, q2_ref)
    k_refs = (k0_ref, k1_ref, k2_ref)
    v_refs = (v0_ref, v1_ref, v2_ref)
    kvp_refs = (kvp0_ref, kvp1_ref, kvp2_ref)

    def attn_operands(g, half):
        window, dil = DILATED_GROUPS[g]
        per = tm // dil
        keep = min(window, tm)
        if dil == 1:
            src = xb_ref
        else:
            if half == 0:
                gathered = (_gather_residues_2level(xn_ref, tmp_ref, dil) if dil == 16
                            else _gather_residues(xn_ref, dil))
                xg_ref[...] = gathered.astype(BF16)
            src = xg_ref
        c = 2 * D_RNN + g * GROUP_WIDTH + half * MXU_DIM
        hl = slice(half * MXU_DIM, (half + 1) * MXU_DIM)
        vl = slice(GROUP_WIDTH + half * MXU_DIM, GROUP_WIDTH + (half + 1) * MXU_DIM)
        q = _dot(src[...], win_ref[:, c:c + MXU_DIM])
        k = _dot(src[...], win_ref[:, c + ATT_WIDTH:c + ATT_WIDTH + MXU_DIM])
        v = _dot(src[...], win_ref[:, c + 2 * ATT_WIDTH:c + 2 * ATT_WIDTH + MXU_DIM])
        qn = _head_rms(q, pn, qg_ref[:, hl]).astype(BF16)
        kn = _head_rms(k, pn, kg_ref[:, hl])
        knb = kn.astype(BF16)
        vb = v.astype(BF16)
        for r in range(dil):
            rows = slice(r * per, (r + 1) * per)
            q_refs[g][r, :, hl] = qn[rows]
            k_refs[g][r, :, hl] = knb[rows]
            v_refs[g][r, :, hl] = vb[rows]
        if dil == 1:
            kvp_refs[g][:, hl] = kn[tm - keep:, :]
            kvp_refs[g][:, vl] = v[tm - keep:, :]
        else:
            if dil == 16:
                _scatter_residues_2level(kv_ref, tmp_ref, kn, dil, first_slab=hl.start // LANES)
                _scatter_residues_2level(kv_ref, tmp_ref, v, dil, first_slab=vl.start // LANES)
            else:
                _scatter_residues(kv_ref, kn, dil, first_slab=hl.start // LANES)
                _scatter_residues(kv_ref, v, dil, first_slab=vl.start // LANES)
            if half == GROUP_WIDTH // MXU_DIM - 1:
                kvp_refs[g][...] = _from_slabs(kv_ref)

    def att_gate(t):
        sl = slice(t * RG_TILE, (t + 1) * RG_TILE)
        al = slice(D_MODEL + t * RG_TILE, D_MODEL + (t + 1) * RG_TILE)
        gatt_ref[:, sl] = _sigmoid(_dot(xb_ref[...], wmg_ref[:, al]) + bmg_ref[:, al]).astype(gatt_ref.dtype)

    for t in range(n_tiles):
        lru_gates(t)
        if t < N_GROUPS:
            attn_operands(t, 0)
        else:
            att_gate(0)
            att_gate(1)
        lru_scan(t)
        if t < N_GROUPS:
            attn_operands(t, 1)
        else:
            att_gate(2)
            att_gate(3)
    hp_ref[...] = h_ref[...]
    for t in range(n_tiles):
        sl = slice(t * RG_TILE, (t + 1) * RG_TILE)
        g_lru = _sigmoid(_dot(xb_ref[...], wmg_ref[:, sl]) + bmg_ref[:, sl])
        ylg_ref[:, sl] = (g_lru * _dot(y_ref[...], wbl_ref[:, sl])).astype(ylg_ref.dtype)


def _front_call(x, lw, layer, tm):
    B, S, _ = x.shape
    nt = S // tm
    row_spec = lambda width: pl.BlockSpec((None, tm, width), lambda b, i: (b, i, 0))
    weights = [lw[n] for n in FRONT_WEIGHTS]
    in_specs = ([row_spec(D_MODEL)] + [_layer_spec(w, layer) for w in weights]
                + [_const_spec(lw["pnorm"].shape)])

    out_shape, out_specs = [], []
    for _ in range(3):
        for _, dil in DILATED_GROUPS:
            out_shape.append(jax.ShapeDtypeStruct((B, dil, S // dil, GROUP_WIDTH), BF16))
            out_specs.append(pl.BlockSpec((None, dil, tm // dil, GROUP_WIDTH), lambda b, i: (b, 0, i, 0)))
    out_shape.append(jax.ShapeDtypeStruct((B, S, D_MODEL), BF16))
    out_specs.append(row_spec(D_MODEL))
    out_shape.append(jax.ShapeDtypeStruct((B, S, D_MODEL), BF16))
    out_specs.append(row_spec(D_MODEL))
    for window, _ in DILATED_GROUPS:
        keep = min(window, tm)
        first = nt - window // keep
        out_shape.append(jax.ShapeDtypeStruct((B, window, 2 * GROUP_WIDTH), F32))
        out_specs.append(pl.BlockSpec(
            (None, keep, 2 * GROUP_WIDTH),
            functools.partial(lambda b, i, first: (b, jnp.maximum(i - first, 0), 0), first=first)))
    for _ in range(2):
        out_shape.append(jax.ShapeDtypeStruct((B, SUBLANES, D_RNN), F32))
        out_specs.append(pl.BlockSpec((None, SUBLANES, D_RNN), lambda b, i: (b, 0, 0)))

    return pl.pallas_call(
        functools.partial(_front_kernel, tm=tm),
        grid=(B, nt),
        in_specs=in_specs,
        out_specs=out_specs,
        out_shape=out_shape,
        scratch_shapes=[pltpu.VMEM((D_MODEL // LANES, tm, LANES), F32),
                        pltpu.VMEM((2 * GROUP_WIDTH // LANES, tm, LANES), F32),
                        pltpu.VMEM((D_MODEL // LANES, tm, LANES), F32),
                        pltpu.VMEM((tm + SUBLANES, D_RNN), F32), pltpu.VMEM((SUBLANES, D_RNN), F32),
                        pltpu.VMEM((tm, D_MODEL), BF16), pltpu.VMEM((tm, D_MODEL), BF16),
                        pltpu.VMEM((tm, D_RNN), F32), pltpu.VMEM((tm, D_RNN), BF16),
                        pltpu.VMEM((tm, D_RNN), F32), pltpu.VMEM((tm, D_RNN), BF16)],
        compiler_params=pltpu.CompilerParams(
            dimension_semantics=("arbitrary", "arbitrary"), vmem_limit_bytes=VMEM_LIMIT_BYTES),
        name="prompt_front",
    )(x, *weights, lw["pnorm"])


def _attn_kernel(tab_ref, bkt_ref, q_ref, kp_ref, kc_ref, vp_ref, vc_ref, o_ref, lse_ref,
                 bias_ref, kbuf_ref, vbuf_ref, *, qblocks):
    b, r, j = pl.program_id(0), pl.program_id(1), pl.program_id(2)
    BB = BAND_BLOCK
    PAIR = 2 * HEAD_DIM

    @pl.when((b == 0) & (r == 0) & (j == 0))
    def _():
        bkt = bkt_ref[...]
        for h in range(HEADS_PER_GROUP):
            acc = jnp.full(bkt.shape, -jnp.inf, F32)
            for n in range(NUM_BUCKETS):
                acc = jnp.where(bkt == n, tab_ref[n, h], acc)
            bias_ref[h // 2, (h % 2) * BB:(h % 2 + 1) * BB, :] = acc

    first_col = lax.broadcasted_iota(jnp.int32, (1, 2 * BB), 1) < BB
    pen = jnp.where(first_col & (j == 0), -jnp.inf, 0.0).astype(F32)
    low_half = lax.broadcasted_iota(jnp.int32, (BB, PAIR), 1) < HEAD_DIM
    ones = jnp.ones((2 * BB, PAIR), BF16)
    for rr in range(q_ref.shape[0]):
        kbuf_ref[0:BB, :] = kp_ref[rr]
        kbuf_ref[BB:, :] = kc_ref[rr]
        vbuf_ref[0:BB, :] = vp_ref[rr]
        vbuf_ref[BB:, :] = vc_ref[rr]
        for t in range(qblocks):
            rows = slice(t * BB, (t + 1) * BB)
            keys = slice(t * BB, (t + 2) * BB)
            for hp in range(HEADS_PER_GROUP // 2):
                sl = slice(hp * PAIR, (hp + 1) * PAIR)
                qp = q_ref[rr, rows, sl]
                zero = jnp.zeros_like(qp)
                lhs = jnp.concatenate([jnp.where(low_half, qp, zero), jnp.where(low_half, zero, qp)], axis=0)
                s = _dot_nt(lhs, kbuf_ref[keys, sl]) + bias_ref[hp]
                if t == 0:
                    s = s + pen
                m = jnp.max(s, axis=-1, keepdims=True)
                p = jnp.exp((s - m).astype(BF16))
                oe = _dot(p, jnp.concatenate([vbuf_ref[keys, sl], ones], axis=1))
                l = oe[:, PAIR:]
                o = oe[:, :PAIR] / l
                lse = m + jnp.log(l)
                for e in range(2):
                    hl = slice(hp * PAIR + e * HEAD_DIM, hp * PAIR + (e + 1) * HEAD_DIM)
                    o_ref[rr, rows, hl] = o[e * BB:(e + 1) * BB, e * HEAD_DIM:(e + 1) * HEAD_DIM]
                    lse_ref[rr, rows, hl] = lse[e * BB:(e + 1) * BB, e * HEAD_DIM:(e + 1) * HEAD_DIM]


def _t5_bucket(dist):
    max_exact = NUM_BUCKETS // 2
    d_f = np.maximum(dist, 1).astype(np.float32)
    large = max_exact + (np.log(d_f / np.float32(max_exact)) / np.float32(math.log(MAX_DISTANCE / max_exact))
                         * np.float32(NUM_BUCKETS - max_exact)).astype(np.int32)
    large = np.minimum(large, NUM_BUCKETS - 1)
    return np.where(dist < max_exact, dist, large).astype(np.int32)


def _band_buckets(dil):
    BB = BAND_BLOCK
    qi = np.arange(BB)[:, None]
    ki = np.arange(2 * BB)[None, :]
    sub = qi + BB - ki
    in_band = (sub >= 0) & (sub <= BB)
    return jnp.asarray(np.where(in_band, _t5_bucket(np.clip(sub, 0, BB) * dil), -1).astype(np.int32))


def _attn_call(q, k, v, tab, dil):
    B, _, n, C = q.shape
    BB = BAND_BLOCK
    qb = min(ATTN_QBLOCKS, n // BB)
    res = ATTN_QBLOCKS // qb
    cur = pl.BlockSpec((None, res, qb * BB, C), lambda b, r, j: (b, r, j, 0))
    prev = pl.BlockSpec((None, res, BB, C), lambda b, r, j: (b, r, jnp.maximum(j * qb - 1, 0), 0))
    return pl.pallas_call(
        functools.partial(_attn_kernel, qblocks=qb),
        grid=(B, dil // res, n // (qb * BB)),
        in_specs=[pl.BlockSpec(memory_space=pltpu.SMEM),
                  pl.BlockSpec((BB, 2 * BB), lambda b, r, j: (0, 0)),
                  cur, prev, cur, prev, cur],
        out_specs=[cur, cur],
        out_shape=[jax.ShapeDtypeStruct(q.shape, F32)] * 2,
        scratch_shapes=[pltpu.VMEM((HEADS_PER_GROUP // 2, 2 * BB, 2 * BB), F32),
                        pltpu.VMEM(((qb + 1) * BB, C), BF16), pltpu.VMEM(((qb + 1) * BB, C), BF16)],
        compiler_params=pltpu.CompilerParams(
            dimension_semantics=("arbitrary", "arbitrary", "arbitrary"), vmem_limit_bytes=VMEM_LIMIT_BYTES),
        name="prompt_attn_d%d" % dil,
    )(tab, _band_buckets(dil), q, k, k, v, v)


def _back_kernel(x_ref, ylg_ref, gatt_ref, o0_ref, o1_ref, o2_ref, l0_ref, l1_ref, l2_ref,
                 wba_ref, wo_ref, nf_ref, wfi_ref, wfo_ref,
                 qt_ref, kt_ref, vt_ref, c0_ref, c1_ref, c2_ref, tabt_ref, bkt0_ref, bkt1_ref, bkt2_ref,
                 y_ref, ot_ref,
                 so1_ref, so2_ref, sl1_ref, sl2_ref, tmp_ref,
                 bias0_ref, bias1_ref, bias2_ref, s0_ref, s1_ref, s2_ref,
                 *, tm, dec_parts):
    step = pl.program_id(0) * pl.num_programs(1) + pl.program_id(1)
    bias_refs = (bias0_ref, bias1_ref, bias2_ref)
    s_refs = (s0_ref, s1_ref, s2_ref)

    @pl.when(step == 0)
    def _():
        _decode_attention_init(tabt_ref, (bkt0_ref, bkt1_ref, bkt2_ref), bias_refs, s_refs, ot_ref)

    _decode_attention_step(step // dec_parts, step % dec_parts, HEADS_PER_GROUP // dec_parts,
                           qt_ref, kt_ref, vt_ref, (c0_ref, c1_ref, c2_ref), tabt_ref, bias_refs, s_refs, ot_ref)

    for (_, dil), src, dst in ((DILATED_GROUPS[1], o1_ref, so1_ref), (DILATED_GROUPS[2], o2_ref, so2_ref),
                               (DILATED_GROUPS[1], l1_ref, sl1_ref), (DILATED_GROUPS[2], l2_ref, sl2_ref)):
        if dil == 16:
            _scatter_residues_2level(dst, tmp_ref, src[...].reshape(tm, GROUP_WIDTH), dil)
        else:
            _scatter_residues(dst, src[...].reshape(tm, GROUP_WIDTH), dil)
    os_ = (o0_ref[0], _from_slabs(so1_ref), _from_slabs(so2_ref))
    lses = (l0_ref[0], _from_slabs(sl1_ref), _from_slabs(sl2_ref))

    top = jnp.maximum(jnp.maximum(lses[0], lses[1]), lses[2])
    num = 0.0
    den = 0.0
    for o, lse in zip(os_, lses):
        z = jnp.exp(lse - top)
        num = num + z * o
        den = den + z
    o = num / den

    y_att = _dot(o.astype(BF16), wba_ref[...])
    mix = ylg_ref[...] + gatt_ref[...] * y_att
    x1 = x_ref[...] + _dot(mix.astype(BF16), wo_ref[...])
    y_ref[...] = _ffn(x1, _rms_rows(x1, nf_ref[...]).astype(BF16), wfi_ref, wfo_ref)


def _back_call(x, ylg, gatt, os_, lses, lw, layer, tm, qt, kt, vt, caches_t, rel_bias):
    B, S, _ = x.shape
    nt = S // tm
    nb = qt.shape[1]
    dec_parts = (B * nt) // nb
    assert dec_parts * nb == B * nt and HEADS_PER_GROUP % dec_parts == 0
    n_heads = HEADS_PER_GROUP // dec_parts
    tabt, bkts = _decode_attention_operands(rel_bias, dec_parts)

    row_spec = lambda width: pl.BlockSpec((None, tm, width), lambda b, i: (b, i, 0))
    res_specs = [pl.BlockSpec((None, dil, tm // dil, GROUP_WIDTH), lambda b, i: (b, 0, i, 0))
                 for _, dil in DILATED_GROUPS]
    cache_spec = lambda c: pl.BlockSpec(
        (None, None, 2, n_heads) + c.shape[4:],
        lambda b, i: (layer, (b * nt + i) // dec_parts, 0, (b * nt + i) % dec_parts, 0, 0))
    weights = [lw[n] for n in BACK_WEIGHTS]
    in_specs = ([row_spec(D_MODEL)] * 3 + res_specs * 2 + [_layer_spec(w, layer) for w in weights]
                + [_const_spec(qt.shape)] * 3 + [cache_spec(c) for c in caches_t]
                + [_const_spec(tabt.shape)] + [_const_spec(b.shape) for b in bkts])
    return pl.pallas_call(
        functools.partial(_back_kernel, tm=tm, dec_parts=dec_parts),
        grid=(B, nt),
        in_specs=in_specs,
        out_specs=[row_spec(D_MODEL), pl.BlockSpec((GROUP_WIDTH, nb), lambda b, i: (0, 0))],
        out_shape=[jax.ShapeDtypeStruct((B, S, D_MODEL), F32), jax.ShapeDtypeStruct((GROUP_WIDTH, nb), F32)],
        scratch_shapes=[pltpu.VMEM((GROUP_WIDTH // LANES, tm, LANES), F32)] * 5
        + [pltpu.VMEM((dec_parts, SUBLANES, window), F32) for window, _ in DILATED_GROUPS]
        + [pltpu.VMEM((SUBLANES, window), F32) for window, _ in DILATED_GROUPS],
        compiler_params=pltpu.CompilerParams(
            dimension_semantics=("arbitrary", "arbitrary"), vmem_limit_bytes=VMEM_LIMIT_BYTES),
        name="prompt_back",
    )(x, ylg, gatt, *os_, *lses, *weights, qt, kt, vt, *caches_t, tabt, *bkts)


def _dec_front_kernel(x_ref, c0_ref, c1_ref, c2_ref, h0_ref, nm_ref, win_ref, wmg_ref, bmg_ref,
                      wconv_ref, bconv_ref, wrg_ref, brga_ref, brgx_ref, lam_ref, qg_ref, kg_ref,
                      wbl_ref, pn_ref,
                      qt_ref, kt_ref, vt_ref, k_ref, v_ref, ylg_ref, gatt_ref, lrux_ref, hs_ref):
    xb = _rms_rows(x_ref[...], nm_ref[...]).astype(BF16)
    lru_x = _dot(xb, win_ref[:, 0:D_RNN])
    lrux_ref[...] = lru_x
    xc = bconv_ref[...] + wconv_ref[0:1, :] * c0_ref[...]
    xc = xc + wconv_ref[1:2, :] * c1_ref[...]
    xc = xc + wconv_ref[2:3, :] * c2_ref[...]
    xc = xc + wconv_ref[3:4, :] * lru_x

    lru_g = _dot(xb, win_ref[:, D_RNN:2 * D_RNN])
    xcb = xc.astype(BF16)
    sp = _softplus(-lam_ref[...])
    ys = []
    for t in range(D_RNN // RG_TILE):
        sl = slice(t * RG_TILE, (t + 1) * RG_TILE)
        g = _dot(xcb[:, sl], wrg_ref[t])
        a, mult, ig = _lru_gates(g, brga_ref[:, sl], brgx_ref[:, sl], sp[:, sl])
        h = a * h0_ref[:, sl] + (mult * ig) * xc[:, sl]
        hs_ref[:, sl] = h
        ys.append((h * _gelu_tanh(lru_g[:, sl])).astype(BF16))
    y_lru = _dot(jnp.concatenate(ys, axis=1), wbl_ref[...])
    g_lru = _sigmoid(_dot(xb, wmg_ref[:, 0:D_MODEL]) + bmg_ref[:, 0:D_MODEL])
    ylg_ref[...] = g_lru * y_lru
    gatt_ref[...] = _sigmoid(_dot(xb, wmg_ref[:, D_MODEL:2 * D_MODEL]) + bmg_ref[:, D_MODEL:2 * D_MODEL])

    pn = pn_ref[...]
    for g in range(N_GROUPS):
        c = 2 * D_RNN + g * GROUP_WIDTH
        sl = slice(g * GROUP_WIDTH, (g + 1) * GROUP_WIDTH)
        q = _dot(xb, win_ref[:, c:c + GROUP_WIDTH])
        k = _dot(xb, win_ref[:, c + ATT_WIDTH:c + ATT_WIDTH + GROUP_WIDTH])
        v = _dot(xb, win_ref[:, c + 2 * ATT_WIDTH:c + 2 * ATT_WIDTH + GROUP_WIDTH])
        qn = _head_rms(q, pn, qg_ref[...])
        kn = _head_rms(k, pn, kg_ref[...])
        k_ref[:, sl] = kn
        v_ref[:, sl] = v
        qt_ref[sl, :] = qn.T
        kt_ref[sl, :] = kn.T
        vt_ref[sl, :] = v.T


def _dec_front_call(x, conv_rows, h0, lw, layer):
    nb = x.shape[0]
    weights = [lw[n] for n in FRONT_WEIGHTS]
    acts = (x,) + tuple(conv_rows) + (h0,)
    shapes = [(ATT_WIDTH, nb)] * 3 + [(nb, w) for w in (ATT_WIDTH, ATT_WIDTH, D_MODEL, D_MODEL, D_RNN, D_RNN)]
    return pl.pallas_call(
        _dec_front_kernel,
        grid=(1,),
        in_specs=([_const_spec(a.shape) for a in acts] + [_layer_spec(w, layer) for w in weights]
                  + [_const_spec(lw["pnorm"].shape)]),
        out_specs=[pl.BlockSpec(s, lambda i: (0, 0)) for s in shapes],
        out_shape=[jax.ShapeDtypeStruct(s, F32) for s in shapes],
        compiler_params=pltpu.CompilerParams(
            dimension_semantics=("arbitrary",), vmem_limit_bytes=VMEM_LIMIT_BYTES),
        name="decode_front",
    )(*acts, *weights, lw["pnorm"])


def _decode_attention_init(tabt_ref, bkt_refs, bias_refs, s_refs, ot_ref):
    ot_ref[...] = jnp.zeros(ot_ref.shape, F32)
    for g in range(N_GROUPS):
        s_refs[g][...] = jnp.zeros(s_refs[g].shape, F32)
        bkt = bkt_refs[g][...]
        for part in range(bias_refs[g].shape[0]):
            acc = jnp.full(bkt.shape, -jnp.inf, F32)
            for n in range(NUM_BUCKETS):
                acc = jnp.where(bkt == n, tabt_ref[g, part, :, n:n + 1], acc)
            bias_refs[g][part] = acc


def _decode_attention_step(seq, part, n_heads, qt_ref, kt_ref, vt_ref, c_refs, tabt_ref, bias_refs, s_refs,
                           ot_ref):
    mine = lax.broadcasted_iota(jnp.int32, (HEAD_DIM, qt_ref.shape[1]), 1) == seq

    def column(ref, g, h):
        row0 = pl.multiple_of(g * GROUP_WIDTH + (part * n_heads + h) * HEAD_DIM, HEAD_DIM)
        return jnp.sum(jnp.where(mine, ref[pl.ds(row0, HEAD_DIM), :], 0.0), axis=1, keepdims=True)

    def fold_lanes(t, op):
        out = t[:, 0:LANES]
        for c in range(1, t.shape[1] // LANES):
            out = op(out, t[:, c * LANES:(c + 1) * LANES])
        return out

    pad = [jnp.zeros((SUBLANES - n_heads, 1), F32)] if n_heads < SUBLANES else []
    s0, vcols = [], []
    for g in range(N_GROUPS):
        rows = []
        for h in range(n_heads):
            qh = column(qt_ref, g, h)
            s_refs[g][h:h + 1, :] = jnp.sum(c_refs[g][0, h] * qh, axis=0, keepdims=True)
            rows.append(jnp.sum(qh * column(kt_ref, g, h), axis=0, keepdims=True))
            vcols.append(column(vt_ref, g, h))
        s0.append(jnp.concatenate(rows + pad, axis=0) + tabt_ref[g, part, :, 0:1])

    ss = [s_refs[g][...] + bias_refs[g][part] for g in range(N_GROUPS)]
    m_max = jnp.maximum(jnp.maximum(s0[0], s0[1]), s0[2])
    for s in ss:
        m_max = jnp.maximum(m_max, jnp.max(fold_lanes(s, jnp.maximum), axis=1, keepdims=True))
    p0 = [jnp.exp(s - m_max) for s in s0]
    den = p0[0] + p0[1] + p0[2]
    for g, s in enumerate(ss):
        p = jnp.exp(s - m_max)
        s_refs[g][...] = p
        den = den + jnp.sum(fold_lanes(p, jnp.add), axis=1, keepdims=True)
    inv_den = 1.0 / den

    for h in range(n_heads):
        acc = None
        new = 0.0
        for g in range(N_GROUPS):
            f = fold_lanes(c_refs[g][1, h] * s_refs[g][h:h + 1, :], jnp.add)
            acc = f if acc is None else acc + f
            new = new + p0[g][h:h + 1, :] * vcols[g * n_heads + h]
        num = jnp.sum(acc, axis=1, keepdims=True) + new
        row0 = pl.multiple_of((part * n_heads + h) * HEAD_DIM, HEAD_DIM)
        ot_ref[pl.ds(row0, HEAD_DIM), :] = jnp.where(mine, num * inv_den[h:h + 1, :],
                                                      ot_ref[pl.ds(row0, HEAD_DIM), :])


def _decode_attention_operands(rel_bias, parts):
    n_heads = HEADS_PER_GROUP // parts
    tab = rel_bias.reshape(NUM_BUCKETS, N_GROUPS, parts, n_heads).transpose(1, 2, 3, 0)
    tab = jnp.pad(tab, ((0, 0), (0, 0), (0, SUBLANES - n_heads), (0, 0)))
    bkts = []
    for window, dil in DILATED_GROUPS:
        w = np.arange(window)
        bkt = np.where(w % dil == 0, _t5_bucket(window - w), -1).astype(np.int32)
        bkts.append(jnp.asarray(np.broadcast_to(bkt[None, :], (SUBLANES, window))))
    return tab, bkts


def _dec_back_kernel(x_ref, ylg_ref, gatt_ref, ot_ref, wba_ref, wo_ref, nf_ref, wfi_ref, wfo_ref, y_ref):
    y_att = _dot(ot_ref[...].T.astype(BF16), wba_ref[...])
    mix = ylg_ref[...] + gatt_ref[...] * y_att
    x1 = x_ref[...] + _dot(mix.astype(BF16), wo_ref[...])
    y_ref[...] = _ffn(x1, _rms_rows(x1, nf_ref[...]).astype(BF16), wfi_ref, wfo_ref)


def _dec_back_call(x, ylg, gatt, o, lw, layer):
    weights = [lw[n] for n in BACK_WEIGHTS]
    acts = (x, ylg, gatt, o)
    return pl.pallas_call(
        _dec_back_kernel,
        grid=(1,),
        in_specs=[_const_spec(a.shape) for a in acts] + [_layer_spec(w, layer) for w in weights],
        out_specs=pl.BlockSpec(x.shape, lambda i: (0, 0)),
        out_shape=jax.ShapeDtypeStruct(x.shape, F32),
        compiler_params=pltpu.CompilerParams(
            dimension_semantics=("arbitrary",), vmem_limit_bytes=VMEM_LIMIT_BYTES),
        name="decode_back",
    )(*acts, *weights)


def _prepare_weights(norm_mix, w_in, w_conv, b_conv, w_rg_a, b_rg_a, w_rg_x, b_rg_x, lru_lambda,
                     q_gain, k_gain, w_merge, b_merge, w_branch_lru, w_branch_att, w_o, norm_ffn,
                     w_ffn_in, w_ffn_out):
    depth = norm_mix.shape[0]
    row = lambda t: t.reshape(depth, 1, -1).astype(F32)
    per_tile = RG_TILE // LRU_BLOCK
    tiles = D_RNN // RG_TILE

    def block_diag(w):
        w = w.reshape(depth, tiles, per_tile, LRU_BLOCK, LRU_BLOCK)
        eye = jnp.eye(per_tile, dtype=w.dtype)
        return jnp.einsum("ltnij,nm->ltnimj", w, eye).reshape(depth, tiles, RG_TILE, RG_TILE)

    head = jnp.arange(MXU_DIM) // HEAD_DIM
    tile_gain = lambda t: jnp.tile(t.reshape(depth, 1, HEAD_DIM), (1, 1, HEADS_PER_GROUP)).astype(F32)
    return dict(
        norm_mix=row(norm_mix), w_in=w_in.astype(BF16), w_merge=w_merge.astype(BF16),
        b_merge=row(b_merge), w_conv=w_conv.astype(F32), b_conv=row(b_conv),
        w_rg=jnp.concatenate([block_diag(w_rg_a), block_diag(w_rg_x)], axis=3).astype(BF16),
        b_rg_a=row(b_rg_a), b_rg_x=row(b_rg_x), lam=row(lru_lambda),
        q_gain=tile_gain(q_gain) * (HEAD_DIM ** -0.5),
        k_gain=tile_gain(k_gain),
        w_branch_lru=w_branch_lru.astype(BF16),
        pnorm=((head[:, None] == head[None, :]).astype(F32) / HEAD_DIM).astype(BF16),
        w_branch_att=w_branch_att.astype(BF16), w_o=w_o.astype(BF16), norm_ffn=row(norm_ffn),
        w_ffn_in=w_ffn_in.astype(BF16), w_ffn_out=w_ffn_out.astype(BF16))


def kernel(x_prompt, x_sample, cache_kv_g0, cache_kv_g1, cache_kv_g2, state_conv, state_h, rel_bias,
           norm_mix, w_in, w_conv, b_conv, w_rg_a, b_rg_a, w_rg_x, b_rg_x, lru_lambda, q_gain, k_gain,
           w_merge, b_merge, w_branch_lru, w_branch_att, w_o, norm_ffn, w_ffn_in, w_ffn_out):
    B, S, _ = x_prompt.shape
    nb = x_sample.shape[0]
    depth = norm_mix.shape[0]
    lw = _prepare_weights(norm_mix, w_in, w_conv, b_conv, w_rg_a, b_rg_a, w_rg_x, b_rg_x, lru_lambda,
                          q_gain, k_gain, w_merge, b_merge, w_branch_lru, w_branch_att, w_o, norm_ffn,
                          w_ffn_in, w_ffn_out)
    rel_bias = rel_bias.astype(F32)
    caches_t = [jnp.transpose(c.astype(F32), (0, 1, 3, 4, 5, 2))
                for c in (cache_kv_g0, cache_kv_g1, cache_kv_g2)]

    yp = x_prompt
    ys = x_sample.reshape(nb, D_MODEL)
    kvp = [[] for _ in range(N_GROUPS)]
    kvs = [[] for _ in range(N_GROUPS)]
    conv_p, h_p, conv_s, h_s = [], [], [], []
    for l in range(depth):
        sc = state_conv[l].astype(F32)
        conv_rows = [sc[:, r, :] for r in range(CONV_WIDTH - 1)]
        qt, kt, vt, ks, vs, ylg_s, gatt_s, lrux_s, hs_s = _dec_front_call(
            ys, conv_rows, state_h[l].astype(F32), lw, l)

        (q0, q1, q2, k0, k1, k2, v0, v1, v2, ylg, gatt, kv0, kv1, kv2, ctail, hfin) = _front_call(
            yp, lw, l, TM_PROMPT)
        os_, lses = [], []
        for g, (qg, kg, vg) in enumerate(((q0, k0, v0), (q1, k1, v1), (q2, k2, v2))):
            tab = rel_bias[:, g * HEADS_PER_GROUP:(g + 1) * HEADS_PER_GROUP]
            o, lse = _attn_call(qg, kg, vg, tab, DILATED_GROUPS[g][1])
            os_.append(o)
            lses.append(lse)
        yp, o_t = _back_call(yp, ylg, gatt, os_, lses, lw, l, TM_PROMPT, qt, kt, vt, caches_t, rel_bias)
        for g, kv in enumerate((kv0, kv1, kv2)):
            kvp[g].append(kv.reshape(B, kv.shape[1], 2, HEADS_PER_GROUP, HEAD_DIM))
        conv_p.append(ctail[:, SUBLANES - (CONV_WIDTH - 1):, :])
        h_p.append(hfin[:, 0, :])

        ys = _dec_back_call(ys, ylg_s, gatt_s, o_t, lw, l)
        for g in range(N_GROUPS):
            sl = slice(g * GROUP_WIDTH, (g + 1) * GROUP_WIDTH)
            kvs[g].append(jnp.stack([ks[:, sl], vs[:, sl]], axis=1).reshape(
                nb, 1, 2, HEADS_PER_GROUP, HEAD_DIM))
        conv_s.append(jnp.stack(conv_rows[1:] + [lrux_s], axis=1))
        h_s.append(hs_s)

    stack = jnp.stack
    return (yp, ys.reshape(nb, 1, D_MODEL),
            stack(kvp[0]), stack(kvp[1]), stack(kvp[2]), stack(conv_p), stack(h_p),
            stack(kvs[0]), stack(kvs[1]), stack(kvs[2]), stack(conv_s), stack(h_s))
```
